```python
import jax, jax.numpy as jnp
from jax import lax
import numpy as np

D_MODEL = 1024
BATCH = 16
SEQ = 256
DEPTH = 4
DEC_BATCH = 2
DEC_SEQ = 1024
PAST_LEN = 512

GRID_W = 64
HEAD_DIM = 64
N_Q_HEADS = 8
N_KV_HEADS = 2
Q_PER_KV = N_Q_HEADS // N_KV_HEADS
ATTN_WIDTH = N_Q_HEADS * HEAD_DIM
KV_WIDTH = N_KV_HEADS * HEAD_DIM
WINDOW = 128
BLOCK = 128
ROPE_BASE = 10000.0
N_RET_HEADS = 4
RET_DK = 128
RET_DV = 128
RET_QK_WIDTH = N_RET_HEADS * RET_DK
RET_WIDTH = N_RET_HEADS * RET_DV
CHUNK = 128
MIX_WIDTH = ATTN_WIDTH + RET_WIDTH
IN_SPLITS = (ATTN_WIDTH, KV_WIDTH, KV_WIDTH, RET_QK_WIDTH, RET_QK_WIDTH, RET_WIDTH, RET_WIDTH)
IN_WIDTH = ATTN_WIDTH + 2 * KV_WIDTH + 2 * RET_QK_WIDTH + 2 * RET_WIDTH
N_EXPERTS = 32
TOP_K = 4
D_FF = D_MODEL
SWIGLU_LIMIT = 7.0
SWIGLU_ALPHA = 1.702
MOE_BLOCK = 128
EPS = 1e-6

kernel_name = 'hymba_retention_moe_diffusion_step'


def _rmsnorm(x, g):
    xf = x.astype(jnp.float32)
    y = xf * lax.rsqrt(jnp.mean(xf * xf, axis=-1, keepdims=True) + EPS)
    return (y * g.astype(jnp.float32)).astype(x.dtype)


def _modulation(cond, w_ada, b_ada):
    mod = jax.nn.silu(cond) @ w_ada + b_ada
    return jnp.split(mod[:, None, :], 6, axis=-1)


def _split_in(p):
    offs = np.cumsum(IN_SPLITS)[:-1].tolist()
    return jnp.split(p, offs, axis=-1)


def _axial_rope(x):
    n = x.shape[1]
    t = jnp.arange(n)
    row = t // GRID_W
    col = t % GRID_W
    half = HEAD_DIM // 2
    nf = half // 2
    inv = ROPE_BASE ** (-jnp.arange(nf, dtype=jnp.float32) / nf)

    def rot(xh, coord):
        ang = coord.astype(jnp.float32)[:, None] * inv[None, :]
        cos = jnp.cos(ang)[None, :, None, :].astype(x.dtype)
        sin = jnp.sin(ang)[None, :, None, :].astype(x.dtype)
        x1, x2 = xh[..., :nf], xh[..., nf:]
        return jnp.concatenate([x1 * cos - x2 * sin, x2 * cos + x1 * sin], axis=-1)

    return jnp.concatenate([rot(x[..., :half], row), rot(x[..., half:], col)], axis=-1)


def _sink_softmax(s, sink):
    m = jnp.maximum(jnp.max(s, axis=-1, keepdims=True), sink)
    e = jnp.exp(s - m)
    return e / (jnp.sum(e, axis=-1, keepdims=True) + jnp.exp(sink - m))


def _context_attention(q, k, v, sink):
    B, C, H, Dh = q.shape
    nb = C // BLOCK
    scale = Dh ** -0.5
    qb = q.reshape(B, nb, BLOCK, N_KV_HEADS, Q_PER_KV, Dh).transpose(1, 0, 2, 3, 4, 5)
    sink_f = sink.astype(jnp.float32).reshape(N_KV_HEADS, Q_PER_KV)[None, :, :, None, None]

    def one_block(qblk):
        s = jnp.einsum('bqkgd,bckd->bkgqc', qblk, k).astype(jnp.float32) * scale
        p = _sink_softmax(s, sink_f).astype(v.dtype)
        return jnp.einsum('bkgqc,bckd->bqkgd', p, v)

    o = lax.map(one_block, qb)
    return o.transpose(1, 0, 2, 3, 4, 5).reshape(B, C, H, Dh)


def _latent_attention(q, k, v, ck, cv, sink):
    B, N, H, Dh = q.shape
    nb = N // BLOCK
    scale = Dh ** -0.5
    qb = q.reshape(B, nb, BLOCK, N_KV_HEADS, Q_PER_KV, Dh)

    def band(t):
        tp = jnp.pad(t, ((0, 0), (BLOCK, BLOCK), (0, 0), (0, 0))).reshape(B, nb + 2, BLOCK, N_KV_HEADS, Dh)
        return jnp.concatenate([tp[:, :-2], tp[:, 1:-1], tp[:, 2:]], axis=2)

    kb, vb = band(k), band(v)
    blk = jnp.arange(nb)[:, None] * BLOCK
    qpos = blk + jnp.arange(BLOCK)[None, :]
    kpos = blk - BLOCK + jnp.arange(3 * BLOCK)[None, :]
    valid = ((jnp.abs(qpos[:, :, None] - kpos[:, None, :]) <= WINDOW)
             & (kpos[:, None, :] >= 0) & (kpos[:, None, :] < N))
    s_loc = jnp.einsum('bnqkgd,bnjkd->bkgnqj', qb, kb).astype(jnp.float32) * scale
    s_loc = jnp.where(valid[None, None, None], s_loc, -jnp.inf)
    s_ctx = jnp.einsum('bnqkgd,bckd->bkgnqc', qb, ck).astype(jnp.float32) * scale
    s = jnp.concatenate([s_loc, s_ctx], axis=-1)
    sink_f = sink.astype(jnp.float32).reshape(N_KV_HEADS, Q_PER_KV)[None, :, :, None, None, None]
    p = _sink_softmax(s, sink_f).astype(v.dtype)
    L = 3 * BLOCK
    o = (jnp.einsum('bkgnqj,bnjkd->bnqkgd', p[..., :L], vb)
         + jnp.einsum('bkgnqc,bckd->bnqkgd', p[..., L:], cv))
    return o.reshape(B, N, H, Dh)


def _retention_chunkwise(q, k, v, log_gamma, s0):
    B, N, H, Dk = q.shape
    Dv = v.shape[-1]
    nc = N // CHUNK
    dt = q.dtype
    pos = jnp.arange(CHUNK, dtype=jnp.float32)
    rel = pos[:, None] - pos[None, :]
    intra = jnp.exp(jnp.where(rel[None] >= 0, log_gamma[:, None, None] * rel[None], -jnp.inf)).astype(dt)
    q_dec = jnp.exp(log_gamma[None, :] * (pos[:, None] + 1.0)).astype(dt)
    k_dec = jnp.exp(log_gamma[None, :] * (CHUNK - 1.0 - pos[:, None])).astype(dt)
    chunk_dec = jnp.exp(log_gamma * CHUNK).astype(dt)

    def to_chunks(t):
        return t.reshape(B, nc, CHUNK, H, t.shape[-1]).transpose(1, 0, 2, 3, 4)

    def step(S, qkv):
        qc, kc, vc = qkv
        scores = jnp.einsum('bihd,bjhd->bhij', qc, kc) * intra
        o = (jnp.einsum('bhij,bjhv->bihv', scores, vc)
             + jnp.einsum('bihd,bhdv->bihv', qc, S) * q_dec[None, :, :, None])
        S = S * chunk_dec[None, :, None, None] + jnp.einsum('bjhd,bjhv->bhdv', kc * k_dec[None, :, :, None], vc)
        return S, o

    S, o = lax.scan(step, s0.astype(dt), (to_chunks(q), to_chunks(k), to_chunks(v)))
    return o.transpose(1, 0, 2, 3, 4).reshape(B, N, H, Dv), S


def _bidir_retention(q, k, v, log_gamma, s0):
    o_f, s_f = _retention_chunkwise(q, k, v, log_gamma[0], s0[:, 0])
    o_b, s_b = _retention_chunkwise(q[:, ::-1], k[:, ::-1], v[:, ::-1], log_gamma[1], s0[:, 1])
    return o_f + o_b[:, ::-1], jnp.stack([s_f, s_b], axis=1)


def _token_mixers(h, p, latent_ctx):
    B, N, _ = h.shape
    aq, ak, av, rq, rk, rv, rg = _split_in(h @ p['w_in'])
    q = _rmsnorm(aq.reshape(B, N, N_Q_HEADS, HEAD_DIM), p['q_norm'])
    k = _rmsnorm(ak.reshape(B, N, N_KV_HEADS, HEAD_DIM), p['k_norm'])
    v = av.reshape(B, N, N_KV_HEADS, HEAD_DIM)
    rq = rq.reshape(B, N, N_RET_HEADS, RET_DK) * (RET_DK ** -0.5)
    rk = rk.reshape(B, N, N_RET_HEADS, RET_DK)
    rv = rv.reshape(B, N, N_RET_HEADS, RET_DV)
    log_gamma = jax.nn.log_sigmoid(p['ret_decay'].astype(jnp.float32))
    if latent_ctx is None:
        att = _context_attention(q, k, v, p['attn_sink'])
        s0 = jnp.zeros((B, 2, N_RET_HEADS, RET_DK, RET_DV), rq.dtype)
    else:
        ck, cv, s0 = latent_ctx
        att = _latent_attention(_axial_rope(q), _axial_rope(k), v, ck, cv, p['attn_sink'])
    ret, s_fin = _bidir_retention(rq, rk, rv, log_gamma, s0)
    ret = _rmsnorm(ret, p['ret_norm'].reshape(N_RET_HEADS, RET_DV)).reshape(B, N, RET_WIDTH) * jax.nn.silu(rg)
    y = jnp.concatenate([att.reshape(B, N, ATTN_WIDTH), ret], axis=-1) @ p['w_out']
    return y, k, v, s_fin


def _moe(h, p):
    B, N, D = h.shape
    T = B * N
    A = T * TOP_K
    x = h.reshape(T, D)
    logits = (x @ p['w_router'] + p['b_router']).astype(jnp.float32)
    top_logit, top_idx = lax.top_k(logits, TOP_K)
    top_w = jax.nn.softmax(top_logit, axis=-1).astype(x.dtype)
    flat_e = top_idx.reshape(A)
    order = jnp.argsort(flat_e)
    e_sorted = flat_e[order]
    tok_sorted = order // TOP_K
    w_sorted = top_w.reshape(A)[order]
    counts = jnp.bincount(flat_e, length=N_EXPERTS)
    padded = (counts + MOE_BLOCK - 1) // MOE_BLOCK * MOE_BLOCK
    pad_end = jnp.cumsum(padded)
    start = jnp.cumsum(counts) - counts
    dest = (pad_end - padded)[e_sorted] + jnp.arange(A) - start[e_sorted]
    n_blocks = -(-A // MOE_BLOCK) + N_EXPERTS
    buf = jnp.zeros((n_blocks * MOE_BLOCK, D), x.dtype).at[dest].set(x[tok_sorted])
    block_expert = jnp.minimum(jnp.searchsorted(pad_end, jnp.arange(n_blocks) * MOE_BLOCK, side='right'),
                               N_EXPERTS - 1)
    w_gu, b_gu, w_dn, b_dn = p['w_gate_up'], p['b_gate_up'], p['w_down'], p['b_down']

    def expert_block(args):
        xb, e = args
        gu = xb @ w_gu[e] + b_gu[e]
        x_glu = jnp.minimum(gu[:, :D_FF], SWIGLU_LIMIT)
        x_lin = jnp.clip(gu[:, D_FF:], -SWIGLU_LIMIT, SWIGLU_LIMIT)
        act = x_glu * jax.nn.sigmoid(SWIGLU_ALPHA * x_glu) * (x_lin + 1.0)
        return act @ w_dn[e] + b_dn[e]

    out = lax.map(expert_block, (buf.reshape(n_blocks, MOE_BLOCK, D), block_expert))
    out = out.reshape(n_blocks * MOE_BLOCK, D)[dest] * w_sorted[:, None]
    y = jnp.zeros((T, D), x.dtype).at[tok_sorted].add(out)
    return y.reshape(B, N, D)


def _layer(x, cond, p, latent_ctx):
    sh1, sc1, g1, sh2, sc2, g2 = _modulation(cond, p['w_ada'], p['b_ada'])
    h = _rmsnorm(x, p['norm_mix']) * (1.0 + sc1) + sh1
    y, k, v, s_fin = _token_mixers(h, p, latent_ctx)
    x = x + g1 * y
    h = _rmsnorm(x, p['norm_ffn']) * (1.0 + sc2) + sh2
    x = x + g2 * _moe(h, p)
    return x, k, v, s_fin


def setup_inputs(seed: int = 0) -> dict:
    key = jax.random.key(seed)
    ks = jax.random.split(key, 24)
    f32 = jnp.float32

    def nrm(k, shape, scale):
        return jax.random.normal(k, shape, f32) * scale

    decay_base = jnp.log(2.0 ** (5.0 + jnp.arange(N_RET_HEADS, dtype=f32)) - 1.0)
    return {
        'x_prompt': nrm(ks[0], (BATCH, SEQ, D_MODEL), 1.0),
        'x_sample': nrm(ks[1], (DEC_BATCH, DEC_SEQ, D_MODEL), 1.0),
        'cache_attn_k': nrm(ks[2], (DEC_BATCH, DEPTH, PAST_LEN, N_KV_HEADS, HEAD_DIM), 1.0),
        'cache_attn_v': nrm(ks[3], (DEC_BATCH, DEPTH, PAST_LEN, N_KV_HEADS, HEAD_DIM), 1.0),
        'state_ret': nrm(ks[4], (DEC_BATCH, DEPTH, 2, N_RET_HEADS, RET_DK, RET_DV), 1.0),
        'c': nrm(ks[5], (DEC_BATCH, D_MODEL), 1.0),
        'c_ctx': nrm(ks[6], (D_MODEL,), 1.0),
        'norm_mix': 1.0 + nrm(ks[7], (DEPTH, D_MODEL), 0.02),
        'norm_ffn': 1.0 + nrm(ks[8], (DEPTH, D_MODEL), 0.02),
        'w_ada': nrm(ks[9], (DEPTH, D_MODEL, 6 * D_MODEL), 0.5 * D_MODEL ** -0.5),
        'b_ada': nrm(ks[10], (DEPTH, 6 * D_MODEL), 0.02),
        'w_in': nrm(ks[11], (DEPTH, D_MODEL, IN_WIDTH), D_MODEL ** -0.5),
        'q_norm': 1.0 + nrm(ks[12], (DEPTH, HEAD_DIM), 0.02),
        'k_norm': 1.0 + nrm(ks[13], (DEPTH, HEAD_DIM), 0.02),
        'attn_sink': nrm(ks[14], (DEPTH, N_Q_HEADS), 0.5),
        'ret_decay': decay_base + nrm(ks[15], (DEPTH, 2, N_RET_HEADS), 0.1),
        'ret_norm': 1.0 + nrm(ks[16], (DEPTH, RET_WIDTH), 0.02),
        'w_out': nrm(ks[17], (DEPTH, MIX_WIDTH, D_MODEL), MIX_WIDTH ** -0.5),
        'w_router': nrm(ks[18], (DEPTH, D_MODEL, N_EXPERTS), D_MODEL ** -0.5),
        'b_router': nrm(ks[19], (DEPTH, N_EXPERTS), 0.01),
        'w_gate_up': nrm(ks[20], (DEPTH, N_EXPERTS, D_MODEL, 2 * D_FF), D_MODEL ** -0.5),
        'b_gate_up': nrm(ks[21], (DEPTH, N_EXPERTS, 2 * D_FF), 0.02),
        'w_down': nrm(ks[22], (DEPTH, N_EXPERTS, D_FF, D_MODEL), D_FF ** -0.5),
        'b_down': nrm(ks[23], (DEPTH, N_EXPERTS, D_MODEL), 0.02),
    }


def reference(x_prompt, x_sample, cache_attn_k, cache_attn_v, state_ret, c, c_ctx,
              norm_mix, norm_ffn, w_ada, b_ada, w_in, q_norm, k_norm, attn_sink, ret_decay, ret_norm,
              w_out, w_router, b_router, w_gate_up, b_gate_up, w_down, b_down):
    ctx_cond = c_ctx[None, :]
    xp = x_prompt
    xs = x_sample
    new_k, new_v, new_s = [], [], []
    for l in range(DEPTH):
        p = {
            'norm_mix': norm_mix[l], 'norm_ffn': norm_ffn[l], 'w_ada': w_ada[l], 'b_ada': b_ada[l],
            'w_in': w_in[l], 'q_norm': q_norm[l], 'k_norm': k_norm[l], 'attn_sink': attn_sink[l],
            'ret_decay': ret_decay[l], 'ret_norm': ret_norm[l], 'w_out': w_out[l],
            'w_router': w_router[l], 'b_router': b_router[l], 'w_gate_up': w_gate_up[l],
            'b_gate_up': b_gate_up[l], 'w_down': w_down[l], 'b_down': b_down[l],
        }
        xp, k_l, v_l, s_l = _layer(xp, ctx_cond, p, None)
        new_k.append(k_l)
        new_v.append(v_l)
        new_s.append(s_l)
        xs, _, _, _ = _layer(xs, c, p, (cache_attn_k[:, l], cache_attn_v[:, l], state_ret[:, l]))
    new_attn_k = jnp.stack(new_k, axis=1)
    new_attn_v = jnp.stack(new_v, axis=1)
    new_state_ret = jnp.stack(new_s, axis=1)
    return (xp, xs, new_attn_k, new_attn_v, new_state_ret)
```

```python
import functools

import jax
import jax.numpy as jnp
from jax import lax
from jax.experimental import pallas as pl
from jax.experimental.pallas import tpu as pltpu

F32 = jnp.float32
BF16 = jnp.bfloat16

D_MODEL = 1024
DEPTH = 4
BATCH, SEQ = 16, 256
DEC_BATCH, DEC_SEQ = 2, 1024
PAST_LEN = 512
GRID_W = 64
HEAD_DIM = 64
N_Q_HEADS = 8
N_KV_HEADS = 2
ATTN_WIDTH = N_Q_HEADS * HEAD_DIM
KV_WIDTH = N_KV_HEADS * HEAD_DIM
WINDOW = 128
BLOCK = 128
ROPE_BASE = 10000.0
N_RET_HEADS = 4
RET_DK = 128
RET_WIDTH = N_RET_HEADS * RET_DK
CHUNK = 128
IN_WIDTH = ATTN_WIDTH + 2 * KV_WIDTH + 4 * RET_WIDTH
N_EXPERTS = 32
TOP_K = 4
D_FF = D_MODEL
SWIGLU_LIMIT = 7.0
SWIGLU_ALPHA = 1.702
EPS = 1e-6

T_CTX = BATCH * SEQ
T_LAT = DEC_BATCH * DEC_SEQ
T_ALL = T_CTX + T_LAT
N_COND = 1 + DEC_BATCH
COND_PAD = 8
LANES = 128
ROW_TILE = 512
MOE_ROWS = 128
N_ASSIGN = T_ALL * TOP_K
N_MOE_BLOCKS = N_ASSIGN // MOE_ROWS + N_EXPERTS
COMBINE_ROWS = 256
NEG_BIG = -1e30
VMEM_LIMIT = 48 * 1024 * 1024


def _cond_of_tile(i):
    return jnp.where(i < T_CTX // ROW_TILE, 0, 1 + (i - T_CTX // ROW_TILE) // (DEC_SEQ // ROW_TILE))


def _params(sem, vmem=VMEM_LIMIT):
    return pltpu.CompilerParams(dimension_semantics=sem, vmem_limit_bytes=vmem)


def _mod_kernel(c_ref, w_ref, b_ref, o_ref):
    c = c_ref[...]
    s = (c * jax.nn.sigmoid(c)).astype(BF16)
    o_ref[0] = jnp.dot(s, w_ref[0].astype(BF16), preferred_element_type=F32) + b_ref[0]


def _modulation(cond, w_ada, b_ada):
    n_col = 6 * D_MODEL // D_MODEL
    return pl.pallas_call(
        _mod_kernel,
        grid=(DEPTH, n_col),
        in_specs=[
            pl.BlockSpec((COND_PAD, D_MODEL), lambda l, j: (0, 0)),
            pl.BlockSpec((1, D_MODEL, D_MODEL), lambda l, j: (l, 0, j)),
            pl.BlockSpec((1, 1, D_MODEL), lambda l, j: (l, 0, j)),
        ],
        out_specs=pl.BlockSpec((1, COND_PAD, D_MODEL), lambda l, j: (l, 0, j)),
        out_shape=jax.ShapeDtypeStruct((DEPTH, COND_PAD, 6 * D_MODEL), F32),
        compiler_params=_params(("arbitrary", "arbitrary")),
        name="modulation",
    )(cond, w_ada, b_ada.reshape(DEPTH, 1, 6 * D_MODEL))


def _rms_rows(x, g):
    ms = jnp.mean(x * x, axis=-1, keepdims=True)
    return x * lax.rsqrt(ms + EPS) * g


def _group_rmsnorm(a, avg_ref, g):
    sq = a * a
    hi = sq.astype(BF16)
    lo = (sq - hi.astype(F32)).astype(BF16)
    avg = avg_ref[...]
    ms = jnp.dot(hi, avg, preferred_element_type=F32) + jnp.dot(lo, avg, preferred_element_type=F32)
    return a * lax.rsqrt(ms + EPS) * g


def _inproj_kernel(x_ref, g_ref, sh_ref, sc_ref, w_ref, qn_ref, kn_ref, avgq_ref, avgk_ref,
                   q_ref, k_ref, v_ref, rq_ref, rk_ref, rv_ref, sg_ref):
    h = _rms_rows(x_ref[...], g_ref[...]) * (1.0 + sc_ref[0]) + sh_ref[0]
    hb = h.astype(BF16)

    def proj(lo, width):
        return jnp.dot(hb, w_ref[:, lo:lo + width], preferred_element_type=F32)

    o = 0
    q_ref[...] = _group_rmsnorm(proj(o, ATTN_WIDTH), avgq_ref, qn_ref[...])
    o += ATTN_WIDTH
    k_ref[...] = _group_rmsnorm(proj(o, KV_WIDTH), avgk_ref, kn_ref[...])
    o += KV_WIDTH
    v_ref[...] = proj(o, KV_WIDTH)
    o += KV_WIDTH
    rq_ref[...] = proj(o, RET_WIDTH) * (RET_DK ** -0.5)
    o += RET_WIDTH
    rk_ref[...] = proj(o, RET_WIDTH)
    o += RET_WIDTH
    rv_ref[...] = proj(o, RET_WIDTH)
    o += RET_WIDTH
    rg = proj(o, RET_WIDTH)
    sg_ref[...] = rg * jax.nn.sigmoid(rg)


def _mod_spec(col):
    return pl.BlockSpec((1, 1, D_MODEL), lambda i, col=col: (_cond_of_tile(i), 0, col))


def _full(shape):
    return pl.BlockSpec(shape, lambda *_: (0,) * len(shape))


def _input_projection(x, norm_g, mod_l, w_in_bf, qn, kn, avgq, avgk):
    rows = lambda w: pl.BlockSpec((ROW_TILE, w), lambda i: (i, 0))
    widths = (ATTN_WIDTH, KV_WIDTH, KV_WIDTH, RET_WIDTH, RET_WIDTH, RET_WIDTH, RET_WIDTH)
    return pl.pallas_call(
        _inproj_kernel,
        grid=(T_ALL // ROW_TILE,),
        in_specs=[rows(D_MODEL), _full((1, D_MODEL)), _mod_spec(0), _mod_spec(1),
                  _full((D_MODEL, IN_WIDTH)), _full((1, ATTN_WIDTH)), _full((1, KV_WIDTH)),
                  _full((ATTN_WIDTH, ATTN_WIDTH)), _full((KV_WIDTH, KV_WIDTH))],
        out_specs=[rows(w) for w in widths],
        out_shape=[jax.ShapeDtypeStruct((T_ALL, w), F32) for w in widths],
        compiler_params=_params(("arbitrary",)),
        name="norm_inproj",
    )(x, norm_g, mod_l, mod_l, w_in_bf, qn, kn, avgq, avgk)


def _attend(q, kall, vall, valid, sink_ref, o_ref):
    scale = HEAD_DIM ** -0.5
    lane = lax.broadcasted_iota(jnp.int32, (1, LANES), 1)
    low = lane < HEAD_DIM
    k_at, v_at = [], []
    for g in range(N_KV_HEADS):
        keep = low if g == 0 else jnp.logical_not(low)
        kg = jnp.where(keep, kall, 0.0)
        vg = jnp.where(keep, vall, 0.0)
        kr = pltpu.roll(kg, HEAD_DIM, 1)
        vr = pltpu.roll(vg, HEAD_DIM, 1)
        pair_k = (kg, kr) if g == 0 else (kr, kg)
        pair_v = (vg, vr) if g == 0 else (vr, vg)
        k_at.append([t.astype(BF16) for t in pair_k])
        v_at.append([t.astype(BF16) for t in pair_v])
    for j in range(ATTN_WIDTH // LANES):
        qj = q[:, j * LANES:(j + 1) * LANES].astype(BF16)
        acc = None
        for off in range(2):
            h = 2 * j + off
            g = h // (N_Q_HEADS // N_KV_HEADS)
            s = lax.dot_general(qj, k_at[g][off], (((1,), (1,)), ((), ())), preferred_element_type=F32) * scale
            if valid is not None:
                s = jnp.where(valid, s, NEG_BIG)
            sink = sink_ref[h]
            m = jnp.maximum(jnp.max(s, axis=-1, keepdims=True), sink)
            e = jnp.exp(s - m)
            den = jnp.sum(e, axis=-1, keepdims=True) + jnp.exp(sink - m)
            o = jnp.dot(e.astype(BF16), v_at[g][off], preferred_element_type=F32) / den
            acc = o if acc is None else acc + o
        o_ref[:, j * LANES:(j + 1) * LANES] = acc


def _ctx_attn_kernel(sink_ref, q_ref, k_ref, v_ref, o_ref):
    _attend(q_ref[...], k_ref[...], v_ref[...], None, sink_ref, o_ref)


def _context_attention(sink, q, k, v):
    return pl.pallas_call(
        _ctx_attn_kernel,
        grid_spec=pltpu.PrefetchScalarGridSpec(
            num_scalar_prefetch=1,
            grid=(BATCH,),
            in_specs=[pl.BlockSpec((SEQ, ATTN_WIDTH), lambda b, s: (b, 0)),
                      pl.BlockSpec((SEQ, KV_WIDTH), lambda b, s: (b, 0)),
                      pl.BlockSpec((SEQ, KV_WIDTH), lambda b, s: (b, 0))],
            out_specs=pl.BlockSpec((SEQ, ATTN_WIDTH), lambda b, s: (b, 0)),
        ),
        out_shape=jax.ShapeDtypeStruct((T_CTX, ATTN_WIDTH), F32),
        compiler_params=_params(("arbitrary",)),
        name="context_attention",
    )(sink, q, k, v)


def _rope_block(x, cos, sin_signed):
    lane = lax.broadcasted_iota(jnp.int32, (1, LANES), 1)
    first = (lane % (HEAD_DIM // 2)) < (HEAD_DIM // 4)
    swapped = jnp.where(first, pltpu.roll(x, LANES - HEAD_DIM // 4, 1), pltpu.roll(x, HEAD_DIM // 4, 1))
    return x * cos + swapped * sin_signed


LOCAL_KEYS = 3 * BLOCK


def _lat_attn_kernel(sink_ref, q_ref, k_ref, v_ref, ck_ref, cv_ref, cosq_ref, sinq_ref, cosk_ref, sin_k_ref,
                     o_ref):
    n = pl.program_id(1)
    start = pl.multiple_of(jnp.clip((n - 1) * BLOCK, 0, DEC_SEQ - LOCAL_KEYS), BLOCK)
    q = q_ref[...]
    q = jnp.concatenate(
        [_rope_block(q[:, j * LANES:(j + 1) * LANES], cosq_ref[:, j * LANES:(j + 1) * LANES],
                     sinq_ref[:, j * LANES:(j + 1) * LANES]) for j in range(ATTN_WIDTH // LANES)], axis=1)
    kw = _rope_block(k_ref[pl.ds(start, LOCAL_KEYS), :], cosk_ref[pl.ds(start, LOCAL_KEYS), :],
                     sin_k_ref[pl.ds(start, LOCAL_KEYS), :])
    vw = v_ref[pl.ds(start, LOCAL_KEYS), :]
    kall = jnp.concatenate([kw, ck_ref[0, 0]], axis=0)
    vall = jnp.concatenate([vw, cv_ref[0, 0]], axis=0)
    qpos = n * BLOCK + lax.broadcasted_iota(jnp.int32, (BLOCK, LOCAL_KEYS + PAST_LEN), 0)
    col = lax.broadcasted_iota(jnp.int32, (BLOCK, LOCAL_KEYS + PAST_LEN), 1)
    valid = jnp.logical_or(col >= LOCAL_KEYS, jnp.abs(qpos - (start + col)) <= WINDOW)
    _attend(q, kall, vall, valid, sink_ref, o_ref)


def _latent_attention(sink, q, k, v, cache_k, cache_v, layer, cosq, sinq, cosk, sin_k):
    nb = DEC_SEQ // BLOCK
    ctx_block0 = T_CTX // BLOCK
    ctx_seq0 = T_CTX // DEC_SEQ
    cache_spec = pl.BlockSpec((1, 1, PAST_LEN, KV_WIDTH), lambda b, n, s: (b, layer, 0, 0))
    return pl.pallas_call(
        _lat_attn_kernel,
        grid_spec=pltpu.PrefetchScalarGridSpec(
            num_scalar_prefetch=1,
            grid=(DEC_BATCH, nb),
            in_specs=[pl.BlockSpec((BLOCK, ATTN_WIDTH), lambda b, n, s: (ctx_block0 + b * nb + n, 0)),
                      pl.BlockSpec((DEC_SEQ, KV_WIDTH), lambda b, n, s: (ctx_seq0 + b, 0)),
                      pl.BlockSpec((DEC_SEQ, KV_WIDTH), lambda b, n, s: (ctx_seq0 + b, 0)),
                      cache_spec, cache_spec,
                      pl.BlockSpec((BLOCK, ATTN_WIDTH), lambda b, n, s: (n, 0)),
                      pl.BlockSpec((BLOCK, ATTN_WIDTH), lambda b, n, s: (n, 0)),
                      pl.BlockSpec((DEC_SEQ, KV_WIDTH), lambda b, n, s: (0, 0)),
                      pl.BlockSpec((DEC_SEQ, KV_WIDTH), lambda b, n, s: (0, 0))],
            out_specs=pl.BlockSpec((BLOCK, ATTN_WIDTH), lambda b, n, s: (b * nb + n, 0)),
        ),
        out_shape=jax.ShapeDtypeStruct((T_LAT, ATTN_WIDTH), F32),
        compiler_params=_params(("arbitrary", "arbitrary")),
        name="latent_attention",
    )(sink, q, k, v, cache_k, cache_v, cosq, sinq, cosk, sin_k)


def _rope_tables():
    t = jnp.arange(DEC_SEQ)
    nf = HEAD_DIM // 4
    inv = ROPE_BASE ** (-jnp.arange(nf, dtype=F32) / nf)

    def half(coord):
        ang = coord.astype(F32)[:, None] * inv[None, :]
        c, s = jnp.cos(ang), jnp.sin(ang)
        return jnp.concatenate([c, c], axis=1), jnp.concatenate([-s, s], axis=1)

    cr, sr = half(t // GRID_W)
    cc, sc = half(t % GRID_W)
    cos = jnp.concatenate([cr, cc], axis=1)
    sin = jnp.concatenate([sr, sc], axis=1)
    return (jnp.tile(cos, (1, N_Q_HEADS)), jnp.tile(sin, (1, N_Q_HEADS)),
            jnp.tile(cos, (1, N_KV_HEADS)), jnp.tile(sin, (1, N_KV_HEADS)))


def _ret_kernel(lg_ref, cd_ref, q_ref, k_ref, v_ref, sg_ref, gn_ref, *rest, n_chunks, has_s0, write_state):
    rest = list(rest)
    s0_ref = rest.pop(0) if has_s0 else None
    o_ref = rest.pop(0)
    sf_ref = rest.pop(0) if write_state else None
    acc_ref = rest.pop(0)
    h = pl.program_id(1)
    row = lax.broadcasted_iota(jnp.int32, (CHUNK, CHUNK), 0).astype(F32)
    col = lax.broadcasted_iota(jnp.int32, (CHUNK, CHUNK), 1).astype(F32)
    rel = row - col
    pos = lax.broadcasted_iota(jnp.int32, (CHUNK, 1), 0).astype(F32)
    gn = gn_ref[...]

    def run(direction):
        lg = lg_ref[direction * N_RET_HEADS + h]
        cd = cd_ref[direction * N_RET_HEADS + h]
        if direction == 0:
            intra = jnp.where(rel >= 0, jnp.exp(lg * rel), 0.0)
            q_dec = jnp.exp(lg * (pos + 1.0))
            k_dec = jnp.exp(lg * (CHUNK - 1.0 - pos))
            order = range(n_chunks)
        else:
            intra = jnp.where(rel <= 0, jnp.exp(-lg * rel), 0.0)
            q_dec = jnp.exp(lg * (CHUNK - pos))
            k_dec = jnp.exp(lg * pos)
            order = range(n_chunks - 1, -1, -1)
        state = s0_ref[0, 0, direction, 0] if has_s0 else jnp.zeros((RET_DK, RET_DK), F32)
        for c in order:
            rows = slice(c * CHUNK, (c + 1) * CHUNK)
            qc, kc, vc = q_ref[rows, :], k_ref[rows, :], v_ref[rows, :]
            qb, kb, vb = qc.astype(BF16), kc.astype(BF16), vc.astype(BF16)
            scores = lax.dot_general(qb, kb, (((1,), (1,)), ((), ())), preferred_element_type=F32) * intra
            o = (jnp.dot(scores.astype(BF16), vb, preferred_element_type=F32)
                 + jnp.dot(qb, state.astype(BF16), preferred_element_type=F32) * q_dec)
            kd_t = (kc * k_dec).T.astype(BF16)
            state = state * cd + jnp.dot(kd_t, vb, preferred_element_type=F32)
            if direction == 0:
                acc_ref[rows, :] = o
            else:
                tot = acc_ref[rows, :] + o
                o_ref[rows, :] = _rms_rows(tot, gn) * sg_ref[rows, :]
        if write_state:
            sf_ref[0, direction, 0] = state

    run(0)
    run(1)


def _retention(lg, cd, rq, rk, rv, sg, gn, s0, layer, *, n_seq, seq_len, row0, write_state):
    blk0 = row0 // seq_len
    rows = pl.BlockSpec((seq_len, RET_DK), lambda b, h, *_: (blk0 + b, h))
    in_specs = [rows, rows, rows, rows, pl.BlockSpec((1, RET_DK), lambda b, h, *_: (0, h))]
    args = [rq, rk, rv, sg, gn]
    if s0 is not None:
        in_specs.append(pl.BlockSpec((1, 1, 2, 1, RET_DK, RET_DK), lambda b, h, *_: (b, layer, 0, h, 0, 0)))
        args.append(s0)
    out_specs = [pl.BlockSpec((seq_len, RET_DK), lambda b, h, *_: (b, h))]
    out_shape = [jax.ShapeDtypeStruct((n_seq * seq_len, RET_WIDTH), F32)]
    if write_state:
        out_specs.append(pl.BlockSpec((1, 2, 1, RET_DK, RET_DK), lambda b, h, *_: (b, 0, h, 0, 0)))
        out_shape.append(jax.ShapeDtypeStruct((n_seq, 2, N_RET_HEADS, RET_DK, RET_DK), F32))
    kern = functools.partial(_ret_kernel, n_chunks=seq_len // CHUNK, has_s0=s0 is not None,
                             write_state=write_state)
    return pl.pallas_call(
        kern,
        grid_spec=pltpu.PrefetchScalarGridSpec(
            num_scalar_prefetch=2,
            grid=(n_seq, N_RET_HEADS),
            in_specs=in_specs,
            out_specs=out_specs,
            scratch_shapes=[pltpu.VMEM((seq_len, RET_DK), F32)],
        ),
        out_shape=out_shape,
        compiler_params=_params(("arbitrary", "arbitrary")),
        name="retention_ctx" if write_state else "retention_lat",
    )(lg, cd, *args)


def _outproj_kernel(att_ref, ret_ref, x_ref, wo_ref, g1_ref, sh2_ref, sc2_ref, nf_ref, wrh_ref, wrl_ref, br_ref,
                    x1_ref, h2_ref, ti_ref, tw_ref):
    y = (jnp.dot(att_ref[...].astype(BF16), wo_ref[0:ATTN_WIDTH, :], preferred_element_type=F32)
         + jnp.dot(ret_ref[...].astype(BF16), wo_ref[ATTN_WIDTH:, :], preferred_element_type=F32))
    x1 = x_ref[...] + g1_ref[0] * y
    x1_ref[...] = x1
    h2 = _rms_rows(x1, nf_ref[...]) * (1.0 + sc2_ref[0]) + sh2_ref[0]
    h2_ref[...] = h2
    hh = h2.astype(BF16)
    hl = (h2 - hh.astype(F32)).astype(BF16)
    wrh = wrh_ref[...]
    logits = (jnp.dot(hh, wrh, preferred_element_type=F32) + jnp.dot(hl, wrh, preferred_element_type=F32)
              + jnp.dot(hh, wrl_ref[...], preferred_element_type=F32) + br_ref[...])
    lane = lax.broadcasted_iota(jnp.int32, logits.shape, 1)
    vals, idxs = [], []
    cur = logits
    for _ in range(TOP_K):
        m = jnp.max(cur, axis=-1, keepdims=True)
        idx = jnp.min(jnp.where(cur == m, lane, LANES), axis=-1, keepdims=True)
        vals.append(m)
        idxs.append(idx)
        cur = jnp.where(lane == idx, -jnp.inf, cur)
    es = [jnp.exp(v - vals[0]) for v in vals]
    den = es[0] + es[1] + es[2] + es[3]
    ti = jnp.zeros(logits.shape, jnp.int32)
    tw = jnp.zeros(logits.shape, F32)
    for k in range(TOP_K):
        ti = jnp.where(lane == k, idxs[k], ti)
        tw = jnp.where(lane == k, es[k] / den, tw)
    ti_ref[...] = ti
    tw_ref[...] = tw


def _output_projection(att, ret, x, w_out_bf, mod_l, nf, wrh, wrl, br):
    rows = lambda w: pl.BlockSpec((ROW_TILE, w), lambda i: (i, 0))
    return pl.pallas_call(
        _outproj_kernel,
        grid=(T_ALL // ROW_TILE,),
        in_specs=[rows(ATTN_WIDTH), rows(RET_WIDTH), rows(D_MODEL), _full((D_MODEL, D_MODEL)),
                  _mod_spec(2), _mod_spec(3), _mod_spec(4), _full((1, D_MODEL)),
                  _full((D_MODEL, LANES)), _full((D_MODEL, LANES)), _full((1, LANES))],
        out_specs=[rows(D_MODEL), rows(D_MODEL), rows(LANES), rows(LANES)],
        out_shape=[jax.ShapeDtypeStruct((T_ALL, D_MODEL), F32), jax.ShapeDtypeStruct((T_ALL, D_MODEL), F32),
                   jax.ShapeDtypeStruct((T_ALL, LANES), jnp.int32), jax.ShapeDtypeStruct((T_ALL, LANES), F32)],
        compiler_params=_params(("arbitrary",)),
        name="outproj_router",
    )(att, ret, x, w_out_bf, mod_l, mod_l, mod_l, nf, wrh, wrl, br)


def _gather_copy(h_hbm, xbuf, sem, tok, slot, r):
    return pltpu.make_async_copy(h_hbm.at[pl.ds(tok, 1), :], xbuf.at[slot, pl.ds(r, 1), :], sem.at[slot])


def _moe_kernel(be_ref, src_ref, nact_ref, h_hbm, wgu_ref, bgu_ref, wdn_ref, bdn_ref, o_ref,
                xbuf, wgu_bf, wdn_bf, sem):
    b = pl.program_id(0)
    nact = nact_ref[0]
    slot = b % 2

    def start_gather(blk, slot_):
        def body(r, carry):
            _gather_copy(h_hbm, xbuf, sem, src_ref[blk * MOE_ROWS + r], slot_, r).start()
            return carry
        lax.fori_loop(0, MOE_ROWS, body, 0)

    @pl.when(jnp.logical_and(b == 0, nact > 0))
    def _():
        start_gather(0, 0)

    @pl.when(b + 1 < nact)
    def _():
        start_gather(b + 1, 1 - slot)

    @pl.when(b < nact)
    def _():
        pltpu.make_async_copy(h_hbm.at[pl.ds(0, MOE_ROWS), :], xbuf.at[slot], sem.at[slot]).wait()

        @pl.when(jnp.logical_or(b == 0, be_ref[b] != be_ref[jnp.maximum(b - 1, 0)]))
        def _():
            wgu_bf[...] = wgu_ref[...].astype(BF16)
            wdn_bf[...] = wdn_ref[...].astype(BF16)

        xb = xbuf[slot].astype(BF16)
        gu = jnp.dot(xb, wgu_bf[...], preferred_element_type=F32) + bgu_ref[...]
        x_glu = jnp.minimum(gu[:, :D_FF], SWIGLU_LIMIT)
        x_lin = jnp.clip(gu[:, D_FF:], -SWIGLU_LIMIT, SWIGLU_LIMIT)
        act = x_glu * jax.nn.sigmoid(SWIGLU_ALPHA * x_glu) * (x_lin + 1.0)
        o_ref[...] = jnp.dot(act.astype(BF16), wdn_bf[...], preferred_element_type=F32) + bdn_ref[...]

    @pl.when(b >= nact)
    def _():
        o_ref[...] = jnp.zeros(o_ref.shape, F32)


def _moe_experts(block_expert, src_tok, n_active, h2, w_gu, b_gu, w_dn, b_dn, layer):
    wspec = lambda rows, cols: pl.BlockSpec((None, None, rows, cols), lambda b, be, src, na: (layer, be[b], 0, 0))
    return pl.pallas_call(
        _moe_kernel,
        grid_spec=pltpu.PrefetchScalarGridSpec(
            num_scalar_prefetch=3,
            grid=(N_MOE_BLOCKS,),
            in_specs=[pl.BlockSpec(memory_space=pl.ANY),
                      wspec(D_MODEL, 2 * D_FF), wspec(1, 2 * D_FF), wspec(D_FF, D_MODEL), wspec(1, D_MODEL)],
            out_specs=pl.BlockSpec((MOE_ROWS, D_MODEL), lambda b, be, src, na: (b, 0)),
            scratch_shapes=[pltpu.VMEM((2, MOE_ROWS, D_MODEL), F32),
                            pltpu.VMEM((D_MODEL, 2 * D_FF), BF16),
                            pltpu.VMEM((D_FF, D_MODEL), BF16),
                            pltpu.SemaphoreType.DMA((2,))],
        ),
        out_shape=jax.ShapeDtypeStruct((N_MOE_BLOCKS * MOE_ROWS, D_MODEL), F32),
        compiler_params=_params(("arbitrary",)),
        name="moe_experts",
    )(block_expert, src_tok, n_active, h2, w_gu, b_gu.reshape(DEPTH, N_EXPERTS, 1, 2 * D_FF),
      w_dn, b_dn.reshape(DEPTH, N_EXPERTS, 1, D_MODEL))


def _combine_kernel(pos_ref, ys_hbm, x1_ref, tw_ref, g2_ref, o_ref, buf, sem):
    i = pl.program_id(0)

    def body(r, carry):
        for k in range(TOP_K):
            p = pos_ref[(i * COMBINE_ROWS + r) * TOP_K + k]
            pltpu.make_async_copy(ys_hbm.at[pl.ds(p, 1), :], buf.at[k, pl.ds(r, 1), :], sem.at[0]).start()
        return carry

    lax.fori_loop(0, COMBINE_ROWS, body, 0)
    for k in range(TOP_K):
        pltpu.make_async_copy(ys_hbm.at[pl.ds(0, COMBINE_ROWS), :], buf.at[k], sem.at[0]).wait()
    tw = tw_ref[...]
    y = tw[:, 0:1] * buf[0]
    for k in range(1, TOP_K):
        y = y + tw[:, k:k + 1] * buf[k]
    o_ref[...] = x1_ref[...] + g2_ref[0] * y


def _moe_combine(pos_flat, ys, x1, tw, mod_l):
    tiles_per_cond = ROW_TILE // COMBINE_ROWS
    return pl.pallas_call(
        _combine_kernel,
        grid_spec=pltpu.PrefetchScalarGridSpec(
            num_scalar_prefetch=1,
            grid=(T_ALL // COMBINE_ROWS,),
            in_specs=[pl.BlockSpec(memory_space=pl.ANY),
                      pl.BlockSpec((COMBINE_ROWS, D_MODEL), lambda i, p: (i, 0)),
                      pl.BlockSpec((COMBINE_ROWS, LANES), lambda i, p: (i, 0)),
                      pl.BlockSpec((1, 1, D_MODEL), lambda i, p: (_cond_of_tile(i // tiles_per_cond), 0, 5))],
            out_specs=pl.BlockSpec((COMBINE_ROWS, D_MODEL), lambda i, p: (i, 0)),
            scratch_shapes=[pltpu.VMEM((TOP_K, COMBINE_ROWS, D_MODEL), F32), pltpu.SemaphoreType.DMA((1,))],
        ),
        out_shape=jax.ShapeDtypeStruct((T_ALL, D_MODEL), F32),
        compiler_params=_params(("arbitrary",)),
        name="moe_combine",
    )(pos_flat, ys, x1, tw, mod_l)


def _routing_tables(top_idx):
    flat_e = top_idx.reshape(N_ASSIGN)
    order = jnp.argsort(flat_e, stable=True).astype(jnp.int32)
    e_sorted = flat_e[order]
    tok_sorted = order // TOP_K
    counts = jnp.bincount(flat_e, length=N_EXPERTS).astype(jnp.int32)
    padded = (counts + MOE_ROWS - 1) // MOE_ROWS * MOE_ROWS
    pad_end = jnp.cumsum(padded)
    pad_start = pad_end - padded
    start = jnp.cumsum(counts) - counts
    dest = pad_start[e_sorted] + jnp.arange(N_ASSIGN, dtype=jnp.int32) - start[e_sorted]
    pos_flat = jnp.zeros((N_ASSIGN,), jnp.int32).at[order].set(dest)
    blk_row0 = jnp.arange(N_MOE_BLOCKS, dtype=jnp.int32) * MOE_ROWS
    block_expert = jnp.minimum(jnp.searchsorted(pad_end, blk_row0, side='right'), N_EXPERTS - 1).astype(jnp.int32)
    n_active = (pad_end[-1] // MOE_ROWS).astype(jnp.int32).reshape(1)
    row = jnp.arange(N_MOE_BLOCKS * MOE_ROWS, dtype=jnp.int32)
    row_e = jnp.repeat(block_expert, MOE_ROWS)
    rank = row - pad_start[row_e]
    is_real = jnp.logical_and(rank < counts[row_e], row < pad_end[-1])
    src_tok = jnp.where(is_real, tok_sorted[jnp.clip(start[row_e] + rank, 0, N_ASSIGN - 1)], 0).astype(jnp.int32)
    return block_expert, src_tok, n_active, pos_flat


def kernel(x_prompt, x_sample, cache_attn_k, cache_attn_v, state_ret, c, c_ctx, norm_mix, norm_ffn, w_ada, b_ada,
           w_in, q_norm, k_norm, attn_sink, ret_decay, ret_norm, w_out, w_router, b_router, w_gate_up, b_gate_up,
           w_down, b_down):
    x = jnp.concatenate([x_prompt.reshape(T_CTX, D_MODEL), x_sample.reshape(T_LAT, D_MODEL)], axis=0)
    cond = jnp.zeros((COND_PAD, D_MODEL), F32).at[0].set(c_ctx).at[1:N_COND].set(c)
    mod = _modulation(cond, w_ada, b_ada)[:, :N_COND].reshape(DEPTH, N_COND, 1, 6 * D_MODEL)

    cache_k = cache_attn_k.reshape(DEC_BATCH, DEPTH, PAST_LEN, KV_WIDTH)
    cache_v = cache_attn_v.reshape(DEC_BATCH, DEPTH, PAST_LEN, KV_WIDTH)
    cosq, sinq, cosk, sin_k = _rope_tables()
    grp = jnp.arange(ATTN_WIDTH) // HEAD_DIM
    avgq = jnp.where(grp[:, None] == grp[None, :], 1.0 / HEAD_DIM, 0.0).astype(BF16)
    avgk = avgq[:KV_WIDTH, :KV_WIDTH]
    log_gamma = jax.nn.log_sigmoid(ret_decay.astype(F32))
    chunk_decay = jnp.exp(log_gamma * CHUNK)

    new_k, new_v, new_s = [], [], []
    for l in range(DEPTH):
        mod_l = mod[l]
        q, k, v, rq, rk, rv, sg = _input_projection(
            x, norm_mix[l].reshape(1, D_MODEL), mod_l, w_in[l].astype(BF16),
            jnp.tile(q_norm[l], N_Q_HEADS).reshape(1, ATTN_WIDTH), jnp.tile(k_norm[l], N_KV_HEADS).reshape(1, KV_WIDTH),
            avgq, avgk)
        new_k.append(k[:T_CTX].reshape(BATCH, SEQ, N_KV_HEADS, HEAD_DIM))
        new_v.append(v[:T_CTX].reshape(BATCH, SEQ, N_KV_HEADS, HEAD_DIM))
        sink = attn_sink[l].astype(F32)
        att_c = _context_attention(sink, q, k, v)
        att_l = _latent_attention(sink, q, k, v, cache_k, cache_v, l, cosq, sinq, cosk, sin_k)
        lg = log_gamma[l].reshape(2 * N_RET_HEADS)
        cd = chunk_decay[l].reshape(2 * N_RET_HEADS)
        gn = ret_norm[l].reshape(1, RET_WIDTH)
        ret_c, s_fin = _retention(lg, cd, rq, rk, rv, sg, gn, None, l, n_seq=BATCH, seq_len=SEQ, row0=0,
                                  write_state=True)
        (ret_l,) = _retention(lg, cd, rq, rk, rv, sg, gn, state_ret, l, n_seq=DEC_BATCH, seq_len=DEC_SEQ,
                              row0=T_CTX, write_state=False)
        new_s.append(s_fin)
        att = jnp.concatenate([att_c, att_l], axis=0)
        ret = jnp.concatenate([ret_c, ret_l], axis=0)
        wr = jnp.zeros((D_MODEL, LANES), F32).at[:, :N_EXPERTS].set(w_router[l])
        wrh = wr.astype(BF16)
        wrl = (wr - wrh.astype(F32)).astype(BF16)
        br = jnp.full((1, LANES), NEG_BIG, F32).at[0, :N_EXPERTS].set(b_router[l])
        x1, h2, ti, tw = _output_projection(att, ret, x, w_out[l].astype(BF16), mod_l,
                                            norm_ffn[l].reshape(1, D_MODEL), wrh, wrl, br)
        block_expert, src_tok, n_active, pos_flat = _routing_tables(ti[:, :TOP_K])
        ys = _moe_experts(block_expert, src_tok, n_active, h2, w_gate_up, b_gate_up, w_down, b_down, l)
        x = _moe_combine(pos_flat, ys, x1, tw, mod_l)

    y_prompt = x[:T_CTX].reshape(BATCH, SEQ, D_MODEL)
    y_sample = x[T_CTX:].reshape(DEC_BATCH, DEC_SEQ, D_MODEL)
    return (y_prompt, y_sample, jnp.stack(new_k, axis=1), jnp.stack(new_v, axis=1), jnp.stack(new_s, axis=1))
```

```python
import functools

import jax
import jax.numpy as jnp
from jax import lax
from jax.experimental import pallas as pl
from jax.experimental.pallas import tpu as pltpu

F32 = jnp.float32
BF16 = jnp.bfloat16

D_MODEL = 1024
DEPTH = 4
BATCH, SEQ = 16, 256
DEC_BATCH, DEC_SEQ = 2, 1024
PAST_LEN = 512
GRID_W = 64
HEAD_DIM = 64
N_Q_HEADS = 8
N_KV_HEADS = 2
ATTN_WIDTH = N_Q_HEADS * HEAD_DIM
KV_WIDTH = N_KV_HEADS * HEAD_DIM
WINDOW = 128
BLOCK = 128
ROPE_BASE = 10000.0
N_RET_HEADS = 4
RET_DK = 128
RET_WIDTH = N_RET_HEADS * RET_DK
CHUNK = 128
IN_WIDTH = ATTN_WIDTH + 2 * KV_WIDTH + 4 * RET_WIDTH
N_EXPERTS = 32
TOP_K = 4
D_FF = D_MODEL
SWIGLU_LIMIT = 7.0
SWIGLU_ALPHA = 1.702
EPS = 1e-6

T_CTX = BATCH * SEQ
T_LAT = DEC_BATCH * DEC_SEQ
T_ALL = T_CTX + T_LAT
N_COND = 1 + DEC_BATCH
COND_PAD = 8
LANES = 128
ROW_TILE = 512
MOE_ROWS = 128
N_ASSIGN = T_ALL * TOP_K
N_MOE_BLOCKS = N_ASSIGN // MOE_ROWS
N_MOE_ITEMS = N_MOE_BLOCKS + N_EXPERTS
SLABS = D_MODEL // LANES
GROUP = 8
NEG_BIG = -1e30
VMEM_LIMIT = 48 * 1024 * 1024


def _cond_of_tile(i):
    return jnp.where(i < T_CTX // ROW_TILE, 0, 1 + (i - T_CTX // ROW_TILE) // (DEC_SEQ // ROW_TILE))


def _params(sem, vmem=VMEM_LIMIT):
    return pltpu.CompilerParams(dimension_semantics=sem, vmem_limit_bytes=vmem)


def _mod_kernel(c_ref, w_ref, b_ref, o_ref):
    c = c_ref[...]
    s = (c * jax.nn.sigmoid(c)).astype(BF16)
    o_ref[0] = jnp.dot(s, w_ref[0].astype(BF16), preferred_element_type=F32) + b_ref[0]


def _modulation(cond, w_ada, b_ada):
    n_col = 6 * D_MODEL // D_MODEL
    return pl.pallas_call(
        _mod_kernel,
        grid=(DEPTH, n_col),
        in_specs=[
            pl.BlockSpec((COND_PAD, D_MODEL), lambda l, j: (0, 0)),
            pl.BlockSpec((1, D_MODEL, D_MODEL), lambda l, j: (l, 0, j)),
            pl.BlockSpec((1, 1, D_MODEL), lambda l, j: (l, 0, j)),
        ],
        out_specs=pl.BlockSpec((1, COND_PAD, D_MODEL), lambda l, j: (l, 0, j)),
        out_shape=jax.ShapeDtypeStruct((DEPTH, COND_PAD, 6 * D_MODEL), F32),
        compiler_params=_params(("arbitrary", "arbitrary")),
        name="modulation",
    )(cond, w_ada, b_ada.reshape(DEPTH, 1, 6 * D_MODEL))


def _rms_rows(x, g):
    ms = jnp.mean(x * x, axis=-1, keepdims=True)
    return x * lax.rsqrt(ms + EPS) * g


def _group_rmsnorm(a, avg_ref, g):
    sq = a * a
    hi = sq.astype(BF16)
    lo = (sq - hi.astype(F32)).astype(BF16)
    avg = avg_ref[...]
    ms = jnp.dot(hi, avg, preferred_element_type=F32) + jnp.dot(lo, avg, preferred_element_type=F32)
    return a * lax.rsqrt(ms + EPS) * g


def _inproj_kernel(x_ref, g_ref, sh_ref, sc_ref, w_ref, qn_ref, kn_ref, avgq_ref, avgk_ref,
                   q_ref, k_ref, v_ref, rq_ref, rk_ref, rv_ref, sg_ref):
    h = _rms_rows(x_ref[...], g_ref[...]) * (1.0 + sc_ref[0]) + sh_ref[0]
    hb = h.astype(BF16)

    def proj(lo, width):
        return jnp.dot(hb, w_ref[:, lo:lo + width], preferred_element_type=F32)

    o = 0
    q_ref[...] = _group_rmsnorm(proj(o, ATTN_WIDTH), avgq_ref, qn_ref[...])
    o += ATTN_WIDTH
    k_ref[...] = _group_rmsnorm(proj(o, KV_WIDTH), avgk_ref, kn_ref[...])
    o += KV_WIDTH
    v_ref[...] = proj(o, KV_WIDTH)
    o += KV_WIDTH
    rq_ref[...] = proj(o, RET_WIDTH) * (RET_DK ** -0.5)
    o += RET_WIDTH
    rk_ref[...] = proj(o, RET_WIDTH)
    o += RET_WIDTH
    rv_ref[...] = proj(o, RET_WIDTH)
    o += RET_WIDTH
    rg = proj(o, RET_WIDTH)
    sg_ref[...] = rg * jax.nn.sigmoid(rg)


def _mod_spec(col):
    return pl.BlockSpec((1, 1, D_MODEL), lambda i, col=col: (_cond_of_tile(i), 0, col))


def _full(shape):
    return pl.BlockSpec(shape, lambda *_: (0,) * len(shape))


def _input_projection(x, norm_g, mod_l, w_in_bf, qn, kn, avgq, avgk):
    rows = lambda w: pl.BlockSpec((ROW_TILE, w), lambda i: (i, 0))
    widths = (ATTN_WIDTH, KV_WIDTH, KV_WIDTH, RET_WIDTH, RET_WIDTH, RET_WIDTH, RET_WIDTH)
    return pl.pallas_call(
        _inproj_kernel,
        grid=(T_ALL // ROW_TILE,),
        in_specs=[rows(D_MODEL), _full((1, D_MODEL)), _mod_spec(0), _mod_spec(1),
                  _full((D_MODEL, IN_WIDTH)), _full((1, ATTN_WIDTH)), _full((1, KV_WIDTH)),
                  _full((ATTN_WIDTH, ATTN_WIDTH)), _full((KV_WIDTH, KV_WIDTH))],
        out_specs=[rows(w) for w in widths],
        out_shape=[jax.ShapeDtypeStruct((T_ALL, w), F32) for w in widths],
        compiler_params=_params(("arbitrary",)),
        name="norm_inproj",
    )(x, norm_g, mod_l, mod_l, w_in_bf, qn, kn, avgq, avgk)


def _attend(q, kall, vall, valid, sink_ref, o_ref):
    scale = HEAD_DIM ** -0.5
    lane = lax.broadcasted_iota(jnp.int32, (1, LANES), 1)
    low = lane < HEAD_DIM
    k_at, v_at = [], []
    for g in range(N_KV_HEADS):
        keep = low if g == 0 else jnp.logical_not(low)
        kg = jnp.where(keep, kall, 0.0)
        vg = jnp.where(keep, vall, 0.0)
        kr = pltpu.roll(kg, HEAD_DIM, 1)
        vr = pltpu.roll(vg, HEAD_DIM, 1)
        pair_k = (kg, kr) if g == 0 else (kr, kg)
        pair_v = (vg, vr) if g == 0 else (vr, vg)
        k_at.append([t.astype(BF16) for t in pair_k])
        v_at.append([t.astype(BF16) for t in pair_v])
    for j in range(ATTN_WIDTH // LANES):
        qj = q[:, j * LANES:(j + 1) * LANES].astype(BF16)
        acc = None
        for off in range(2):
            h = 2 * j + off
            g = h // (N_Q_HEADS // N_KV_HEADS)
            s = lax.dot_general(qj, k_at[g][off], (((1,), (1,)), ((), ())), preferred_element_type=F32) * scale
            if valid is not None:
                s = jnp.where(valid, s, NEG_BIG)
            sink = sink_ref[h]
            m = jnp.maximum(jnp.max(s, axis=-1, keepdims=True), sink)
            e = jnp.exp(s - m)
            den = jnp.sum(e, axis=-1, keepdims=True) + jnp.exp(sink - m)
            o = jnp.dot(e.astype(BF16), v_at[g][off], preferred_element_type=F32) / den
            acc = o if acc is None else acc + o
        o_ref[:, j * LANES:(j + 1) * LANES] = acc


def _ctx_attn_kernel(sink_ref, q_ref, k_ref, v_ref, o_ref):
    _attend(q_ref[...], k_ref[...], v_ref[...], None, sink_ref, o_ref)


def _context_attention(sink, q, k, v):
    return pl.pallas_call(
        _ctx_attn_kernel,
        grid_spec=pltpu.PrefetchScalarGridSpec(
            num_scalar_prefetch=1,
            grid=(BATCH,),
            in_specs=[pl.BlockSpec((SEQ, ATTN_WIDTH), lambda b, s: (b, 0)),
                      pl.BlockSpec((SEQ, KV_WIDTH), lambda b, s: (b, 0)),
                      pl.BlockSpec((SEQ, KV_WIDTH), lambda b, s: (b, 0))],
            out_specs=pl.BlockSpec((SEQ, ATTN_WIDTH), lambda b, s: (b, 0)),
        ),
        out_shape=jax.ShapeDtypeStruct((T_CTX, ATTN_WIDTH), F32),
        compiler_params=_params(("arbitrary",)),
        name="context_attention",
    )(sink, q, k, v)


def _rope_block(x, cos, sin_signed):
    lane = lax.broadcasted_iota(jnp.int32, (1, LANES), 1)
    first = (lane % (HEAD_DIM // 2)) < (HEAD_DIM // 4)
    swapped = jnp.where(first, pltpu.roll(x, LANES - HEAD_DIM // 4, 1), pltpu.roll(x, HEAD_DIM // 4, 1))
    return x * cos + swapped * sin_signed


LOCAL_KEYS = 3 * BLOCK


def _lat_attn_kernel(sink_ref, q_ref, k_ref, v_ref, ck_ref, cv_ref, cosq_ref, sinq_ref, cosk_ref, sin_k_ref,
                     o_ref):
    n = pl.program_id(1)
    start = pl.multiple_of(jnp.clip((n - 1) * BLOCK, 0, DEC_SEQ - LOCAL_KEYS), BLOCK)
    q = q_ref[...]
    q = jnp.concatenate(
        [_rope_block(q[:, j * LANES:(j + 1) * LANES], cosq_ref[:, j * LANES:(j + 1) * LANES],
                     sinq_ref[:, j * LANES:(j + 1) * LANES]) for j in range(ATTN_WIDTH // LANES)], axis=1)
    kw = _rope_block(k_ref[pl.ds(start, LOCAL_KEYS), :], cosk_ref[pl.ds(start, LOCAL_KEYS), :],
                     sin_k_ref[pl.ds(start, LOCAL_KEYS), :])
    vw = v_ref[pl.ds(start, LOCAL_KEYS), :]
    kall = jnp.concatenate([kw, ck_ref[0, 0]], axis=0)
    vall = jnp.concatenate([vw, cv_ref[0, 0]], axis=0)
    qpos = n * BLOCK + lax.broadcasted_iota(jnp.int32, (BLOCK, LOCAL_KEYS + PAST_LEN), 0)
    col = lax.broadcasted_iota(jnp.int32, (BLOCK, LOCAL_KEYS + PAST_LEN), 1)
    valid = jnp.logical_or(col >= LOCAL_KEYS, jnp.abs(qpos - (start + col)) <= WINDOW)
    _attend(q, kall, vall, valid, sink_ref, o_ref)


def _latent_attention(sink, q, k, v, cache_k, cache_v, layer, cosq, sinq, cosk, sin_k):
    nb = DEC_SEQ // BLOCK
    ctx_block0 = T_CTX // BLOCK
    ctx_seq0 = T_CTX // DEC_SEQ
    cache_spec = pl.BlockSpec((1, 1, PAST_LEN, KV_WIDTH), lambda b, n, s: (b, layer, 0, 0))
    return pl.pallas_call(
        _lat_attn_kernel,
        grid_spec=pltpu.PrefetchScalarGridSpec(
            num_scalar_prefetch=1,
            grid=(DEC_BATCH, nb),
            in_specs=[pl.BlockSpec((BLOCK, ATTN_WIDTH), lambda b, n, s: (ctx_block0 + b * nb + n, 0)),
                      pl.BlockSpec((DEC_SEQ, KV_WIDTH), lambda b, n, s: (ctx_seq0 + b, 0)),
                      pl.BlockSpec((DEC_SEQ, KV_WIDTH), lambda b, n, s: (ctx_seq0 + b, 0)),
                      cache_spec, cache_spec,
                      pl.BlockSpec((BLOCK, ATTN_WIDTH), lambda b, n, s: (n, 0)),
                      pl.BlockSpec((BLOCK, ATTN_WIDTH), lambda b, n, s: (n, 0)),
                      pl.BlockSpec((DEC_SEQ, KV_WIDTH), lambda b, n, s: (0, 0)),
                      pl.BlockSpec((DEC_SEQ, KV_WIDTH), lambda b, n, s: (0, 0))],
            out_specs=pl.BlockSpec((BLOCK, ATTN_WIDTH), lambda b, n, s: (b * nb + n, 0)),
        ),
        out_shape=jax.ShapeDtypeStruct((T_LAT, ATTN_WIDTH), F32),
        compiler_params=_params(("arbitrary", "arbitrary")),
        name="latent_attention",
    )(sink, q, k, v, cache_k, cache_v, cosq, sinq, cosk, sin_k)


def _rope_tables():
    t = jnp.arange(DEC_SEQ)
    nf = HEAD_DIM // 4
    inv = ROPE_BASE ** (-jnp.arange(nf, dtype=F32) / nf)

    def half(coord):
        ang = coord.astype(F32)[:, None] * inv[None, :]
        c, s = jnp.cos(ang), jnp.sin(ang)
        return jnp.concatenate([c, c], axis=1), jnp.concatenate([-s, s], axis=1)

    cr, sr = half(t // GRID_W)
    cc, sc = half(t % GRID_W)
    cos = jnp.concatenate([cr, cc], axis=1)
    sin = jnp.concatenate([sr, sc], axis=1)
    return (jnp.tile(cos, (1, N_Q_HEADS)), jnp.tile(sin, (1, N_Q_HEADS)),
            jnp.tile(cos, (1, N_KV_HEADS)), jnp.tile(sin, (1, N_KV_HEADS)))


def _ret_kernel(lg_ref, cd_ref, q_ref, k_ref, v_ref, sg_ref, gn_ref, *rest, n_chunks, has_s0, write_state):
    rest = list(rest)
    s0_ref = rest.pop(0) if has_s0 else None
    o_ref = rest.pop(0)
    sf_ref = rest.pop(0) if write_state else None
    acc_ref = rest.pop(0)
    h = pl.program_id(1)
    row = lax.broadcasted_iota(jnp.int32, (CHUNK, CHUNK), 0).astype(F32)
    col = lax.broadcasted_iota(jnp.int32, (CHUNK, CHUNK), 1).astype(F32)
    rel = row - col
    pos = lax.broadcasted_iota(jnp.int32, (CHUNK, 1), 0).astype(F32)
    gn = gn_ref[...]

    def run(direction):
        lg = lg_ref[direction * N_RET_HEADS + h]
        cd = cd_ref[direction * N_RET_HEADS + h]
        if direction == 0:
            intra = jnp.where(rel >= 0, jnp.exp(lg * rel), 0.0)
            q_dec = jnp.exp(lg * (pos + 1.0))
            k_dec = jnp.exp(lg * (CHUNK - 1.0 - pos))
            order = range(n_chunks)
        else:
            intra = jnp.where(rel <= 0, jnp.exp(-lg * rel), 0.0)
            q_dec = jnp.exp(lg * (CHUNK - pos))
            k_dec = jnp.exp(lg * pos)
            order = range(n_chunks - 1, -1, -1)
        state = s0_ref[0, 0, direction, 0] if has_s0 else jnp.zeros((RET_DK, RET_DK), F32)
        for c in order:
            rows = slice(c * CHUNK, (c + 1) * CHUNK)
            qc, kc, vc = q_ref[rows, :], k_ref[rows, :], v_ref[rows, :]
            qb, kb, vb = qc.astype(BF16), kc.astype(BF16), vc.astype(BF16)
            scores = lax.dot_general(qb, kb, (((1,), (1,)), ((), ())), preferred_element_type=F32) * intra
            o = (jnp.dot(scores.astype(BF16), vb, preferred_element_type=F32)
                 + jnp.dot(qb, state.astype(BF16), preferred_element_type=F32) * q_dec)
            kd_t = (kc * k_dec).T.astype(BF16)
            state = state * cd + jnp.dot(kd_t, vb, preferred_element_type=F32)
            if direction == 0:
                acc_ref[rows, :] = o
            else:
                tot = acc_ref[rows, :] + o
                o_ref[rows, :] = _rms_rows(tot, gn) * sg_ref[rows, :]
        if write_state:
            sf_ref[0, direction, 0] = state

    run(0)
    run(1)


def _retention(lg, cd, rq, rk, rv, sg, gn, s0, layer, *, n_seq, seq_len, row0, write_state):
    blk0 = row0 // seq_len
    rows = pl.BlockSpec((seq_len, RET_DK), lambda b, h, *_: (blk0 + b, h))
    in_specs = [rows, rows, rows, rows, pl.BlockSpec((1, RET_DK), lambda b, h, *_: (0, h))]
    args = [rq, rk, rv, sg, gn]
    if s0 is not None:
        in_specs.append(pl.BlockSpec((1, 1, 2, 1, RET_DK, RET_DK), lambda b, h, *_: (b, layer, 0, h, 0, 0)))
        args.append(s0)
    out_specs = [pl.BlockSpec((seq_len, RET_DK), lambda b, h, *_: (b, h))]
    out_shape = [jax.ShapeDtypeStruct((n_seq * seq_len, RET_WIDTH), F32)]
    if write_state:
        out_specs.append(pl.BlockSpec((1, 2, 1, RET_DK, RET_DK), lambda b, h, *_: (b, 0, h, 0, 0)))
        out_shape.append(jax.ShapeDtypeStruct((n_seq, 2, N_RET_HEADS, RET_DK, RET_DK), F32))
    kern = functools.partial(_ret_kernel, n_chunks=seq_len // CHUNK, has_s0=s0 is not None,
                             write_state=write_state)
    return pl.pallas_call(
        kern,
        grid_spec=pltpu.PrefetchScalarGridSpec(
            num_scalar_prefetch=2,
            grid=(n_seq, N_RET_HEADS),
            in_specs=in_specs,
            out_specs=out_specs,
            scratch_shapes=[pltpu.VMEM((seq_len, RET_DK), F32)],
        ),
        out_shape=out_shape,
        compiler_params=_params(("arbitrary", "arbitrary")),
        name="retention_ctx" if write_state else "retention_lat",
    )(lg, cd, *args)


def _outproj_kernel(att_ref, ret_ref, x_ref, wo_ref, g1_ref, sh2_ref, sc2_ref, nf_ref, wrh_ref, wrl_ref, br_ref,
                    x1_ref, h2t_ref, ti_ref, tw_ref):
    y = (jnp.dot(att_ref[...].astype(BF16), wo_ref[0:ATTN_WIDTH, :], preferred_element_type=F32)
         + jnp.dot(ret_ref[...].astype(BF16), wo_ref[ATTN_WIDTH:, :], preferred_element_type=F32))
    x1 = x_ref[...] + g1_ref[0] * y
    x1_ref[...] = x1
    h2 = _rms_rows(x1, nf_ref[...]) * (1.0 + sc2_ref[0]) + sh2_ref[0]
    for s in range(SLABS):
        h2t_ref[pl.ds(s, ROW_TILE, stride=SLABS), :] = h2[:, s * LANES:(s + 1) * LANES]
    hh = h2.astype(BF16)
    hl = (h2 - hh.astype(F32)).astype(BF16)
    wrh = wrh_ref[...]
    logits = (jnp.dot(hh, wrh, preferred_element_type=F32) + jnp.dot(hl, wrh, preferred_element_type=F32)
              + jnp.dot(hh, wrl_ref[...], preferred_element_type=F32) + br_ref[...])
    lane = lax.broadcasted_iota(jnp.int32, logits.shape, 1)
    vals, idxs = [], []
    cur = logits
    for _ in range(TOP_K):
        m = jnp.max(cur, axis=-1, keepdims=True)
        idx = jnp.min(jnp.where(cur == m, lane, LANES), axis=-1, keepdims=True)
        vals.append(m)
        idxs.append(idx)
        cur = jnp.where(lane == idx, -jnp.inf, cur)
    es = [jnp.exp(v - vals[0]) for v in vals]
    den = es[0] + es[1] + es[2] + es[3]
    ti = jnp.zeros(logits.shape, jnp.int32)
    tw = jnp.zeros(logits.shape, F32)
    for k in range(TOP_K):
        ti = jnp.where(lane == k, idxs[k], ti)
        tw = jnp.where(lane == k, es[k] / den, tw)
    ti_ref[...] = ti
    tw_ref[...] = tw


def _output_projection(att, ret, x, w_out_bf, mod_l, nf, wrh, wrl, br):
    rows = lambda w: pl.BlockSpec((ROW_TILE, w), lambda i: (i, 0))
    return pl.pallas_call(
        _outproj_kernel,
        grid=(T_ALL // ROW_TILE,),
        in_specs=[rows(ATTN_WIDTH), rows(RET_WIDTH), rows(D_MODEL), _full((D_MODEL, D_MODEL)),
                  _mod_spec(2), _mod_spec(3), _mod_spec(4), _full((1, D_MODEL)),
                  _full((D_MODEL, LANES)), _full((D_MODEL, LANES)), _full((1, LANES))],
        out_specs=[rows(D_MODEL), pl.BlockSpec((ROW_TILE * SLABS, LANES), lambda i: (i, 0)), rows(LANES), rows(LANES)],
        out_shape=[jax.ShapeDtypeStruct((T_ALL, D_MODEL), F32), jax.ShapeDtypeStruct((T_ALL * SLABS, LANES), F32),
                   jax.ShapeDtypeStruct((T_ALL, LANES), jnp.int32), jax.ShapeDtypeStruct((T_ALL, LANES), F32)],
        compiler_params=_params(("arbitrary",)),
        name="outproj_router",
    )(att, ret, x, w_out_bf, mod_l, mod_l, mod_l, nf, wrh, wrl, br)


def _token_tile(ref, t):
    return ref.at[pl.ds(pl.multiple_of(t * SLABS, SLABS), SLABS), :]


def _dispatch_kernel(src_ref, h2t_hbm, o_ref, hres, xg, sem):
    @pl.when(pl.program_id(0) == 0)
    def _():
        cp = pltpu.make_async_copy(h2t_hbm, hres, sem.at[0])
        cp.start()
        cp.wait()

    def body(r, carry):
        _token_tile(xg, r)[...] = _token_tile(hres, src_ref[0, 0, r])[...]
        return carry

    lax.fori_loop(0, MOE_ROWS, body, 0, unroll=8)
    for s in range(SLABS):
        o_ref[:, s * LANES:(s + 1) * LANES] = xg[pl.ds(s, MOE_ROWS, stride=SLABS), :].astype(BF16)


def _smem_rows(width):
    return pl.BlockSpec((1, 1, width), lambda i, *_: (jnp.minimum(i, N_MOE_BLOCKS - 1), 0, 0),
                        memory_space=pltpu.SMEM)


def _moe_dispatch(src3, h2t):
    return pl.pallas_call(
        _dispatch_kernel,
        grid=(N_MOE_BLOCKS,),
        in_specs=[_smem_rows(MOE_ROWS), pl.BlockSpec(memory_space=pl.ANY)],
        out_specs=pl.BlockSpec((MOE_ROWS, D_MODEL), lambda b: (b, 0)),
        out_shape=jax.ShapeDtypeStruct((N_ASSIGN, D_MODEL), BF16),
        scratch_shapes=[pltpu.VMEM((T_ALL * SLABS, LANES), F32), pltpu.VMEM((MOE_ROWS * SLABS, LANES), F32),
                        pltpu.SemaphoreType.DMA((1,))],
        compiler_params=_params(("arbitrary",)),
        name="moe_dispatch",
    )(src3, h2t)


def _experts_kernel(ib_ref, ie_ref, lo_ref, hi_ref, x_ref, wgu_ref, bgu_ref, wdn_ref, bdn_ref, o_ref,
                    wgu_bf, wdn_bf):
    i = pl.program_id(0)
    lo, hi = lo_ref[i], hi_ref[i]

    @pl.when(jnp.logical_or(i == 0, ie_ref[i] != ie_ref[jnp.maximum(i - 1, 0)]))
    def _():
        wgu_bf[...] = wgu_ref[...].astype(BF16)
        wdn_bf[...] = wdn_ref[...].astype(BF16)

    @pl.when(hi > lo)
    def _():
        gu = jnp.dot(x_ref[...], wgu_bf[...], preferred_element_type=F32) + bgu_ref[...]
        x_glu = jnp.minimum(gu[:, :D_FF], SWIGLU_LIMIT)
        x_lin = jnp.clip(gu[:, D_FF:], -SWIGLU_LIMIT, SWIGLU_LIMIT)
        act = x_glu * jax.nn.sigmoid(SWIGLU_ALPHA * x_glu) * (x_lin + 1.0)
        out = jnp.dot(act.astype(BF16), wdn_bf[...], preferred_element_type=F32) + bdn_ref[...]

        @pl.when(lo == 0)
        def _():
            o_ref[...] = out

        @pl.when(lo > 0)
        def _():
            row = lax.broadcasted_iota(jnp.int32, (MOE_ROWS, 1), 0)
            o_ref[...] = jnp.where(jnp.logical_and(row >= lo, row < hi), out, o_ref[...])


def _moe_experts(item_block, item_expert, item_lo, item_hi, xs, w_gu, b_gu, w_dn, b_dn, layer):
    wspec = lambda rows, cols: pl.BlockSpec((None, None, rows, cols),
                                            lambda i, ib, ie, lo, hi: (layer, ie[i], 0, 0))
    rows = pl.BlockSpec((MOE_ROWS, D_MODEL), lambda i, ib, ie, lo, hi: (ib[i], 0))
    return pl.pallas_call(
        _experts_kernel,
        grid_spec=pltpu.PrefetchScalarGridSpec(
            num_scalar_prefetch=4,
            grid=(N_MOE_ITEMS,),
            in_specs=[rows, wspec(D_MODEL, 2 * D_FF), wspec(1, 2 * D_FF), wspec(D_FF, D_MODEL), wspec(1, D_MODEL)],
            out_specs=rows,
            scratch_shapes=[pltpu.VMEM((D_MODEL, 2 * D_FF), BF16), pltpu.VMEM((D_FF, D_MODEL), BF16)],
        ),
        out_shape=jax.ShapeDtypeStruct((N_ASSIGN, D_MODEL), F32),
        compiler_params=_params(("arbitrary",)),
        name="moe_experts",
    )(item_block, item_expert, item_lo, item_hi, xs, w_gu, b_gu.reshape(DEPTH, N_EXPERTS, 1, 2 * D_FF),
      w_dn, b_dn.reshape(DEPTH, N_EXPERTS, 1, D_MODEL))


N_ROW_TILES = T_ALL // ROW_TILE
ZERO_ROWS = ROW_TILE * SLABS


def _combine_kernel(src_ref, w_ref, same_ref, ys_ref, x1_ref, g2_ref, o_ref, yres, ot):
    i = pl.program_id(0)

    @pl.when(i == 0)
    def _():
        def zero(j, carry):
            yres[pl.ds(pl.multiple_of(j * ZERO_ROWS, ZERO_ROWS), ZERO_ROWS), :] = jnp.zeros((ZERO_ROWS, LANES), F32)
            return carry
        lax.fori_loop(0, N_ROW_TILES, zero, 0)

    @pl.when(i < N_MOE_BLOCKS)
    def _():
        for s in range(SLABS):
            ot[pl.ds(s, MOE_ROWS, stride=SLABS), :] = ys_ref[:, s * LANES:(s + 1) * LANES]

        def group(g, carry):
            rows = [g * GROUP + j for j in range(GROUP)]
            toks = [src_ref[0, 0, r] for r in rows]
            ws = [w_ref[0, 0, r] for r in rows]

            @pl.when(same_ref[0, 0, g] != 0)
            def _():
                new = [_token_tile(yres, t)[...] + w * _token_tile(ot, r)[...] for t, w, r in zip(toks, ws, rows)]
                for t, v in zip(toks, new):
                    _token_tile(yres, t)[...] = v

            @pl.when(same_ref[0, 0, g] == 0)
            def _():
                for t, w, r in zip(toks, ws, rows):
                    _token_tile(yres, t)[...] = _token_tile(yres, t)[...] + w * _token_tile(ot, r)[...]

            return carry

        lax.fori_loop(0, MOE_ROWS // GROUP, group, 0)

    @pl.when(i >= N_MOE_BLOCKS)
    def _():
        base = (i - N_MOE_BLOCKS) * ZERO_ROWS
        for s in range(SLABS):
            cols = slice(s * LANES, (s + 1) * LANES)
            y = yres[pl.ds(base + s, ROW_TILE, stride=SLABS), :]
            o_ref[:, cols] = x1_ref[:, cols] + g2_ref[0][:, cols] * y


def _moe_combine(src3, w3, same3, ys, x1, mod_l):
    tile = lambda i: jnp.maximum(i - N_MOE_BLOCKS, 0)
    return pl.pallas_call(
        _combine_kernel,
        grid=(N_MOE_BLOCKS + N_ROW_TILES,),
        in_specs=[_smem_rows(MOE_ROWS), _smem_rows(MOE_ROWS), _smem_rows(MOE_ROWS // GROUP),
                  pl.BlockSpec((MOE_ROWS, D_MODEL), lambda i: (jnp.minimum(i, N_MOE_BLOCKS - 1), 0)),
                  pl.BlockSpec((ROW_TILE, D_MODEL), lambda i: (tile(i), 0)),
                  pl.BlockSpec((1, 1, D_MODEL), lambda i: (_cond_of_tile(tile(i)), 0, 5))],
        out_specs=pl.BlockSpec((ROW_TILE, D_MODEL), lambda i: (tile(i), 0)),
        out_shape=jax.ShapeDtypeStruct((T_ALL, D_MODEL), F32),
        scratch_shapes=[pltpu.VMEM((T_ALL * SLABS, LANES), F32), pltpu.VMEM((MOE_ROWS * SLABS, LANES), F32)],
        compiler_params=_params(("arbitrary",)),
        name="moe_combine",
    )(src3, w3, same3, ys, x1, mod_l)


def _routing_tables(top_idx, top_w):
    flat_e = top_idx.reshape(N_ASSIGN)
    e_sorted, order, w_sorted = lax.sort(
        (flat_e, jnp.arange(N_ASSIGN, dtype=jnp.int32), top_w.reshape(N_ASSIGN)), num_keys=1, is_stable=True)
    src3 = (order // TOP_K).reshape(N_MOE_BLOCKS, 1, MOE_ROWS)
    w3 = w_sorted.reshape(N_MOE_BLOCKS, 1, MOE_ROWS)
    eg = e_sorted.reshape(N_ASSIGN // GROUP, GROUP)
    same3 = (eg[:, 0] == eg[:, -1]).astype(jnp.int32).reshape(N_MOE_BLOCKS, 1, MOE_ROWS // GROUP)
    experts = jnp.arange(N_EXPERTS, dtype=jnp.int32)
    counts = jnp.sum((flat_e[:, None] == experts[None, :]).astype(jnp.int32), axis=0)
    end = jnp.cumsum(counts)
    start = end - counts
    blk0 = jnp.arange(N_MOE_BLOCKS, dtype=jnp.int32)[:, None] * MOE_ROWS
    lo = jnp.maximum(start[None, :], blk0) - blk0
    hi = jnp.minimum(end[None, :], blk0 + MOE_ROWS) - blk0
    live = (hi > lo).reshape(-1)
    slot = jnp.cumsum(live.astype(jnp.int32)) - 1
    pick = jnp.logical_and(live[None, :], slot[None, :] == jnp.arange(N_MOE_ITEMS, dtype=jnp.int32)[:, None])
    pair = jnp.arange(N_MOE_BLOCKS * N_EXPERTS, dtype=jnp.int32)
    take = lambda v: jnp.sum(jnp.where(pick, v.reshape(-1)[None, :], 0), axis=1).astype(jnp.int32)
    used = jnp.any(pick, axis=1)
    item_block = jnp.where(used, take(pair // N_EXPERTS), N_MOE_BLOCKS - 1)
    item_expert = jnp.where(used, take(pair % N_EXPERTS), e_sorted[-1])
    return src3, w3, same3, item_block, item_expert, take(lo), take(hi)


def kernel(x_prompt, x_sample, cache_attn_k, cache_attn_v, state_ret, c, c_ctx, norm_mix, norm_ffn, w_ada, b_ada,
           w_in, q_norm, k_norm, attn_sink, ret_decay, ret_norm, w_out, w_router, b_router, w_gate_up, b_gate_up,
           w_down, b_down):
    x = jnp.concatenate([x_prompt.reshape(T_CTX, D_MODEL), x_sample.reshape(T_LAT, D_MODEL)], axis=0)
    cond = jnp.zeros((COND_PAD, D_MODEL), F32).at[0].set(c_ctx).at[1:N_COND].set(c)
    mod = _modulation(cond, w_ada, b_ada)[:, :N_COND].reshape(DEPTH, N_COND, 1, 6 * D_MODEL)

    cache_k = cache_attn_k.reshape(DEC_BATCH, DEPTH, PAST_LEN, KV_WIDTH)
    cache_v = cache_attn_v.reshape(DEC_BATCH, DEPTH, PAST_LEN, KV_WIDTH)
    cosq, sinq, cosk, sin_k = _rope_tables()
    grp = jnp.arange(ATTN_WIDTH) // HEAD_DIM
    avgq = jnp.where(grp[:, None] == grp[None, :], 1.0 / HEAD_DIM, 0.0).astype(BF16)
    avgk = avgq[:KV_WIDTH, :KV_WIDTH]
    log_gamma = jax.nn.log_sigmoid(ret_decay.astype(F32))
    chunk_decay = jnp.exp(log_gamma * CHUNK)

    new_k, new_v, new_s = [], [], []
    for l in range(DEPTH):
        mod_l = mod[l]
        q, k, v, rq, rk, rv, sg = _input_projection(
            x, norm_mix[l].reshape(1, D_MODEL), mod_l, w_in[l].astype(BF16),
            jnp.tile(q_norm[l], N_Q_HEADS).reshape(1, ATTN_WIDTH), jnp.tile(k_norm[l], N_KV_HEADS).reshape(1, KV_WIDTH),
            avgq, avgk)
        new_k.append(k[:T_CTX].reshape(BATCH, SEQ, N_KV_HEADS, HEAD_DIM))
        new_v.append(v[:T_CTX].reshape(BATCH, SEQ, N_KV_HEADS, HEAD_DIM))
        sink = attn_sink[l].astype(F32)
        att_c = _context_attention(sink, q, k, v)
        att_l = _latent_attention(sink, q, k, v, cache_k, cache_v, l, cosq, sinq, cosk, sin_k)
        lg = log_gamma[l].reshape(2 * N_RET_HEADS)
        cd = chunk_decay[l].reshape(2 * N_RET_HEADS)
        gn = ret_norm[l].reshape(1, RET_WIDTH)
        ret_c, s_fin = _retention(lg, cd, rq, rk, rv, sg, gn, None, l, n_seq=BATCH, seq_len=SEQ, row0=0,
                                  write_state=True)
        (ret_l,) = _retention(lg, cd, rq, rk, rv, sg, gn, state_ret, l, n_seq=DEC_BATCH, seq_len=DEC_SEQ,
                              row0=T_CTX, write_state=False)
        new_s.append(s_fin)
        att = jnp.concatenate([att_c, att_l], axis=0)
        ret = jnp.concatenate([ret_c, ret_l], axis=0)
        wr = jnp.zeros((D_MODEL, LANES), F32).at[:, :N_EXPERTS].set(w_router[l])
        wrh = wr.astype(BF16)
        wrl = (wr - wrh.astype(F32)).astype(BF16)
        br = jnp.full((1, LANES), NEG_BIG, F32).at[0, :N_EXPERTS].set(b_router[l])
        x1, h2t, ti, tw = _output_projection(att, ret, x, w_out[l].astype(BF16), mod_l,
                                             norm_ffn[l].reshape(1, D_MODEL), wrh, wrl, br)
        src3, w3, same3, item_block, item_expert, item_lo, item_hi = _routing_tables(ti[:, :TOP_K], tw[:, :TOP_K])
        xs = _moe_dispatch(src3, h2t)
        ys = _moe_experts(item_block, item_expert, item_lo, item_hi, xs, w_gate_up, b_gate_up, w_down, b_down, l)
        x = _moe_combine(src3, w3, same3, ys, x1, mod_l)

    y_prompt = x[:T_CTX].reshape(BATCH, SEQ, D_MODEL)
    y_sample = x[T_CTX:].reshape(DEC_BATCH, DEC_SEQ, D_MODEL)
    return (y_prompt, y_sample, jnp.stack(new_k, axis=1), jnp.stack(new_v, axis=1), jnp.stack(new_s, axis=1))
```

```python
import functools

import jax
import jax.numpy as jnp
from jax import lax
from jax.experimental import pallas as pl
from jax.experimental.pallas import tpu as pltpu

F32 = jnp.float32
BF16 = jnp.bfloat16

D_MODEL = 1024
DEPTH = 4
BATCH, SEQ = 16, 256
DEC_BATCH, DEC_SEQ = 2, 1024
PAST_LEN = 512
GRID_W = 64
HEAD_DIM = 64
N_Q_HEADS = 8
N_KV_HEADS = 2
ATTN_WIDTH = N_Q_HEADS * HEAD_DIM
KV_WIDTH = N_KV_HEADS * HEAD_DIM
WINDOW = 128
BLOCK = 128
ROPE_BASE = 10000.0
N_RET_HEADS = 4
RET_DK = 128
RET_WIDTH = N_RET_HEADS * RET_DK
CHUNK = 128
IN_WIDTH = ATTN_WIDTH + 2 * KV_WIDTH + 4 * RET_WIDTH
N_EXPERTS = 32
TOP_K = 4
D_FF = D_MODEL
SWIGLU_LIMIT = 7.0
SWIGLU_ALPHA = 1.702
EPS = 1e-6

T_CTX = BATCH * SEQ
T_LAT = DEC_BATCH * DEC_SEQ
T_ALL = T_CTX + T_LAT
N_COND = 1 + DEC_BATCH
COND_PAD = 8
LANES = 128
ROW_TILE = 512
MOE_ROWS = 128
N_ASSIGN = T_ALL * TOP_K
N_MOE_BLOCKS = N_ASSIGN // MOE_ROWS + N_EXPERTS
N_PAD = N_MOE_BLOCKS * MOE_ROWS
MOE_STEP_BLOCKS = 4
MOE_STEP_ROWS = MOE_STEP_BLOCKS * MOE_ROWS
N_MOE_STEPS = N_MOE_BLOCKS // MOE_STEP_BLOCKS
SLABS = D_MODEL // LANES
GROUP = 8
N_TILES = T_ALL + 1
NEG_BIG = -1e30
VMEM_LIMIT = 48 * 1024 * 1024


def _cond_of_tile(i):
    return jnp.where(i < T_CTX // ROW_TILE, 0, 1 + (i - T_CTX // ROW_TILE) // (DEC_SEQ // ROW_TILE))


def _params(sem, vmem=VMEM_LIMIT):
    return pltpu.CompilerParams(dimension_semantics=sem, vmem_limit_bytes=vmem)


def _mod_kernel(c_ref, w_ref, b_ref, o_ref):
    c = c_ref[...]
    s = (c * jax.nn.sigmoid(c)).astype(BF16)
    o_ref[0] = jnp.dot(s, w_ref[0].astype(BF16), preferred_element_type=F32) + b_ref[0]


def _modulation(cond, w_ada, b_ada):
    n_col = 6 * D_MODEL // D_MODEL
    return pl.pallas_call(
        _mod_kernel,
        grid=(DEPTH, n_col),
        in_specs=[
            pl.BlockSpec((COND_PAD, D_MODEL), lambda l, j: (0, 0)),
            pl.BlockSpec((1, D_MODEL, D_MODEL), lambda l, j: (l, 0, j)),
            pl.BlockSpec((1, 1, D_MODEL), lambda l, j: (l, 0, j)),
        ],
        out_specs=pl.BlockSpec((1, COND_PAD, D_MODEL), lambda l, j: (l, 0, j)),
        out_shape=jax.ShapeDtypeStruct((DEPTH, COND_PAD, 6 * D_MODEL), F32),
        compiler_params=_params(("arbitrary", "arbitrary")),
        name="modulation",
    )(cond, w_ada, b_ada.reshape(DEPTH, 1, 6 * D_MODEL))


def _rms_rows(x, g):
    ms = jnp.mean(x * x, axis=-1, keepdims=True)
    return x * lax.rsqrt(ms + EPS) * g


def _group_rmsnorm(a, avg_ref, g):
    sq = a * a
    hi = sq.astype(BF16)
    lo = (sq - hi.astype(F32)).astype(BF16)
    avg = avg_ref[...]
    ms = jnp.dot(hi, avg, preferred_element_type=F32) + jnp.dot(lo, avg, preferred_element_type=F32)
    return a * lax.rsqrt(ms + EPS) * g


def _inproj_kernel(x_ref, g_ref, sh_ref, sc_ref, w_ref, qn_ref, kn_ref, avgq_ref, avgk_ref,
                   q_ref, k_ref, v_ref, rq_ref, rk_ref, rv_ref, sg_ref):
    h = _rms_rows(x_ref[...], g_ref[...]) * (1.0 + sc_ref[0]) + sh_ref[0]
    hb = h.astype(BF16)

    def proj(lo, width):
        return jnp.dot(hb, w_ref[:, lo:lo + width], preferred_element_type=F32)

    o = 0
    q_ref[...] = _group_rmsnorm(proj(o, ATTN_WIDTH), avgq_ref, qn_ref[...])
    o += ATTN_WIDTH
    k_ref[...] = _group_rmsnorm(proj(o, KV_WIDTH), avgk_ref, kn_ref[...])
    o += KV_WIDTH
    v_ref[...] = proj(o, KV_WIDTH)
    o += KV_WIDTH
    rq_ref[...] = proj(o, RET_WIDTH) * (RET_DK ** -0.5)
    o += RET_WIDTH
    rk_ref[...] = proj(o, RET_WIDTH)
    o += RET_WIDTH
    rv_ref[...] = proj(o, RET_WIDTH)
    o += RET_WIDTH
    rg = proj(o, RET_WIDTH)
    sg_ref[...] = rg * jax.nn.sigmoid(rg)


def _mod_spec(col):
    return pl.BlockSpec((1, 1, D_MODEL), lambda i, col=col: (_cond_of_tile(i), 0, col))


def _full(shape):
    return pl.BlockSpec(shape, lambda *_: (0,) * len(shape))


def _input_projection(x, norm_g, mod_l, w_in_bf, qn, kn, avgq, avgk):
    rows = lambda w: pl.BlockSpec((ROW_TILE, w), lambda i: (i, 0))
    widths = (ATTN_WIDTH, KV_WIDTH, KV_WIDTH, RET_WIDTH, RET_WIDTH, RET_WIDTH, RET_WIDTH)
    return pl.pallas_call(
        _inproj_kernel,
        grid=(T_ALL // ROW_TILE,),
        in_specs=[rows(D_MODEL), _full((1, D_MODEL)), _mod_spec(0), _mod_spec(1),
                  _full((D_MODEL, IN_WIDTH)), _full((1, ATTN_WIDTH)), _full((1, KV_WIDTH)),
                  _full((ATTN_WIDTH, ATTN_WIDTH)), _full((KV_WIDTH, KV_WIDTH))],
        out_specs=[rows(w) for w in widths],
        out_shape=[jax.ShapeDtypeStruct((T_ALL, w), F32) for w in widths],
        compiler_params=_params(("arbitrary",)),
        name="norm_inproj",
    )(x, norm_g, mod_l, mod_l, w_in_bf, qn, kn, avgq, avgk)


def _attend(q, kall, vall, valid, sink_ref, o_ref):
    scale = HEAD_DIM ** -0.5
    lane = lax.broadcasted_iota(jnp.int32, (1, LANES), 1)
    low = lane < HEAD_DIM
    k_at, v_at = [], []
    for g in range(N_KV_HEADS):
        keep = low if g == 0 else jnp.logical_not(low)
        kg = jnp.where(keep, kall, 0.0)
        vg = jnp.where(keep, vall, 0.0)
        kr = pltpu.roll(kg, HEAD_DIM, 1)
        vr = pltpu.roll(vg, HEAD_DIM, 1)
        pair_k = (kg, kr) if g == 0 else (kr, kg)
        pair_v = (vg, vr) if g == 0 else (vr, vg)
        k_at.append([t.astype(BF16) for t in pair_k])
        v_at.append([t.astype(BF16) for t in pair_v])
    for j in range(ATTN_WIDTH // LANES):
        qj = q[:, j * LANES:(j + 1) * LANES].astype(BF16)
        acc = None
        for off in range(2):
            h = 2 * j + off
            g = h // (N_Q_HEADS // N_KV_HEADS)
            s = lax.dot_general(qj, k_at[g][off], (((1,), (1,)), ((), ())), preferred_element_type=F32) * scale
            if valid is not None:
                s = jnp.where(valid, s, NEG_BIG)
            sink = sink_ref[h]
            m = jnp.maximum(jnp.max(s, axis=-1, keepdims=True), sink)
            e = jnp.exp(s - m)
            den = jnp.sum(e, axis=-1, keepdims=True) + jnp.exp(sink - m)
            o = jnp.dot(e.astype(BF16), v_at[g][off], preferred_element_type=F32) / den
            acc = o if acc is None else acc + o
        o_ref[:, j * LANES:(j + 1) * LANES] = acc


def _ctx_attn_kernel(sink_ref, q_ref, k_ref, v_ref, o_ref):
    _attend(q_ref[...], k_ref[...], v_ref[...], None, sink_ref, o_ref)


def _context_attention(sink, q, k, v):
    return pl.pallas_call(
        _ctx_attn_kernel,
        grid_spec=pltpu.PrefetchScalarGridSpec(
            num_scalar_prefetch=1,
            grid=(BATCH,),
            in_specs=[pl.BlockSpec((SEQ, ATTN_WIDTH), lambda b, s: (b, 0)),
                      pl.BlockSpec((SEQ, KV_WIDTH), lambda b, s: (b, 0)),
                      pl.BlockSpec((SEQ, KV_WIDTH), lambda b, s: (b, 0))],
            out_specs=pl.BlockSpec((SEQ, ATTN_WIDTH), lambda b, s: (b, 0)),
        ),
        out_shape=jax.ShapeDtypeStruct((T_CTX, ATTN_WIDTH), F32),
        compiler_params=_params(("arbitrary",)),
        name="context_attention",
    )(sink, q, k, v)


def _rope_block(x, cos, sin_signed):
    lane = lax.broadcasted_iota(jnp.int32, (1, LANES), 1)
    first = (lane % (HEAD_DIM // 2)) < (HEAD_DIM // 4)
    swapped = jnp.where(first, pltpu.roll(x, LANES - HEAD_DIM // 4, 1), pltpu.roll(x, HEAD_DIM // 4, 1))
    return x * cos + swapped * sin_signed


LOCAL_KEYS = 3 * BLOCK


def _lat_attn_kernel(sink_ref, q_ref, k_ref, v_ref, ck_ref, cv_ref, cosq_ref, sinq_ref, cosk_ref, sin_k_ref,
                     o_ref):
    n = pl.program_id(1)
    start = pl.multiple_of(jnp.clip((n - 1) * BLOCK, 0, DEC_SEQ - LOCAL_KEYS), BLOCK)
    q = q_ref[...]
    q = jnp.concatenate(
        [_rope_block(q[:, j * LANES:(j + 1) * LANES], cosq_ref[:, j * LANES:(j + 1) * LANES],
                     sinq_ref[:, j * LANES:(j + 1) * LANES]) for j in range(ATTN_WIDTH // LANES)], axis=1)
    kw = _rope_block(k_ref[pl.ds(start, LOCAL_KEYS), :], cosk_ref[pl.ds(start, LOCAL_KEYS), :],
                     sin_k_ref[pl.ds(start, LOCAL_KEYS), :])
    vw = v_ref[pl.ds(start, LOCAL_KEYS), :]
    kall = jnp.concatenate([kw, ck_ref[0, 0]], axis=0)
    vall = jnp.concatenate([vw, cv_ref[0, 0]], axis=0)
    qpos = n * BLOCK + lax.broadcasted_iota(jnp.int32, (BLOCK, LOCAL_KEYS + PAST_LEN), 0)
    col = lax.broadcasted_iota(jnp.int32, (BLOCK, LOCAL_KEYS + PAST_LEN), 1)
    valid = jnp.logical_or(col >= LOCAL_KEYS, jnp.abs(qpos - (start + col)) <= WINDOW)
    _attend(q, kall, vall, valid, sink_ref, o_ref)


def _latent_attention(sink, q, k, v, cache_k, cache_v, layer, cosq, sinq, cosk, sin_k):
    nb = DEC_SEQ // BLOCK
    ctx_block0 = T_CTX // BLOCK
    ctx_seq0 = T_CTX // DEC_SEQ
    cache_spec = pl.BlockSpec((1, 1, PAST_LEN, KV_WIDTH), lambda b, n, s: (b, layer, 0, 0))
    return pl.pallas_call(
        _lat_attn_kernel,
        grid_spec=pltpu.PrefetchScalarGridSpec(
            num_scalar_prefetch=1,
            grid=(DEC_BATCH, nb),
            in_specs=[pl.BlockSpec((BLOCK, ATTN_WIDTH), lambda b, n, s: (ctx_block0 + b * nb + n, 0)),
                      pl.BlockSpec((DEC_SEQ, KV_WIDTH), lambda b, n, s: (ctx_seq0 + b, 0)),
                      pl.BlockSpec((DEC_SEQ, KV_WIDTH), lambda b, n, s: (ctx_seq0 + b, 0)),
                      cache_spec, cache_spec,
                      pl.BlockSpec((BLOCK, ATTN_WIDTH), lambda b, n, s: (n, 0)),
                      pl.BlockSpec((BLOCK, ATTN_WIDTH), lambda b, n, s: (n, 0)),
                      pl.BlockSpec((DEC_SEQ, KV_WIDTH), lambda b, n, s: (0, 0)),
                      pl.BlockSpec((DEC_SEQ, KV_WIDTH), lambda b, n, s: (0, 0))],
            out_specs=pl.BlockSpec((BLOCK, ATTN_WIDTH), lambda b, n, s: (b * nb + n, 0)),
        ),
        out_shape=jax.ShapeDtypeStruct((T_LAT, ATTN_WIDTH), F32),
        compiler_params=_params(("arbitrary", "arbitrary")),
        name="latent_attention",
    )(sink, q, k, v, cache_k, cache_v, cosq, sinq, cosk, sin_k)


def _rope_tables():
    t = jnp.arange(DEC_SEQ)
    nf = HEAD_DIM // 4
    inv = ROPE_BASE ** (-jnp.arange(nf, dtype=F32) / nf)

    def half(coord):
        ang = coord.astype(F32)[:, None] * inv[None, :]
        c, s = jnp.cos(ang), jnp.sin(ang)
        return jnp.concatenate([c, c], axis=1), jnp.concatenate([-s, s], axis=1)

    cr, sr = half(t // GRID_W)
    cc, sc = half(t % GRID_W)
    cos = jnp.concatenate([cr, cc], axis=1)
    sin = jnp.concatenate([sr, sc], axis=1)
    return (jnp.tile(cos, (1, N_Q_HEADS)), jnp.tile(sin, (1, N_Q_HEADS)),
            jnp.tile(cos, (1, N_KV_HEADS)), jnp.tile(sin, (1, N_KV_HEADS)))


def _ret_kernel(lg_ref, cd_ref, q_ref, k_ref, v_ref, sg_ref, gn_ref, *rest, n_chunks, has_s0, write_state):
    rest = list(rest)
    s0_ref = rest.pop(0) if has_s0 else None
    o_ref = rest.pop(0)
    sf_ref = rest.pop(0) if write_state else None
    acc_ref = rest.pop(0)
    h = pl.program_id(1)
    row = lax.broadcasted_iota(jnp.int32, (CHUNK, CHUNK), 0).astype(F32)
    col = lax.broadcasted_iota(jnp.int32, (CHUNK, CHUNK), 1).astype(F32)
    rel = row - col
    pos = lax.broadcasted_iota(jnp.int32, (CHUNK, 1), 0).astype(F32)
    gn = gn_ref[...]

    def run(direction):
        lg = lg_ref[direction * N_RET_HEADS + h]
        cd = cd_ref[direction * N_RET_HEADS + h]
        if direction == 0:
            intra = jnp.where(rel >= 0, jnp.exp(lg * rel), 0.0)
            q_dec = jnp.exp(lg * (pos + 1.0))
            k_dec = jnp.exp(lg * (CHUNK - 1.0 - pos))
            order = range(n_chunks)
        else:
            intra = jnp.where(rel <= 0, jnp.exp(-lg * rel), 0.0)
            q_dec = jnp.exp(lg * (CHUNK - pos))
            k_dec = jnp.exp(lg * pos)
            order = range(n_chunks - 1, -1, -1)
        state = s0_ref[0, 0, direction, 0] if has_s0 else jnp.zeros((RET_DK, RET_DK), F32)
        for c in order:
            rows = slice(c * CHUNK, (c + 1) * CHUNK)
            qc, kc, vc = q_ref[rows, :], k_ref[rows, :], v_ref[rows, :]
            qb, kb, vb = qc.astype(BF16), kc.astype(BF16), vc.astype(BF16)
            scores = lax.dot_general(qb, kb, (((1,), (1,)), ((), ())), preferred_element_type=F32) * intra
            o = (jnp.dot(scores.astype(BF16), vb, preferred_element_type=F32)
                 + jnp.dot(qb, state.astype(BF16), preferred_element_type=F32) * q_dec)
            kd_t = (kc * k_dec).T.astype(BF16)
            state = state * cd + jnp.dot(kd_t, vb, preferred_element_type=F32)
            if direction == 0:
                acc_ref[rows, :] = o
            else:
                tot = acc_ref[rows, :] + o
                o_ref[rows, :] = _rms_rows(tot, gn) * sg_ref[rows, :]
        if write_state:
            sf_ref[0, direction, 0] = state

    run(0)
    run(1)


def _retention(lg, cd, rq, rk, rv, sg, gn, s0, layer, *, n_seq, seq_len, row0, write_state):
    blk0 = row0 // seq_len
    rows = pl.BlockSpec((seq_len, RET_DK), lambda b, h, *_: (blk0 + b, h))
    in_specs = [rows, rows, rows, rows, pl.BlockSpec((1, RET_DK), lambda b, h, *_: (0, h))]
    args = [rq, rk, rv, sg, gn]
    if s0 is not None:
        in_specs.append(pl.BlockSpec((1, 1, 2, 1, RET_DK, RET_DK), lambda b, h, *_: (b, layer, 0, h, 0, 0)))
        args.append(s0)
    out_specs = [pl.BlockSpec((seq_len, RET_DK), lambda b, h, *_: (b, h))]
    out_shape = [jax.ShapeDtypeStruct((n_seq * seq_len, RET_WIDTH), F32)]
    if write_state:
        out_specs.append(pl.BlockSpec((1, 2, 1, RET_DK, RET_DK), lambda b, h, *_: (b, 0, h, 0, 0)))
        out_shape.append(jax.ShapeDtypeStruct((n_seq, 2, N_RET_HEADS, RET_DK, RET_DK), F32))
    kern = functools.partial(_ret_kernel, n_chunks=seq_len // CHUNK, has_s0=s0 is not None,
                             write_state=write_state)
    return pl.pallas_call(
        kern,
        grid_spec=pltpu.PrefetchScalarGridSpec(
            num_scalar_prefetch=2,
            grid=(n_seq, N_RET_HEADS),
            in_specs=in_specs,
            out_specs=out_specs,
            scratch_shapes=[pltpu.VMEM((seq_len, RET_DK), F32)],
        ),
        out_shape=out_shape,
        compiler_params=_params(("arbitrary", "arbitrary")),
        name="retention_ctx" if write_state else "retention_lat",
    )(lg, cd, *args)


def _outproj_kernel(att_ref, ret_ref, x_ref, wo_ref, g1_ref, sh2_ref, sc2_ref, nf_ref, wrh_ref, wrl_ref, br_ref,
                    x1_ref, h2t_ref, ti_ref, tw_ref):
    y = (jnp.dot(att_ref[...].astype(BF16), wo_ref[0:ATTN_WIDTH, :], preferred_element_type=F32)
         + jnp.dot(ret_ref[...].astype(BF16), wo_ref[ATTN_WIDTH:, :], preferred_element_type=F32))
    x1 = x_ref[...] + g1_ref[0] * y
    x1_ref[...] = x1
    h2 = _rms_rows(x1, nf_ref[...]) * (1.0 + sc2_ref[0]) + sh2_ref[0]
    for s in range(SLABS):
        h2t_ref[pl.ds(s, ROW_TILE, stride=SLABS), :] = h2[:, s * LANES:(s + 1) * LANES]
    hh = h2.astype(BF16)
    hl = (h2 - hh.astype(F32)).astype(BF16)
    wrh = wrh_ref[...]
    logits = (jnp.dot(hh, wrh, preferred_element_type=F32) + jnp.dot(hl, wrh, preferred_element_type=F32)
              + jnp.dot(hh, wrl_ref[...], preferred_element_type=F32) + br_ref[...])
    lane = lax.broadcasted_iota(jnp.int32, logits.shape, 1)
    vals, idxs = [], []
    cur = logits
    for _ in range(TOP_K):
        m = jnp.max(cur, axis=-1, keepdims=True)
        idx = jnp.min(jnp.where(cur == m, lane, LANES), axis=-1, keepdims=True)
        vals.append(m)
        idxs.append(idx)
        cur = jnp.where(lane == idx, -jnp.inf, cur)
    es = [jnp.exp(v - vals[0]) for v in vals]
    den = es[0] + es[1] + es[2] + es[3]
    ti = jnp.zeros(logits.shape, jnp.int32)
    tw = jnp.zeros(logits.shape, F32)
    for k in range(TOP_K):
        ti = jnp.where(lane == k, idxs[k], ti)
        tw = jnp.where(lane == k, es[k] / den, tw)
    ti_ref[...] = ti
    tw_ref[...] = tw


def _output_projection(att, ret, x, w_out_bf, mod_l, nf, wrh, wrl, br):
    rows = lambda w: pl.BlockSpec((ROW_TILE, w), lambda i: (i, 0))
    return pl.pallas_call(
        _outproj_kernel,
        grid=(T_ALL // ROW_TILE,),
        in_specs=[rows(ATTN_WIDTH), rows(RET_WIDTH), rows(D_MODEL), _full((D_MODEL, D_MODEL)),
                  _mod_spec(2), _mod_spec(3), _mod_spec(4), _full((1, D_MODEL)),
                  _full((D_MODEL, LANES)), _full((D_MODEL, LANES)), _full((1, LANES))],
        out_specs=[rows(D_MODEL), pl.BlockSpec((ROW_TILE * SLABS, LANES), lambda i: (i, 0)), rows(LANES), rows(LANES)],
        out_shape=[jax.ShapeDtypeStruct((T_ALL, D_MODEL), F32), jax.ShapeDtypeStruct((T_ALL * SLABS, LANES), F32),
                   jax.ShapeDtypeStruct((T_ALL, LANES), jnp.int32), jax.ShapeDtypeStruct((T_ALL, LANES), F32)],
        compiler_params=_params(("arbitrary",)),
        name="outproj_router",
    )(att, ret, x, w_out_bf, mod_l, mod_l, mod_l, nf, wrh, wrl, br)


def _token_tile(ref, t):
    return ref.at[pl.ds(pl.multiple_of(t * SLABS, SLABS), SLABS), :]


def _step_is_used(i, nused_ref):
    return i * MOE_STEP_BLOCKS < nused_ref[0]


def _dispatch_kernel(nused_ref, src_ref, h2t_hbm, o_ref, hres, xg, sem):
    i = pl.program_id(0)

    @pl.when(i == 0)
    def _():
        cp = pltpu.make_async_copy(h2t_hbm, hres.at[pl.ds(0, T_ALL * SLABS), :], sem.at[0])
        cp.start()
        _token_tile(hres, T_ALL)[...] = jnp.zeros((SLABS, LANES), F32)
        cp.wait()

    @pl.when(_step_is_used(i, nused_ref))
    def _():
        def body(r, carry):
            _token_tile(xg, r)[...] = _token_tile(hres, src_ref[0, 0, r])[...]
            return carry

        lax.fori_loop(0, MOE_STEP_ROWS, body, 0, unroll=8)
        for s in range(SLABS):
            o_ref[:, s * LANES:(s + 1) * LANES] = xg[pl.ds(s, MOE_STEP_ROWS, stride=SLABS), :].astype(BF16)

    @pl.when(jnp.logical_not(_step_is_used(i, nused_ref)))
    def _():
        o_ref[...] = jnp.zeros(o_ref.shape, BF16)


def _smem_rows(width):
    return pl.BlockSpec((1, 1, width), lambda i, nused: (jnp.minimum(i, N_MOE_STEPS - 1), 0, 0),
                        memory_space=pltpu.SMEM)


def _moe_dispatch(n_used, src3, h2t):
    return pl.pallas_call(
        _dispatch_kernel,
        grid_spec=pltpu.PrefetchScalarGridSpec(
            num_scalar_prefetch=1,
            grid=(N_MOE_STEPS,),
            in_specs=[_smem_rows(MOE_STEP_ROWS), pl.BlockSpec(memory_space=pl.ANY)],
            out_specs=pl.BlockSpec((MOE_STEP_ROWS, D_MODEL), lambda i, nused: (i, 0)),
            scratch_shapes=[pltpu.VMEM((N_TILES * SLABS, LANES), F32), pltpu.VMEM((MOE_STEP_ROWS * SLABS, LANES), F32),
                            pltpu.SemaphoreType.DMA((1,))],
        ),
        out_shape=jax.ShapeDtypeStruct((N_PAD, D_MODEL), BF16),
        compiler_params=_params(("arbitrary",)),
        name="moe_dispatch",
    )(n_used, src3, h2t)


def _block_rows(ref, g):
    return ref.at[pl.ds(pl.multiple_of(g * MOE_ROWS, MOE_ROWS), MOE_ROWS), :]


def _experts_kernel(blk0_ref, nblk_ref, nused_ref, xs_hbm, wgu_ref, bgu_ref, wdn_ref, bdn_ref, ys_hbm,
                    xbuf, ybuf, wgu_bf, wdn_bf, xsem, ysem):
    e = pl.program_id(0)
    b0, nb, nused = blk0_ref[e], nblk_ref[e], nused_ref[0]

    def x_copy(g, slot):
        return pltpu.make_async_copy(_block_rows(xs_hbm, g), xbuf.at[slot], xsem.at[slot])

    def y_copy(g, slot):
        return pltpu.make_async_copy(ybuf.at[slot], _block_rows(ys_hbm, g), ysem.at[slot])

    @pl.when(jnp.logical_and(e == 0, nused > 0))
    def _():
        x_copy(0, 0).start()

    @pl.when(nb > 0)
    def _():
        wgu_bf[...] = wgu_ref[...].astype(BF16)
        wdn_bf[...] = wdn_ref[...].astype(BF16)

        def body(j, carry):
            g = b0 + j
            slot = g % 2
            x_copy(g, slot).wait()

            @pl.when(g + 1 < nused)
            def _():
                x_copy(g + 1, 1 - slot).start()

            gu = jnp.dot(xbuf[slot], wgu_bf[...], preferred_element_type=F32) + bgu_ref[...]
            x_glu = jnp.minimum(gu[:, :D_FF], SWIGLU_LIMIT)
            x_lin = jnp.clip(gu[:, D_FF:], -SWIGLU_LIMIT, SWIGLU_LIMIT)
            act = x_glu * jax.nn.sigmoid(SWIGLU_ALPHA * x_glu) * (x_lin + 1.0)
            out = jnp.dot(act.astype(BF16), wdn_bf[...], preferred_element_type=F32) + bdn_ref[...]

            @pl.when(g >= 2)
            def _():
                y_copy(g - 2, slot).wait()

            ybuf[slot] = out
            y_copy(g, slot).start()
            return carry

        lax.fori_loop(0, nb, body, 0)

    @pl.when(e == N_EXPERTS - 1)
    def _():
        @pl.when(nused >= 2)
        def _():
            y_copy(nused - 2, nused % 2).wait()

        @pl.when(nused >= 1)
        def _():
            y_copy(nused - 1, (nused - 1) % 2).wait()

        ybuf[0] = jnp.zeros((MOE_ROWS, D_MODEL), F32)

        def fill(g, carry):
            cp = y_copy(g, 0)
            cp.start()
            cp.wait()
            return carry

        lax.fori_loop(nused, N_MOE_BLOCKS, fill, 0)


def _moe_experts(blk0, nblk, n_used, xs, w_gu, b_gu, w_dn, b_dn, layer):
    wspec = lambda rows, cols: pl.BlockSpec((None, None, rows, cols), lambda e, *_: (layer, e, 0, 0))
    return pl.pallas_call(
        _experts_kernel,
        grid_spec=pltpu.PrefetchScalarGridSpec(
            num_scalar_prefetch=3,
            grid=(N_EXPERTS,),
            in_specs=[pl.BlockSpec(memory_space=pl.ANY),
                      wspec(D_MODEL, 2 * D_FF), wspec(1, 2 * D_FF), wspec(D_FF, D_MODEL), wspec(1, D_MODEL)],
            out_specs=pl.BlockSpec(memory_space=pl.ANY),
            scratch_shapes=[pltpu.VMEM((2, MOE_ROWS, D_MODEL), BF16), pltpu.VMEM((2, MOE_ROWS, D_MODEL), F32),
                            pltpu.VMEM((D_MODEL, 2 * D_FF), BF16), pltpu.VMEM((D_FF, D_MODEL), BF16),
                            pltpu.SemaphoreType.DMA((2,)), pltpu.SemaphoreType.DMA((2,))],
        ),
        out_shape=jax.ShapeDtypeStruct((N_PAD, D_MODEL), F32),
        compiler_params=_params(("arbitrary",)),
        name="moe_experts",
    )(blk0, nblk, n_used, xs, w_gu, b_gu.reshape(DEPTH, N_EXPERTS, 1, 2 * D_FF),
      w_dn, b_dn.reshape(DEPTH, N_EXPERTS, 1, D_MODEL))


N_ROW_TILES = T_ALL // ROW_TILE
ZERO_ROWS = ROW_TILE * SLABS


def _combine_kernel(nused_ref, src_ref, w_ref, ys_ref, x1_ref, g2_ref, o_ref, yres, ot):
    i = pl.program_id(0)

    @pl.when(i == 0)
    def _():
        def zero(j, carry):
            yres[pl.ds(pl.multiple_of(j * ZERO_ROWS, ZERO_ROWS), ZERO_ROWS), :] = jnp.zeros((ZERO_ROWS, LANES), F32)
            return carry
        lax.fori_loop(0, N_ROW_TILES, zero, 0)
        _token_tile(yres, T_ALL)[...] = jnp.zeros((SLABS, LANES), F32)

    @pl.when(jnp.logical_and(i < N_MOE_STEPS, _step_is_used(i, nused_ref)))
    def _():
        for s in range(SLABS):
            ot[pl.ds(s, MOE_STEP_ROWS, stride=SLABS), :] = ys_ref[:, s * LANES:(s + 1) * LANES]

        def group(g, carry):
            rows = [g * GROUP + j for j in range(GROUP)]
            toks = [src_ref[0, 0, r] for r in rows]
            ws = [w_ref[0, 0, r] for r in rows]
            new = [_token_tile(yres, t)[...] + w * _token_tile(ot, r)[...] for t, w, r in zip(toks, ws, rows)]
            for t, v in zip(toks, new):
                _token_tile(yres, t)[...] = v
            return carry

        lax.fori_loop(0, MOE_STEP_ROWS // GROUP, group, 0)

    @pl.when(i >= N_MOE_STEPS)
    def _():
        base = (i - N_MOE_STEPS) * ZERO_ROWS
        for s in range(SLABS):
            cols = slice(s * LANES, (s + 1) * LANES)
            y = yres[pl.ds(base + s, ROW_TILE, stride=SLABS), :]
            o_ref[:, cols] = x1_ref[:, cols] + g2_ref[0][:, cols] * y


def _moe_combine(n_used, src3, w3, ys, x1, mod_l):
    tile = lambda i: jnp.maximum(i - N_MOE_STEPS, 0)
    last_used = lambda nused: jnp.maximum(nused[0] - 1, 0) // MOE_STEP_BLOCKS
    return pl.pallas_call(
        _combine_kernel,
        grid_spec=pltpu.PrefetchScalarGridSpec(
            num_scalar_prefetch=1,
            grid=(N_MOE_STEPS + N_ROW_TILES,),
            in_specs=[_smem_rows(MOE_STEP_ROWS), _smem_rows(MOE_STEP_ROWS),
                      pl.BlockSpec((MOE_STEP_ROWS, D_MODEL), lambda i, nused: (jnp.minimum(i, last_used(nused)), 0)),
                      pl.BlockSpec((ROW_TILE, D_MODEL), lambda i, nused: (tile(i), 0)),
                      pl.BlockSpec((1, 1, D_MODEL), lambda i, nused: (_cond_of_tile(tile(i)), 0, 5))],
            out_specs=pl.BlockSpec((ROW_TILE, D_MODEL), lambda i, nused: (tile(i), 0)),
            scratch_shapes=[pltpu.VMEM((N_TILES * SLABS, LANES), F32),
                            pltpu.VMEM((MOE_STEP_ROWS * SLABS, LANES), F32)],
        ),
        out_shape=jax.ShapeDtypeStruct((T_ALL, D_MODEL), F32),
        compiler_params=_params(("arbitrary",)),
        name="moe_combine",
    )(n_used, src3, w3, ys, x1, mod_l)


def _routing_tables(top_idx, top_w):
    flat_e = top_idx.reshape(N_ASSIGN)
    experts = jnp.arange(N_EXPERTS, dtype=jnp.int32)
    counts = jnp.sum((flat_e[:, None] == experts[None, :]).astype(jnp.int32), axis=0)
    pad = (-counts) % MOE_ROWS
    spare = jnp.arange(MOE_ROWS, dtype=jnp.int32)
    pad_keys = jnp.where(spare[None, :] < pad[:, None], experts[:, None], N_EXPERTS).reshape(-1)
    keys = jnp.concatenate([flat_e, pad_keys])
    toks = jnp.concatenate([jnp.arange(N_ASSIGN, dtype=jnp.int32) // TOP_K,
                            jnp.full((N_PAD - N_ASSIGN,), T_ALL, jnp.int32)])
    wts = jnp.concatenate([top_w.reshape(N_ASSIGN), jnp.zeros((N_PAD - N_ASSIGN,), F32)])
    _, src, w_sorted = lax.sort((keys, toks, wts), num_keys=1, is_stable=True)
    padded = counts + pad
    blk_end = jnp.cumsum(padded) // MOE_ROWS
    nblk = padded // MOE_ROWS
    return (src.reshape(N_MOE_STEPS, 1, MOE_STEP_ROWS), w_sorted.reshape(N_MOE_STEPS, 1, MOE_STEP_ROWS),
            (blk_end - nblk).astype(jnp.int32), nblk.astype(jnp.int32), blk_end[-1:].astype(jnp.int32))


def kernel(x_prompt, x_sample, cache_attn_k, cache_attn_v, state_ret, c, c_ctx, norm_mix, norm_ffn, w_ada, b_ada,
           w_in, q_norm, k_norm, attn_sink, ret_decay, ret_norm, w_out, w_router, b_router, w_gate_up, b_gate_up,
           w_down, b_down):
    x = jnp.concatenate([x_prompt.reshape(T_CTX, D_MODEL), x_sample.reshape(T_LAT, D_MODEL)], axis=0)
    cond = jnp.zeros((COND_PAD, D_MODEL), F32).at[0].set(c_ctx).at[1:N_COND].set(c)
    mod = _modulation(cond, w_ada, b_ada)[:, :N_COND].reshape(DEPTH, N_COND, 1, 6 * D_MODEL)

    cache_k = cache_attn_k.reshape(DEC_BATCH, DEPTH, PAST_LEN, KV_WIDTH)
    cache_v = cache_attn_v.reshape(DEC_BATCH, DEPTH, PAST_LEN, KV_WIDTH)
    cosq, sinq, cosk, sin_k = _rope_tables()
    grp = jnp.arange(ATTN_WIDTH) // HEAD_DIM
    avgq = jnp.where(grp[:, None] == grp[None, :], 1.0 / HEAD_DIM, 0.0).astype(BF16)
    avgk = avgq[:KV_WIDTH, :KV_WIDTH]
    log_gamma = jax.nn.log_sigmoid(ret_decay.astype(F32))
    chunk_decay = jnp.exp(log_gamma * CHUNK)

    new_k, new_v, new_s = [], [], []
    for l in range(DEPTH):
        mod_l = mod[l]
        q, k, v, rq, rk, rv, sg = _input_projection(
            x, norm_mix[l].reshape(1, D_MODEL), mod_l, w_in[l].astype(BF16),
            jnp.tile(q_norm[l], N_Q_HEADS).reshape(1, ATTN_WIDTH), jnp.tile(k_norm[l], N_KV_HEADS).reshape(1, KV_WIDTH),
            avgq, avgk)
        new_k.append(k[:T_CTX].reshape(BATCH, SEQ, N_KV_HEADS, HEAD_DIM))
        new_v.append(v[:T_CTX].reshape(BATCH, SEQ, N_KV_HEADS, HEAD_DIM))
        sink = attn_sink[l].astype(F32)
        att_c = _context_attention(sink, q, k, v)
        att_l = _latent_attention(sink, q, k, v, cache_k, cache_v, l, cosq, sinq, cosk, sin_k)
        lg = log_gamma[l].reshape(2 * N_RET_HEADS)
        cd = chunk_decay[l].reshape(2 * N_RET_HEADS)
        gn = ret_norm[l].reshape(1, RET_WIDTH)
        ret_c, s_fin = _retention(lg, cd, rq, rk, rv, sg, gn, None, l, n_seq=BATCH, seq_len=SEQ, row0=0,
                                  write_state=True)
        (ret_l,) = _retention(lg, cd, rq, rk, rv, sg, gn, state_ret, l, n_seq=DEC_BATCH, seq_len=DEC_SEQ,
                              row0=T_CTX, write_state=False)
        new_s.append(s_fin)
        att = jnp.concatenate([att_c, att_l], axis=0)
        ret = jnp.concatenate([ret_c, ret_l], axis=0)
        wr = jnp.zeros((D_MODEL, LANES), F32).at[:, :N_EXPERTS].set(w_router[l])
        wrh = wr.astype(BF16)
        wrl = (wr - wrh.astype(F32)).astype(BF16)
        br = jnp.full((1, LANES), NEG_BIG, F32).at[0, :N_EXPERTS].set(b_router[l])
        x1, h2t, ti, tw = _output_projection(att, ret, x, w_out[l].astype(BF16), mod_l,
                                             norm_ffn[l].reshape(1, D_MODEL), wrh, wrl, br)
        src3, w3, blk0, nblk, n_used = _routing_tables(ti[:, :TOP_K], tw[:, :TOP_K])
        xs = _moe_dispatch(n_used, src3, h2t)
        ys = _moe_experts(blk0, nblk, n_used, xs, w_gate_up, b_gate_up, w_down, b_down, l)
        x = _moe_combine(n_used, src3, w3, ys, x1, mod_l)

    y_prompt = x[:T_CTX].reshape(BATCH, SEQ, D_MODEL)
    y_sample = x[T_CTX:].reshape(DEC_BATCH, DEC_SEQ, D_MODEL)
    return (y_prompt, y_sample, jnp.stack(new_k, axis=1), jnp.stack(new_v, axis=1), jnp.stack(new_s, axis=1))
```

```python
import functools

import jax
import jax.numpy as jnp
from jax import lax
from jax.experimental import pallas as pl
from jax.experimental.pallas import tpu as pltpu

F32 = jnp.float32
BF16 = jnp.bfloat16

D_MODEL = 1024
DEPTH = 4
BATCH, SEQ = 16, 256
DEC_BATCH, DEC_SEQ = 2, 1024
PAST_LEN = 512
GRID_W = 64
HEAD_DIM = 64
N_Q_HEADS = 8
N_KV_HEADS = 2
ATTN_WIDTH = N_Q_HEADS * HEAD_DIM
KV_WIDTH = N_KV_HEADS * HEAD_DIM
WINDOW = 128
BLOCK = 128
ROPE_BASE = 10000.0
N_RET_HEADS = 4
RET_DK = 128
RET_WIDTH = N_RET_HEADS * RET_DK
CHUNK = 128
IN_WIDTH = ATTN_WIDTH + 2 * KV_WIDTH + 4 * RET_WIDTH
N_EXPERTS = 32
TOP_K = 4
D_FF = D_MODEL
SWIGLU_LIMIT = 7.0
SWIGLU_ALPHA = 1.702
EPS = 1e-6

T_CTX = BATCH * SEQ
T_LAT = DEC_BATCH * DEC_SEQ
T_ALL = T_CTX + T_LAT
N_COND = 1 + DEC_BATCH
COND_PAD = 8
LANES = 128
ROW_TILE = 512
MOE_ROWS = 128
N_ASSIGN = T_ALL * TOP_K
N_MOE_BLOCKS = N_ASSIGN // MOE_ROWS + N_EXPERTS
N_PAD = N_MOE_BLOCKS * MOE_ROWS
MOE_STEP_BLOCKS = 4
MOE_STEP_ROWS = MOE_STEP_BLOCKS * MOE_ROWS
N_MOE_STEPS = N_MOE_BLOCKS // MOE_STEP_BLOCKS
SLABS = D_MODEL // LANES
GROUP = 8
N_TILES = T_ALL + 1
NEG_BIG = -1e30
VMEM_LIMIT = 48 * 1024 * 1024


def _cond_of_tile(i):
    return jnp.where(i < T_CTX // ROW_TILE, 0, 1 + (i - T_CTX // ROW_TILE) // (DEC_SEQ // ROW_TILE))


def _params(sem, vmem=VMEM_LIMIT):
    return pltpu.CompilerParams(dimension_semantics=sem, vmem_limit_bytes=vmem)


def _mod_kernel(c_ref, w_ref, b_ref, o_ref):
    c = c_ref[...]
    s = (c * jax.nn.sigmoid(c)).astype(BF16)
    o_ref[0] = jnp.dot(s, w_ref[0].astype(BF16), preferred_element_type=F32) + b_ref[0]


def _modulation(cond, w_ada, b_ada):
    n_col = 6 * D_MODEL // D_MODEL
    return pl.pallas_call(
        _mod_kernel,
        grid=(DEPTH, n_col),
        in_specs=[
            pl.BlockSpec((COND_PAD, D_MODEL), lambda l, j: (0, 0)),
            pl.BlockSpec((1, D_MODEL, D_MODEL), lambda l, j: (l, 0, j)),
            pl.BlockSpec((1, 1, D_MODEL), lambda l, j: (l, 0, j)),
        ],
        out_specs=pl.BlockSpec((1, COND_PAD, D_MODEL), lambda l, j: (l, 0, j)),
        out_shape=jax.ShapeDtypeStruct((DEPTH, COND_PAD, 6 * D_MODEL), F32),
        compiler_params=_params(("arbitrary", "arbitrary")),
        name="modulation",
    )(cond, w_ada, b_ada.reshape(DEPTH, 1, 6 * D_MODEL))


def _rms_rows(x, g):
    ms = jnp.mean(x * x, axis=-1, keepdims=True)
    return x * lax.rsqrt(ms + EPS) * g


def _group_rmsnorm(a, avg_ref, g):
    sq = a * a
    hi = sq.astype(BF16)
    lo = (sq - hi.astype(F32)).astype(BF16)
    avg = avg_ref[...]
    ms = jnp.dot(hi, avg, preferred_element_type=F32) + jnp.dot(lo, avg, preferred_element_type=F32)
    return a * lax.rsqrt(ms + EPS) * g


def _inproj_kernel(x_ref, g_ref, sh_ref, sc_ref, w_ref, qn_ref, kn_ref, avgq_ref, avgk_ref,
                   q_ref, k_ref, v_ref, rq_ref, rk_ref, rv_ref, sg_ref):
    h = _rms_rows(x_ref[...], g_ref[...]) * (1.0 + sc_ref[0]) + sh_ref[0]
    hb = h.astype(BF16)

    def proj(lo, width):
        return jnp.dot(hb, w_ref[:, lo:lo + width], preferred_element_type=F32)

    o = 0
    q_ref[...] = _group_rmsnorm(proj(o, ATTN_WIDTH), avgq_ref, qn_ref[...])
    o += ATTN_WIDTH
    k_ref[...] = _group_rmsnorm(proj(o, KV_WIDTH), avgk_ref, kn_ref[...])
    o += KV_WIDTH
    v_ref[...] = proj(o, KV_WIDTH)
    o += KV_WIDTH
    rq_ref[...] = proj(o, RET_WIDTH) * (RET_DK ** -0.5)
    o += RET_WIDTH
    rk_ref[...] = proj(o, RET_WIDTH)
    o += RET_WIDTH
    rv_ref[...] = proj(o, RET_WIDTH)
    o += RET_WIDTH
    rg = proj(o, RET_WIDTH)
    sg_ref[...] = rg * jax.nn.sigmoid(rg)


def _mod_spec(col):
    return pl.BlockSpec((1, 1, D_MODEL), lambda i, col=col: (_cond_of_tile(i), 0, col))


def _full(shape):
    return pl.BlockSpec(shape, lambda *_: (0,) * len(shape))


def _input_projection(x, norm_g, mod_l, w_in_bf, qn, kn, avgq, avgk):
    rows = lambda w: pl.BlockSpec((ROW_TILE, w), lambda i: (i, 0))
    widths = (ATTN_WIDTH, KV_WIDTH, KV_WIDTH, RET_WIDTH, RET_WIDTH, RET_WIDTH, RET_WIDTH)
    return pl.pallas_call(
        _inproj_kernel,
        grid=(T_ALL // ROW_TILE,),
        in_specs=[rows(D_MODEL), _full((1, D_MODEL)), _mod_spec(0), _mod_spec(1),
                  _full((D_MODEL, IN_WIDTH)), _full((1, ATTN_WIDTH)), _full((1, KV_WIDTH)),
                  _full((ATTN_WIDTH, ATTN_WIDTH)), _full((KV_WIDTH, KV_WIDTH))],
        out_specs=[rows(w) for w in widths],
        out_shape=[jax.ShapeDtypeStruct((T_ALL, w), F32) for w in widths],
        compiler_params=_params(("arbitrary",)),
        name="norm_inproj",
    )(x, norm_g, mod_l, mod_l, w_in_bf, qn, kn, avgq, avgk)


def _attend(q, kall, vall, valid, sink_ref, o_ref):
    scale = HEAD_DIM ** -0.5
    lane = lax.broadcasted_iota(jnp.int32, (1, LANES), 1)
    low = lane < HEAD_DIM
    k_at, v_at = [], []
    for g in range(N_KV_HEADS):
        keep = low if g == 0 else jnp.logical_not(low)
        kg = jnp.where(keep, kall, 0.0)
        vg = jnp.where(keep, vall, 0.0)
        kr = pltpu.roll(kg, HEAD_DIM, 1)
        vr = pltpu.roll(vg, HEAD_DIM, 1)
        pair_k = (kg, kr) if g == 0 else (kr, kg)
        pair_v = (vg, vr) if g == 0 else (vr, vg)
        k_at.append([t.astype(BF16) for t in pair_k])
        v_at.append([t.astype(BF16) for t in pair_v])
    for j in range(ATTN_WIDTH // LANES):
        qj = q[:, j * LANES:(j + 1) * LANES].astype(BF16)
        acc = None
        for off in range(2):
            h = 2 * j + off
            g = h // (N_Q_HEADS // N_KV_HEADS)
            s = lax.dot_general(qj, k_at[g][off], (((1,), (1,)), ((), ())), preferred_element_type=F32) * scale
            if valid is not None:
                s = jnp.where(valid, s, NEG_BIG)
            sink = sink_ref[h]
            m = jnp.maximum(jnp.max(s, axis=-1, keepdims=True), sink)
            e = jnp.exp(s - m)
            den = jnp.sum(e, axis=-1, keepdims=True) + jnp.exp(sink - m)
            o = jnp.dot(e.astype(BF16), v_at[g][off], preferred_element_type=F32) / den
            acc = o if acc is None else acc + o
        o_ref[:, j * LANES:(j + 1) * LANES] = acc


def _ctx_attn_kernel(sink_ref, q_ref, k_ref, v_ref, o_ref):
    _attend(q_ref[...], k_ref[...], v_ref[...], None, sink_ref, o_ref)


def _context_attention(sink, q, k, v):
    return pl.pallas_call(
        _ctx_attn_kernel,
        grid_spec=pltpu.PrefetchScalarGridSpec(
            num_scalar_prefetch=1,
            grid=(BATCH,),
            in_specs=[pl.BlockSpec((SEQ, ATTN_WIDTH), lambda b, s: (b, 0)),
                      pl.BlockSpec((SEQ, KV_WIDTH), lambda b, s: (b, 0)),
                      pl.BlockSpec((SEQ, KV_WIDTH), lambda b, s: (b, 0))],
            out_specs=pl.BlockSpec((SEQ, ATTN_WIDTH), lambda b, s: (b, 0)),
        ),
        out_shape=jax.ShapeDtypeStruct((T_CTX, ATTN_WIDTH), F32),
        compiler_params=_params(("arbitrary",)),
        name="context_attention",
    )(sink, q, k, v)


def _rope_block(x, cos, sin_signed):
    lane = lax.broadcasted_iota(jnp.int32, (1, LANES), 1)
    first = (lane % (HEAD_DIM // 2)) < (HEAD_DIM // 4)
    swapped = jnp.where(first, pltpu.roll(x, LANES - HEAD_DIM // 4, 1), pltpu.roll(x, HEAD_DIM // 4, 1))
    return x * cos + swapped * sin_signed


LOCAL_KEYS = 3 * BLOCK


def _lat_attn_kernel(sink_ref, q_ref, k_ref, v_ref, ck_ref, cv_ref, cosq_ref, sinq_ref, cosk_ref, sin_k_ref,
                     o_ref):
    n = pl.program_id(1)
    start = pl.multiple_of(jnp.clip((n - 1) * BLOCK, 0, DEC_SEQ - LOCAL_KEYS), BLOCK)
    q = q_ref[...]
    q = jnp.concatenate(
        [_rope_block(q[:, j * LANES:(j + 1) * LANES], cosq_ref[:, j * LANES:(j + 1) * LANES],
                     sinq_ref[:, j * LANES:(j + 1) * LANES]) for j in range(ATTN_WIDTH // LANES)], axis=1)
    kw = _rope_block(k_ref[pl.ds(start, LOCAL_KEYS), :], cosk_ref[pl.ds(start, LOCAL_KEYS), :],
                     sin_k_ref[pl.ds(start, LOCAL_KEYS), :])
    vw = v_ref[pl.ds(start, LOCAL_KEYS), :]
    kall = jnp.concatenate([kw, ck_ref[0, 0]], axis=0)
    vall = jnp.concatenate([vw, cv_ref[0, 0]], axis=0)
    qpos = n * BLOCK + lax.broadcasted_iota(jnp.int32, (BLOCK, LOCAL_KEYS + PAST_LEN), 0)
    col = lax.broadcasted_iota(jnp.int32, (BLOCK, LOCAL_KEYS + PAST_LEN), 1)
    valid = jnp.logical_or(col >= LOCAL_KEYS, jnp.abs(qpos - (start + col)) <= WINDOW)
    _attend(q, kall, vall, valid, sink_ref, o_ref)


def _latent_attention(sink, q, k, v, cache_k, cache_v, layer, cosq, sinq, cosk, sin_k):
    nb = DEC_SEQ // BLOCK
    ctx_block0 = T_CTX // BLOCK
    ctx_seq0 = T_CTX // DEC_SEQ
    cache_spec = pl.BlockSpec((1, 1, PAST_LEN, KV_WIDTH), lambda b, n, s: (b, layer, 0, 0))
    return pl.pallas_call(
        _lat_attn_kernel,
        grid_spec=pltpu.PrefetchScalarGridSpec(
            num_scalar_prefetch=1,
            grid=(DEC_BATCH, nb),
            in_specs=[pl.BlockSpec((BLOCK, ATTN_WIDTH), lambda b, n, s: (ctx_block0 + b * nb + n, 0)),
                      pl.BlockSpec((DEC_SEQ, KV_WIDTH), lambda b, n, s: (ctx_seq0 + b, 0)),
                      pl.BlockSpec((DEC_SEQ, KV_WIDTH), lambda b, n, s: (ctx_seq0 + b, 0)),
                      cache_spec, cache_spec,
                      pl.BlockSpec((BLOCK, ATTN_WIDTH), lambda b, n, s: (n, 0)),
                      pl.BlockSpec((BLOCK, ATTN_WIDTH), lambda b, n, s: (n, 0)),
                      pl.BlockSpec((DEC_SEQ, KV_WIDTH), lambda b, n, s: (0, 0)),
                      pl.BlockSpec((DEC_SEQ, KV_WIDTH), lambda b, n, s: (0, 0))],
            out_specs=pl.BlockSpec((BLOCK, ATTN_WIDTH), lambda b, n, s: (b * nb + n, 0)),
        ),
        out_shape=jax.ShapeDtypeStruct((T_LAT, ATTN_WIDTH), F32),
        compiler_params=_params(("arbitrary", "arbitrary")),
        name="latent_attention",
    )(sink, q, k, v, cache_k, cache_v, cosq, sinq, cosk, sin_k)


def _rope_tables():
    t = jnp.arange(DEC_SEQ)
    nf = HEAD_DIM // 4
    inv = ROPE_BASE ** (-jnp.arange(nf, dtype=F32) / nf)

    def half(coord):
        ang = coord.astype(F32)[:, None] * inv[None, :]
        c, s = jnp.cos(ang), jnp.sin(ang)
        return jnp.concatenate([c, c], axis=1), jnp.concatenate([-s, s], axis=1)

    cr, sr = half(t // GRID_W)
    cc, sc = half(t % GRID_W)
    cos = jnp.concatenate([cr, cc], axis=1)
    sin = jnp.concatenate([sr, sc], axis=1)
    return (jnp.tile(cos, (1, N_Q_HEADS)), jnp.tile(sin, (1, N_Q_HEADS)),
            jnp.tile(cos, (1, N_KV_HEADS)), jnp.tile(sin, (1, N_KV_HEADS)))


def _ret_kernel(lg_ref, cd_ref, q_ref, k_ref, v_ref, sg_ref, gn_ref, *rest, n_chunks, has_s0, write_state):
    rest = list(rest)
    s0_ref = rest.pop(0) if has_s0 else None
    o_ref = rest.pop(0)
    sf_ref = rest.pop(0) if write_state else None
    acc_ref = rest.pop(0)
    h = pl.program_id(1)
    row = lax.broadcasted_iota(jnp.int32, (CHUNK, CHUNK), 0).astype(F32)
    col = lax.broadcasted_iota(jnp.int32, (CHUNK, CHUNK), 1).astype(F32)
    rel = row - col
    pos = lax.broadcasted_iota(jnp.int32, (CHUNK, 1), 0).astype(F32)
    gn = gn_ref[...]

    def run(direction):
        lg = lg_ref[direction * N_RET_HEADS + h]
        cd = cd_ref[direction * N_RET_HEADS + h]
        if direction == 0:
            intra = jnp.where(rel >= 0, jnp.exp(lg * rel), 0.0)
            q_dec = jnp.exp(lg * (pos + 1.0))
            k_dec = jnp.exp(lg * (CHUNK - 1.0 - pos))
            order = range(n_chunks)
        else:
            intra = jnp.where(rel <= 0, jnp.exp(-lg * rel), 0.0)
            q_dec = jnp.exp(lg * (CHUNK - pos))
            k_dec = jnp.exp(lg * pos)
            order = range(n_chunks - 1, -1, -1)
        state = s0_ref[0, 0, direction, 0] if has_s0 else jnp.zeros((RET_DK, RET_DK), F32)
        for c in order:
            rows = slice(c * CHUNK, (c + 1) * CHUNK)
            qc, kc, vc = q_ref[rows, :], k_ref[rows, :], v_ref[rows, :]
            qb, kb, vb = qc.astype(BF16), kc.astype(BF16), vc.astype(BF16)
            scores = lax.dot_general(qb, kb, (((1,), (1,)), ((), ())), preferred_element_type=F32) * intra
            o = (jnp.dot(scores.astype(BF16), vb, preferred_element_type=F32)
                 + jnp.dot(qb, state.astype(BF16), preferred_element_type=F32) * q_dec)
            kd_t = (kc * k_dec).T.astype(BF16)
            state = state * cd + jnp.dot(kd_t, vb, preferred_element_type=F32)
            if direction == 0:
                acc_ref[rows, :] = o
            else:
                tot = acc_ref[rows, :] + o
                o_ref[rows, :] = _rms_rows(tot, gn) * sg_ref[rows, :]
        if write_state:
            sf_ref[0, direction, 0] = state

    run(0)
    run(1)


def _retention(lg, cd, rq, rk, rv, sg, gn, s0, layer, *, n_seq, seq_len, row0, write_state):
    blk0 = row0 // seq_len
    rows = pl.BlockSpec((seq_len, RET_DK), lambda b, h, *_: (blk0 + b, h))
    in_specs = [rows, rows, rows, rows, pl.BlockSpec((1, RET_DK), lambda b, h, *_: (0, h))]
    args = [rq, rk, rv, sg, gn]
    if s0 is not None:
        in_specs.append(pl.BlockSpec((1, 1, 2, 1, RET_DK, RET_DK), lambda b, h, *_: (b, layer, 0, h, 0, 0)))
        args.append(s0)
    out_specs = [pl.BlockSpec((seq_len, RET_DK), lambda b, h, *_: (b, h))]
    out_shape = [jax.ShapeDtypeStruct((n_seq * seq_len, RET_WIDTH), F32)]
    if write_state:
        out_specs.append(pl.BlockSpec((1, 2, 1, RET_DK, RET_DK), lambda b, h, *_: (b, 0, h, 0, 0)))
        out_shape.append(jax.ShapeDtypeStruct((n_seq, 2, N_RET_HEADS, RET_DK, RET_DK), F32))
    kern = functools.partial(_ret_kernel, n_chunks=seq_len // CHUNK, has_s0=s0 is not None,
                             write_state=write_state)
    return pl.pallas_call(
        kern,
        grid_spec=pltpu.PrefetchScalarGridSpec(
            num_scalar_prefetch=2,
            grid=(n_seq, N_RET_HEADS),
            in_specs=in_specs,
            out_specs=out_specs,
            scratch_shapes=[pltpu.VMEM((seq_len, RET_DK), F32)],
        ),
        out_shape=out_shape,
        compiler_params=_params(("arbitrary", "arbitrary")),
        name="retention_ctx" if write_state else "retention_lat",
    )(lg, cd, *args)


def _outproj_kernel(att_ref, ret_ref, x_ref, wo_ref, g1_ref, sh2_ref, sc2_ref, nf_ref, wrh_ref, wrl_ref, br_ref,
                    x1_ref, h2t_ref, ti_ref, tw_ref):
    y = (jnp.dot(att_ref[...].astype(BF16), wo_ref[0:ATTN_WIDTH, :], preferred_element_type=F32)
         + jnp.dot(ret_ref[...].astype(BF16), wo_ref[ATTN_WIDTH:, :], preferred_element_type=F32))
    x1 = x_ref[...] + g1_ref[0] * y
    x1_ref[...] = x1
    h2 = _rms_rows(x1, nf_ref[...]) * (1.0 + sc2_ref[0]) + sh2_ref[0]
    for s in range(SLABS):
        h2t_ref[pl.ds(s, ROW_TILE, stride=SLABS), :] = h2[:, s * LANES:(s + 1) * LANES]
    hh = h2.astype(BF16)
    hl = (h2 - hh.astype(F32)).astype(BF16)
    wrh = wrh_ref[...]
    logits = (jnp.dot(hh, wrh, preferred_element_type=F32) + jnp.dot(hl, wrh, preferred_element_type=F32)
              + jnp.dot(hh, wrl_ref[...], preferred_element_type=F32) + br_ref[...])
    lane = lax.broadcasted_iota(jnp.int32, logits.shape, 1)
    vals, idxs = [], []
    cur = logits
    for _ in range(TOP_K):
        m = jnp.max(cur, axis=-1, keepdims=True)
        idx = jnp.min(jnp.where(cur == m, lane, LANES), axis=-1, keepdims=True)
        vals.append(m)
        idxs.append(idx)
        cur = jnp.where(lane == idx, -jnp.inf, cur)
    es = [jnp.exp(v - vals[0]) for v in vals]
    den = es[0] + es[1] + es[2] + es[3]
    ti = jnp.zeros(logits.shape, jnp.int32)
    tw = jnp.zeros(logits.shape, F32)
    for k in range(TOP_K):
        ti = jnp.where(lane == k, idxs[k], ti)
        tw = jnp.where(lane == k, es[k] / den, tw)
    ti_ref[...] = ti
    tw_ref[...] = tw


def _output_projection(att, ret, x, w_out_bf, mod_l, nf, wrh, wrl, br):
    rows = lambda w: pl.BlockSpec((ROW_TILE, w), lambda i: (i, 0))
    return pl.pallas_call(
        _outproj_kernel,
        grid=(T_ALL // ROW_TILE,),
        in_specs=[rows(ATTN_WIDTH), rows(RET_WIDTH), rows(D_MODEL), _full((D_MODEL, D_MODEL)),
                  _mod_spec(2), _mod_spec(3), _mod_spec(4), _full((1, D_MODEL)),
                  _full((D_MODEL, LANES)), _full((D_MODEL, LANES)), _full((1, LANES))],
        out_specs=[rows(D_MODEL), pl.BlockSpec((ROW_TILE * SLABS, LANES), lambda i: (i, 0)), rows(LANES), rows(LANES)],
        out_shape=[jax.ShapeDtypeStruct((T_ALL, D_MODEL), F32), jax.ShapeDtypeStruct((T_ALL * SLABS, LANES), F32),
                   jax.ShapeDtypeStruct((T_ALL, LANES), jnp.int32), jax.ShapeDtypeStruct((T_ALL, LANES), F32)],
        compiler_params=_params(("arbitrary",)),
        name="outproj_router",
    )(att, ret, x, w_out_bf, mod_l, mod_l, mod_l, nf, wrh, wrl, br)


def _token_tile(ref, t):
    return ref.at[pl.ds(pl.multiple_of(t * SLABS, SLABS), SLABS), :]


def _step_is_used(i, nused_ref):
    return i * MOE_STEP_BLOCKS < nused_ref[0]


def _dispatch_kernel(nused_ref, src_ref, h2t_hbm, o_ref, hres, xg, sem):
    i = pl.program_id(0)

    @pl.when(i == 0)
    def _():
        cp = pltpu.make_async_copy(h2t_hbm, hres.at[pl.ds(0, T_ALL * SLABS), :], sem.at[0])
        cp.start()
        _token_tile(hres, T_ALL)[...] = jnp.zeros((SLABS, LANES), F32)
        cp.wait()

    @pl.when(_step_is_used(i, nused_ref))
    def _():
        def body(r, carry):
            _token_tile(xg, r)[...] = _token_tile(hres, src_ref[0, 0, r])[...]
            return carry

        lax.fori_loop(0, MOE_STEP_ROWS, body, 0, unroll=8)
        for s in range(SLABS):
            o_ref[:, s * LANES:(s + 1) * LANES] = xg[pl.ds(s, MOE_STEP_ROWS, stride=SLABS), :].astype(BF16)

    @pl.when(jnp.logical_not(_step_is_used(i, nused_ref)))
    def _():
        o_ref[...] = jnp.zeros(o_ref.shape, BF16)


def _smem_rows(width):
    return pl.BlockSpec((1, 1, width), lambda i, nused: (jnp.minimum(i, N_MOE_STEPS - 1), 0, 0),
                        memory_space=pltpu.SMEM)


def _moe_dispatch(n_used, src3, h2t):
    return pl.pallas_call(
        _dispatch_kernel,
        grid_spec=pltpu.PrefetchScalarGridSpec(
            num_scalar_prefetch=1,
            grid=(N_MOE_STEPS,),
            in_specs=[_smem_rows(MOE_STEP_ROWS), pl.BlockSpec(memory_space=pl.ANY)],
            out_specs=pl.BlockSpec((MOE_STEP_ROWS, D_MODEL), lambda i, nused: (i, 0)),
            scratch_shapes=[pltpu.VMEM((N_TILES * SLABS, LANES), F32), pltpu.VMEM((MOE_STEP_ROWS * SLABS, LANES), F32),
                            pltpu.SemaphoreType.DMA((1,))],
        ),
        out_shape=jax.ShapeDtypeStruct((N_PAD, D_MODEL), BF16),
        compiler_params=_params(("arbitrary",)),
        name="moe_dispatch",
    )(n_used, src3, h2t)


ROW_DMA_PRIORITY = 1


def _block_rows(ref, g):
    return ref.at[pl.ds(pl.multiple_of(g * MOE_ROWS, MOE_ROWS), MOE_ROWS), :]


def _experts_kernel(blk0_ref, nblk_ref, nused_ref, xs_hbm, wgu_ref, bgu_ref, wdn_ref, bdn_ref, ys_hbm,
                    xbuf, ybuf, wgu_bf, wdn_bf, xsem, ysem):
    e = pl.program_id(0)
    b0, nb, nused = blk0_ref[e], nblk_ref[e], nused_ref[0]

    def x_copy(g, slot):
        return pltpu.make_async_copy(_block_rows(xs_hbm, g), xbuf.at[slot], xsem.at[slot])

    def y_copy(g, slot):
        return pltpu.make_async_copy(ybuf.at[slot], _block_rows(ys_hbm, g), ysem.at[slot])

    @pl.when(jnp.logical_and(e == 0, nused > 0))
    def _():
        x_copy(0, 0).start(priority=ROW_DMA_PRIORITY)

    @pl.when(nb > 0)
    def _():
        wgu_bf[...] = wgu_ref[...].astype(BF16)
        wdn_bf[...] = wdn_ref[...].astype(BF16)

        def body(j, carry):
            g = b0 + j
            slot = g % 2
            x_copy(g, slot).wait()

            @pl.when(g + 1 < nused)
            def _():
                x_copy(g + 1, 1 - slot).start(priority=ROW_DMA_PRIORITY)

            gu = jnp.dot(xbuf[slot], wgu_bf[...], preferred_element_type=F32) + bgu_ref[...]
            x_glu = jnp.minimum(gu[:, :D_FF], SWIGLU_LIMIT)
            x_lin = jnp.clip(gu[:, D_FF:], -SWIGLU_LIMIT, SWIGLU_LIMIT)
            act = x_glu * jax.nn.sigmoid(SWIGLU_ALPHA * x_glu) * (x_lin + 1.0)
            out = jnp.dot(act.astype(BF16), wdn_bf[...], preferred_element_type=F32) + bdn_ref[...]

            @pl.when(g >= 2)
            def _():
                y_copy(g - 2, slot).wait()

            ybuf[slot] = out
            y_copy(g, slot).start(priority=ROW_DMA_PRIORITY)
            return carry

        lax.fori_loop(0, nb, body, 0)

    @pl.when(e == N_EXPERTS - 1)
    def _():
        @pl.when(nused >= 2)
        def _():
            y_copy(nused - 2, nused % 2).wait()

        @pl.when(nused >= 1)
        def _():
            y_copy(nused - 1, (nused - 1) % 2).wait()

        ybuf[0] = jnp.zeros((MOE_ROWS, D_MODEL), F32)

        def fill(g, carry):
            cp = y_copy(g, 0)
            cp.start()
            cp.wait()
            return carry

        lax.fori_loop(nused, N_MOE_BLOCKS, fill, 0)


def _moe_experts(blk0, nblk, n_used, xs, w_gu, b_gu, w_dn, b_dn, layer):
    wspec = lambda rows, cols: pl.BlockSpec((None, None, rows, cols), lambda e, *_: (layer, e, 0, 0))
    return pl.pallas_call(
        _experts_kernel,
        grid_spec=pltpu.PrefetchScalarGridSpec(
            num_scalar_prefetch=3,
            grid=(N_EXPERTS,),
            in_specs=[pl.BlockSpec(memory_space=pl.ANY),
                      wspec(D_MODEL, 2 * D_FF), wspec(1, 2 * D_FF), wspec(D_FF, D_MODEL), wspec(1, D_MODEL)],
            out_specs=pl.BlockSpec(memory_space=pl.ANY),
            scratch_shapes=[pltpu.VMEM((2, MOE_ROWS, D_MODEL), BF16), pltpu.VMEM((2, MOE_ROWS, D_MODEL), F32),
                            pltpu.VMEM((D_MODEL, 2 * D_FF), BF16), pltpu.VMEM((D_FF, D_MODEL), BF16),
                            pltpu.SemaphoreType.DMA((2,)), pltpu.SemaphoreType.DMA((2,))],
        ),
        out_shape=jax.ShapeDtypeStruct((N_PAD, D_MODEL), F32),
        compiler_params=_params(("arbitrary",)),
        name="moe_experts",
    )(blk0, nblk, n_used, xs, w_gu, b_gu.reshape(DEPTH, N_EXPERTS, 1, 2 * D_FF),
      w_dn, b_dn.reshape(DEPTH, N_EXPERTS, 1, D_MODEL))


N_ROW_TILES = T_ALL // ROW_TILE
ZERO_ROWS = ROW_TILE * SLABS


def _combine_kernel(nused_ref, src_ref, w_ref, ys_ref, x1_ref, g2_ref, o_ref, yres, ot):
    i = pl.program_id(0)

    @pl.when(i == 0)
    def _():
        def zero(j, carry):
            yres[pl.ds(pl.multiple_of(j * ZERO_ROWS, ZERO_ROWS), ZERO_ROWS), :] = jnp.zeros((ZERO_ROWS, LANES), F32)
            return carry
        lax.fori_loop(0, N_ROW_TILES, zero, 0)
        _token_tile(yres, T_ALL)[...] = jnp.zeros((SLABS, LANES), F32)

    @pl.when(jnp.logical_and(i < N_MOE_STEPS, _step_is_used(i, nused_ref)))
    def _():
        for s in range(SLABS):
            ot[pl.ds(s, MOE_STEP_ROWS, stride=SLABS), :] = ys_ref[:, s * LANES:(s + 1) * LANES]

        def group(g, carry):
            rows = [g * GROUP + j for j in range(GROUP)]
            toks = [src_ref[0, 0, r] for r in rows]
            ws = [w_ref[0, 0, r] for r in rows]
            new = [_token_tile(yres, t)[...] + w * _token_tile(ot, r)[...] for t, w, r in zip(toks, ws, rows)]
            for t, v in zip(toks, new):
                _token_tile(yres, t)[...] = v
            return carry

        lax.fori_loop(0, MOE_STEP_ROWS // GROUP, group, 0)

    @pl.when(i >= N_MOE_STEPS)
    def _():
        base = (i - N_MOE_STEPS) * ZERO_ROWS
        for s in range(SLABS):
            cols = slice(s * LANES, (s + 1) * LANES)
            y = yres[pl.ds(base + s, ROW_TILE, stride=SLABS), :]
            o_ref[:, cols] = x1_ref[:, cols] + g2_ref[0][:, cols] * y


def _moe_combine(n_used, src3, w3, ys, x1, mod_l):
    tile = lambda i: jnp.maximum(i - N_MOE_STEPS, 0)
    last_used = lambda nused: jnp.maximum(nused[0] - 1, 0) // MOE_STEP_BLOCKS
    return pl.pallas_call(
        _combine_kernel,
        grid_spec=pltpu.PrefetchScalarGridSpec(
            num_scalar_prefetch=1,
            grid=(N_MOE_STEPS + N_ROW_TILES,),
            in_specs=[_smem_rows(MOE_STEP_ROWS), _smem_rows(MOE_STEP_ROWS),
                      pl.BlockSpec((MOE_STEP_ROWS, D_MODEL), lambda i, nused: (jnp.minimum(i, last_used(nused)), 0)),
                      pl.BlockSpec((ROW_TILE, D_MODEL), lambda i, nused: (tile(i), 0)),
                      pl.BlockSpec((1, 1, D_MODEL), lambda i, nused: (_cond_of_tile(tile(i)), 0, 5))],
            out_specs=pl.BlockSpec((ROW_TILE, D_MODEL), lambda i, nused: (tile(i), 0)),
            scratch_shapes=[pltpu.VMEM((N_TILES * SLABS, LANES), F32),
                            pltpu.VMEM((MOE_STEP_ROWS * SLABS, LANES), F32)],
        ),
        out_shape=jax.ShapeDtypeStruct((T_ALL, D_MODEL), F32),
        compiler_params=_params(("arbitrary",)),
        name="moe_combine",
    )(n_used, src3, w3, ys, x1, mod_l)


def _routing_tables(top_idx, top_w):
    flat_e = top_idx.reshape(N_ASSIGN)
    experts = jnp.arange(N_EXPERTS, dtype=jnp.int32)
    counts = jnp.sum((flat_e[:, None] == experts[None, :]).astype(jnp.int32), axis=0)
    pad = (-counts) % MOE_ROWS
    spare = jnp.arange(MOE_ROWS, dtype=jnp.int32)
    pad_keys = jnp.where(spare[None, :] < pad[:, None], experts[:, None], N_EXPERTS).reshape(-1)
    keys = jnp.concatenate([flat_e, pad_keys])
    toks = jnp.concatenate([jnp.arange(N_ASSIGN, dtype=jnp.int32) // TOP_K,
                            jnp.full((N_PAD - N_ASSIGN,), T_ALL, jnp.int32)])
    wts = jnp.concatenate([top_w.reshape(N_ASSIGN), jnp.zeros((N_PAD - N_ASSIGN,), F32)])
    _, src, w_sorted = lax.sort((keys, toks, wts), num_keys=1, is_stable=True)
    padded = counts + pad
    blk_end = jnp.cumsum(padded) // MOE_ROWS
    nblk = padded // MOE_ROWS
    return (src.reshape(N_MOE_STEPS, 1, MOE_STEP_ROWS), w_sorted.reshape(N_MOE_STEPS, 1, MOE_STEP_ROWS),
            (blk_end - nblk).astype(jnp.int32), nblk.astype(jnp.int32), blk_end[-1:].astype(jnp.int32))


def kernel(x_prompt, x_sample, cache_attn_k, cache_attn_v, state_ret, c, c_ctx, norm_mix, norm_ffn, w_ada, b_ada,
           w_in, q_norm, k_norm, attn_sink, ret_decay, ret_norm, w_out, w_router, b_router, w_gate_up, b_gate_up,
           w_down, b_down):
    x = jnp.concatenate([x_prompt.reshape(T_CTX, D_MODEL), x_sample.reshape(T_LAT, D_MODEL)], axis=0)
    cond = jnp.zeros((COND_PAD, D_MODEL), F32).at[0].set(c_ctx).at[1:N_COND].set(c)
    mod = _modulation(cond, w_ada, b_ada)[:, :N_COND].reshape(DEPTH, N_COND, 1, 6 * D_MODEL)

    cache_k = cache_attn_k.reshape(DEC_BATCH, DEPTH, PAST_LEN, KV_WIDTH)
    cache_v = cache_attn_v.reshape(DEC_BATCH, DEPTH, PAST_LEN, KV_WIDTH)
    cosq, sinq, cosk, sin_k = _rope_tables()
    grp = jnp.arange(ATTN_WIDTH) // HEAD_DIM
    avgq = jnp.where(grp[:, None] == grp[None, :], 1.0 / HEAD_DIM, 0.0).astype(BF16)
    avgk = avgq[:KV_WIDTH, :KV_WIDTH]
    log_gamma = jax.nn.log_sigmoid(ret_decay.astype(F32))
    chunk_decay = jnp.exp(log_gamma * CHUNK)

    new_k, new_v, new_s = [], [], []
    for l in range(DEPTH):
        mod_l = mod[l]
        q, k, v, rq, rk, rv, sg = _input_projection(
            x, norm_mix[l].reshape(1, D_MODEL), mod_l, w_in[l].astype(BF16),
            jnp.tile(q_norm[l], N_Q_HEADS).reshape(1, ATTN_WIDTH), jnp.tile(k_norm[l], N_KV_HEADS).reshape(1, KV_WIDTH),
            avgq, avgk)
        new_k.append(k[:T_CTX].reshape(BATCH, SEQ, N_KV_HEADS, HEAD_DIM))
        new_v.append(v[:T_CTX].reshape(BATCH, SEQ, N_KV_HEADS, HEAD_DIM))
        sink = attn_sink[l].astype(F32)
        att_c = _context_attention(sink, q, k, v)
        att_l = _latent_attention(sink, q, k, v, cache_k, cache_v, l, cosq, sinq, cosk, sin_k)
        lg = log_gamma[l].reshape(2 * N_RET_HEADS)
        cd = chunk_decay[l].reshape(2 * N_RET_HEADS)
        gn = ret_norm[l].reshape(1, RET_WIDTH)
        ret_c, s_fin = _retention(lg, cd, rq, rk, rv, sg, gn, None, l, n_seq=BATCH, seq_len=SEQ, row0=0,
                                  write_state=True)
        (ret_l,) = _retention(lg, cd, rq, rk, rv, sg, gn, state_ret, l, n_seq=DEC_BATCH, seq_len=DEC_SEQ,
                              row0=T_CTX, write_state=False)
        new_s.append(s_fin)
        att = jnp.concatenate([att_c, att_l], axis=0)
        ret = jnp.concatenate([ret_c, ret_l], axis=0)
        wr = jnp.zeros((D_MODEL, LANES), F32).at[:, :N_EXPERTS].set(w_router[l])
        wrh = wr.astype(BF16)
        wrl = (wr - wrh.astype(F32)).astype(BF16)
        br = jnp.full((1, LANES), NEG_BIG, F32).at[0, :N_EXPERTS].set(b_router[l])
        x1, h2t, ti, tw = _output_projection(att, ret, x, w_out[l].astype(BF16), mod_l,
                                             norm_ffn[l].reshape(1, D_MODEL), wrh, wrl, br)
        src3, w3, blk0, nblk, n_used = _routing_tables(ti[:, :TOP_K], tw[:, :TOP_K])
        xs = _moe_dispatch(n_used, src3, h2t)
        ys = _moe_experts(blk0, nblk, n_used, xs, w_gate_up, b_gate_up, w_down, b_down, l)
        x = _moe_combine(n_used, src3, w3, ys, x1, mod_l)

    y_prompt = x[:T_CTX].reshape(BATCH, SEQ, D_MODEL)
    y_sample = x[T_CTX:].reshape(DEC_BATCH, DEC_SEQ, D_MODEL)
    return (y_prompt, y_sample, jnp.stack(new_k, axis=1), jnp.stack(new_v, axis=1), jnp.stack(new_s, axis=1))
```

```python
import functools

import jax
import jax.numpy as jnp
from jax import lax
from jax.experimental import pallas as pl
from jax.experimental.pallas import tpu as pltpu

F32 = jnp.float32
BF16 = jnp.bfloat16

D_MODEL = 1024
DEPTH = 4
BATCH, SEQ = 16, 256
DEC_BATCH, DEC_SEQ = 2, 1024
PAST_LEN = 512
GRID_W = 64
HEAD_DIM = 64
N_Q_HEADS = 8
N_KV_HEADS = 2
ATTN_WIDTH = N_Q_HEADS * HEAD_DIM
KV_WIDTH = N_KV_HEADS * HEAD_DIM
WINDOW = 128
BLOCK = 128
ROPE_BASE = 10000.0
N_RET_HEADS = 4
RET_DK = 128
RET_WIDTH = N_RET_HEADS * RET_DK
CHUNK = 128
IN_WIDTH = ATTN_WIDTH + 2 * KV_WIDTH + 4 * RET_WIDTH
N_EXPERTS = 32
TOP_K = 4
D_FF = D_MODEL
SWIGLU_LIMIT = 7.0
SWIGLU_ALPHA = 1.702
EPS = 1e-6

T_CTX = BATCH * SEQ
T_LAT = DEC_BATCH * DEC_SEQ
T_ALL = T_CTX + T_LAT
N_COND = 1 + DEC_BATCH
COND_PAD = 8
LANES = 128
ROW_TILE = 512
MOE_ROWS = 128
N_ASSIGN = T_ALL * TOP_K
N_MOE_BLOCKS = N_ASSIGN // MOE_ROWS + N_EXPERTS
N_PAD = N_MOE_BLOCKS * MOE_ROWS
MOE_STEP_BLOCKS = 4
MOE_STEP_ROWS = MOE_STEP_BLOCKS * MOE_ROWS
N_MOE_STEPS = N_MOE_BLOCKS // MOE_STEP_BLOCKS
SLABS = D_MODEL // LANES
GROUP = 8
N_TILES = T_ALL + 1
NEG_BIG = -1e30
VMEM_LIMIT = 48 * 1024 * 1024


def _cond_of_tile(i):
    return jnp.where(i < T_CTX // ROW_TILE, 0, 1 + (i - T_CTX // ROW_TILE) // (DEC_SEQ // ROW_TILE))


def _params(sem, vmem=VMEM_LIMIT):
    return pltpu.CompilerParams(dimension_semantics=sem, vmem_limit_bytes=vmem)


def _mod_kernel(c_ref, w_ref, b_ref, o_ref):
    c = c_ref[...]
    s = (c * jax.nn.sigmoid(c)).astype(BF16)
    o_ref[0] = jnp.dot(s, w_ref[0].astype(BF16), preferred_element_type=F32) + b_ref[0]


def _modulation(cond, w_ada, b_ada):
    n_col = 6 * D_MODEL // D_MODEL
    return pl.pallas_call(
        _mod_kernel,
        grid=(DEPTH, n_col),
        in_specs=[
            pl.BlockSpec((COND_PAD, D_MODEL), lambda l, j: (0, 0)),
            pl.BlockSpec((1, D_MODEL, D_MODEL), lambda l, j: (l, 0, j)),
            pl.BlockSpec((1, 1, D_MODEL), lambda l, j: (l, 0, j)),
        ],
        out_specs=pl.BlockSpec((1, COND_PAD, D_MODEL), lambda l, j: (l, 0, j)),
        out_shape=jax.ShapeDtypeStruct((DEPTH, COND_PAD, 6 * D_MODEL), F32),
        compiler_params=_params(("arbitrary", "arbitrary")),
        name="modulation",
    )(cond, w_ada, b_ada.reshape(DEPTH, 1, 6 * D_MODEL))


def _rms_rows(x, g):
    ms = jnp.mean(x * x, axis=-1, keepdims=True)
    return x * lax.rsqrt(ms + EPS) * g


def _group_rmsnorm(a, avg_ref, g):
    sq = a * a
    hi = sq.astype(BF16)
    lo = (sq - hi.astype(F32)).astype(BF16)
    avg = avg_ref[...]
    ms = jnp.dot(hi, avg, preferred_element_type=F32) + jnp.dot(lo, avg, preferred_element_type=F32)
    return a * lax.rsqrt(ms + EPS) * g


def _inproj_kernel(x_ref, g_ref, sh_ref, sc_ref, w_ref, qn_ref, kn_ref, avgq_ref, avgk_ref,
                   q_ref, k_ref, v_ref, rq_ref, rk_ref, rv_ref, sg_ref):
    h = _rms_rows(x_ref[...], g_ref[...]) * (1.0 + sc_ref[0]) + sh_ref[0]
    hb = h.astype(BF16)

    def proj(lo, width):
        return jnp.dot(hb, w_ref[:, lo:lo + width], preferred_element_type=F32)

    o = 0
    q_ref[...] = _group_rmsnorm(proj(o, ATTN_WIDTH), avgq_ref, qn_ref[...])
    o += ATTN_WIDTH
    k_ref[...] = _group_rmsnorm(proj(o, KV_WIDTH), avgk_ref, kn_ref[...])
    o += KV_WIDTH
    v_ref[...] = proj(o, KV_WIDTH)
    o += KV_WIDTH
    rq_ref[...] = proj(o, RET_WIDTH) * (RET_DK ** -0.5)
    o += RET_WIDTH
    rk_ref[...] = proj(o, RET_WIDTH)
    o += RET_WIDTH
    rv_ref[...] = proj(o, RET_WIDTH)
    o += RET_WIDTH
    rg = proj(o, RET_WIDTH)
    sg_ref[...] = rg * jax.nn.sigmoid(rg)


def _mod_spec(col):
    return pl.BlockSpec((1, 1, D_MODEL), lambda i, col=col: (_cond_of_tile(i), 0, col))


def _full(shape):
    return pl.BlockSpec(shape, lambda *_: (0,) * len(shape))


def _input_projection(x, norm_g, mod_l, w_in_bf, qn, kn, avgq, avgk):
    rows = lambda w: pl.BlockSpec((ROW_TILE, w), lambda i: (i, 0))
    widths = (ATTN_WIDTH, KV_WIDTH, KV_WIDTH, RET_WIDTH, RET_WIDTH, RET_WIDTH, RET_WIDTH)
    return pl.pallas_call(
        _inproj_kernel,
        grid=(T_ALL // ROW_TILE,),
        in_specs=[rows(D_MODEL), _full((1, D_MODEL)), _mod_spec(0), _mod_spec(1),
                  _full((D_MODEL, IN_WIDTH)), _full((1, ATTN_WIDTH)), _full((1, KV_WIDTH)),
                  _full((ATTN_WIDTH, ATTN_WIDTH)), _full((KV_WIDTH, KV_WIDTH))],
        out_specs=[rows(w) for w in widths],
        out_shape=[jax.ShapeDtypeStruct((T_ALL, w), F32) for w in widths],
        compiler_params=_params(("arbitrary",)),
        name="norm_inproj",
    )(x, norm_g, mod_l, mod_l, w_in_bf, qn, kn, avgq, avgk)


def _attend(q, kall, vall, valid, sink_ref, o_ref):
    scale = HEAD_DIM ** -0.5
    lane = lax.broadcasted_iota(jnp.int32, (1, LANES), 1)
    low = lane < HEAD_DIM
    k_at, v_at = [], []
    for g in range(N_KV_HEADS):
        keep = low if g == 0 else jnp.logical_not(low)
        kg = jnp.where(keep, kall, 0.0)
        vg = jnp.where(keep, vall, 0.0)
        kr = pltpu.roll(kg, HEAD_DIM, 1)
        vr = pltpu.roll(vg, HEAD_DIM, 1)
        pair_k = (kg, kr) if g == 0 else (kr, kg)
        pair_v = (vg, vr) if g == 0 else (vr, vg)
        k_at.append([t.astype(BF16) for t in pair_k])
        v_at.append([t.astype(BF16) for t in pair_v])
    for j in range(ATTN_WIDTH // LANES):
        qj = q[:, j * LANES:(j + 1) * LANES].astype(BF16)
        acc = None
        for off in range(2):
            h = 2 * j + off
            g = h // (N_Q_HEADS // N_KV_HEADS)
            s = lax.dot_general(qj, k_at[g][off], (((1,), (1,)), ((), ())), preferred_element_type=F32) * scale
            if valid is not None:
                s = jnp.where(valid, s, NEG_BIG)
            sink = sink_ref[h]
            m = jnp.maximum(jnp.max(s, axis=-1, keepdims=True), sink)
            e = jnp.exp(s - m)
            den = jnp.sum(e, axis=-1, keepdims=True) + jnp.exp(sink - m)
            o = jnp.dot(e.astype(BF16), v_at[g][off], preferred_element_type=F32) / den
            acc = o if acc is None else acc + o
        o_ref[:, j * LANES:(j + 1) * LANES] = acc


def _ctx_attn_kernel(sink_ref, q_ref, k_ref, v_ref, o_ref):
    _attend(q_ref[...], k_ref[...], v_ref[...], None, sink_ref, o_ref)


def _context_attention(sink, q, k, v):
    return pl.pallas_call(
        _ctx_attn_kernel,
        grid_spec=pltpu.PrefetchScalarGridSpec(
            num_scalar_prefetch=1,
            grid=(BATCH,),
            in_specs=[pl.BlockSpec((SEQ, ATTN_WIDTH), lambda b, s: (b, 0)),
                      pl.BlockSpec((SEQ, KV_WIDTH), lambda b, s: (b, 0)),
                      pl.BlockSpec((SEQ, KV_WIDTH), lambda b, s: (b, 0))],
            out_specs=pl.BlockSpec((SEQ, ATTN_WIDTH), lambda b, s: (b, 0)),
        ),
        out_shape=jax.ShapeDtypeStruct((T_CTX, ATTN_WIDTH), F32),
        compiler_params=_params(("arbitrary",)),
        name="context_attention",
    )(sink, q, k, v)


def _rope_block(x, cos, sin_signed):
    lane = lax.broadcasted_iota(jnp.int32, (1, LANES), 1)
    first = (lane % (HEAD_DIM // 2)) < (HEAD_DIM // 4)
    swapped = jnp.where(first, pltpu.roll(x, LANES - HEAD_DIM // 4, 1), pltpu.roll(x, HEAD_DIM // 4, 1))
    return x * cos + swapped * sin_signed


LOCAL_KEYS = 3 * BLOCK


def _lat_attn_kernel(sink_ref, q_ref, k_ref, v_ref, ck_ref, cv_ref, cosq_ref, sinq_ref, cosk_ref, sin_k_ref,
                     o_ref):
    n = pl.program_id(1)
    start = pl.multiple_of(jnp.clip((n - 1) * BLOCK, 0, DEC_SEQ - LOCAL_KEYS), BLOCK)
    q = q_ref[...]
    q = jnp.concatenate(
        [_rope_block(q[:, j * LANES:(j + 1) * LANES], cosq_ref[:, j * LANES:(j + 1) * LANES],
                     sinq_ref[:, j * LANES:(j + 1) * LANES]) for j in range(ATTN_WIDTH // LANES)], axis=1)
    kw = _rope_block(k_ref[pl.ds(start, LOCAL_KEYS), :], cosk_ref[pl.ds(start, LOCAL_KEYS), :],
                     sin_k_ref[pl.ds(start, LOCAL_KEYS), :])
    vw = v_ref[pl.ds(start, LOCAL_KEYS), :]
    kall = jnp.concatenate([kw, ck_ref[0, 0]], axis=0)
    vall = jnp.concatenate([vw, cv_ref[0, 0]], axis=0)
    qpos = n * BLOCK + lax.broadcasted_iota(jnp.int32, (BLOCK, LOCAL_KEYS + PAST_LEN), 0)
    col = lax.broadcasted_iota(jnp.int32, (BLOCK, LOCAL_KEYS + PAST_LEN), 1)
    valid = jnp.logical_or(col >= LOCAL_KEYS, jnp.abs(qpos - (start + col)) <= WINDOW)
    _attend(q, kall, vall, valid, sink_ref, o_ref)


def _latent_attention(sink, q, k, v, cache_k, cache_v, layer, cosq, sinq, cosk, sin_k):
    nb = DEC_SEQ // BLOCK
    ctx_block0 = T_CTX // BLOCK
    ctx_seq0 = T_CTX // DEC_SEQ
    cache_spec = pl.BlockSpec((1, 1, PAST_LEN, KV_WIDTH), lambda b, n, s: (b, layer, 0, 0))
    return pl.pallas_call(
        _lat_attn_kernel,
        grid_spec=pltpu.PrefetchScalarGridSpec(
            num_scalar_prefetch=1,
            grid=(DEC_BATCH, nb),
            in_specs=[pl.BlockSpec((BLOCK, ATTN_WIDTH), lambda b, n, s: (ctx_block0 + b * nb + n, 0)),
                      pl.BlockSpec((DEC_SEQ, KV_WIDTH), lambda b, n, s: (ctx_seq0 + b, 0)),
                      pl.BlockSpec((DEC_SEQ, KV_WIDTH), lambda b, n, s: (ctx_seq0 + b, 0)),
                      cache_spec, cache_spec,
                      pl.BlockSpec((BLOCK, ATTN_WIDTH), lambda b, n, s: (n, 0)),
                      pl.BlockSpec((BLOCK, ATTN_WIDTH), lambda b, n, s: (n, 0)),
                      pl.BlockSpec((DEC_SEQ, KV_WIDTH), lambda b, n, s: (0, 0)),
                      pl.BlockSpec((DEC_SEQ, KV_WIDTH), lambda b, n, s: (0, 0))],
            out_specs=pl.BlockSpec((BLOCK, ATTN_WIDTH), lambda b, n, s: (b * nb + n, 0)),
        ),
        out_shape=jax.ShapeDtypeStruct((T_LAT, ATTN_WIDTH), F32),
        compiler_params=_params(("arbitrary", "arbitrary")),
        name="latent_attention",
    )(sink, q, k, v, cache_k, cache_v, cosq, sinq, cosk, sin_k)


def _rope_tables():
    t = jnp.arange(DEC_SEQ)
    nf = HEAD_DIM // 4
    inv = ROPE_BASE ** (-jnp.arange(nf, dtype=F32) / nf)

    def half(coord):
        ang = coord.astype(F32)[:, None] * inv[None, :]
        c, s = jnp.cos(ang), jnp.sin(ang)
        return jnp.concatenate([c, c], axis=1), jnp.concatenate([-s, s], axis=1)

    cr, sr = half(t // GRID_W)
    cc, sc = half(t % GRID_W)
    cos = jnp.concatenate([cr, cc], axis=1)
    sin = jnp.concatenate([sr, sc], axis=1)
    return (jnp.tile(cos, (1, N_Q_HEADS)), jnp.tile(sin, (1, N_Q_HEADS)),
            jnp.tile(cos, (1, N_KV_HEADS)), jnp.tile(sin, (1, N_KV_HEADS)))


def _ret_kernel(lg_ref, cd_ref, q_ref, k_ref, v_ref, sg_ref, gn_ref, *rest, n_chunks, has_s0, write_state):
    rest = list(rest)
    s0_ref = rest.pop(0) if has_s0 else None
    o_ref = rest.pop(0)
    sf_ref = rest.pop(0) if write_state else None
    acc_ref = rest.pop(0)
    row = lax.broadcasted_iota(jnp.int32, (CHUNK, CHUNK), 0).astype(F32)
    col = lax.broadcasted_iota(jnp.int32, (CHUNK, CHUNK), 1).astype(F32)
    rel = row - col
    pos = lax.broadcasted_iota(jnp.int32, (CHUNK, 1), 0).astype(F32)

    def run(direction, h):
        cols = slice(h * RET_DK, (h + 1) * RET_DK)
        gn = gn_ref[:, cols]
        lg = lg_ref[direction * N_RET_HEADS + h]
        cd = cd_ref[direction * N_RET_HEADS + h]
        if direction == 0:
            intra = jnp.where(rel >= 0, jnp.exp(lg * rel), 0.0)
            q_dec = jnp.exp(lg * (pos + 1.0))
            k_dec = jnp.exp(lg * (CHUNK - 1.0 - pos))
            order = range(n_chunks)
        else:
            intra = jnp.where(rel <= 0, jnp.exp(-lg * rel), 0.0)
            q_dec = jnp.exp(lg * (CHUNK - pos))
            k_dec = jnp.exp(lg * pos)
            order = range(n_chunks - 1, -1, -1)
        state = s0_ref[0, 0, direction, h] if has_s0 else jnp.zeros((RET_DK, RET_DK), F32)
        for c in order:
            rows = slice(c * CHUNK, (c + 1) * CHUNK)
            qc, kc, vc = q_ref[rows, cols], k_ref[rows, cols], v_ref[rows, cols]
            qb, kb, vb = qc.astype(BF16), kc.astype(BF16), vc.astype(BF16)
            scores = lax.dot_general(qb, kb, (((1,), (1,)), ((), ())), preferred_element_type=F32) * intra
            o = (jnp.dot(scores.astype(BF16), vb, preferred_element_type=F32)
                 + jnp.dot(qb, state.astype(BF16), preferred_element_type=F32) * q_dec)
            kd_t = (kc * k_dec).T.astype(BF16)
            state = state * cd + jnp.dot(kd_t, vb, preferred_element_type=F32)
            if direction == 0:
                acc_ref[rows, cols] = o
            else:
                tot = acc_ref[rows, cols] + o
                o_ref[rows, cols] = _rms_rows(tot, gn) * sg_ref[rows, cols]
        if write_state:
            sf_ref[0, direction, h] = state

    for direction in range(2):
        for h in range(N_RET_HEADS):
            run(direction, h)


def _retention(lg, cd, rq, rk, rv, sg, gn, s0, layer, *, n_seq, seq_len, row0, write_state):
    blk0 = row0 // seq_len
    rows = pl.BlockSpec((seq_len, RET_WIDTH), lambda b, *_: (blk0 + b, 0))
    in_specs = [rows, rows, rows, rows, pl.BlockSpec((1, RET_WIDTH), lambda b, *_: (0, 0))]
    args = [rq, rk, rv, sg, gn]
    if s0 is not None:
        in_specs.append(pl.BlockSpec((1, 1, 2, N_RET_HEADS, RET_DK, RET_DK), lambda b, *_: (b, layer, 0, 0, 0, 0)))
        args.append(s0)
    out_specs = [pl.BlockSpec((seq_len, RET_WIDTH), lambda b, *_: (b, 0))]
    out_shape = [jax.ShapeDtypeStruct((n_seq * seq_len, RET_WIDTH), F32)]
    if write_state:
        out_specs.append(pl.BlockSpec((1, 2, N_RET_HEADS, RET_DK, RET_DK), lambda b, *_: (b, 0, 0, 0, 0)))
        out_shape.append(jax.ShapeDtypeStruct((n_seq, 2, N_RET_HEADS, RET_DK, RET_DK), F32))
    kern = functools.partial(_ret_kernel, n_chunks=seq_len // CHUNK, has_s0=s0 is not None,
                             write_state=write_state)
    return pl.pallas_call(
        kern,
        grid_spec=pltpu.PrefetchScalarGridSpec(
            num_scalar_prefetch=2,
            grid=(n_seq,),
            in_specs=in_specs,
            out_specs=out_specs,
            scratch_shapes=[pltpu.VMEM((seq_len, RET_WIDTH), F32)],
        ),
        out_shape=out_shape,
        compiler_params=_params(("arbitrary",)),
        name="retention_ctx" if write_state else "retention_lat",
    )(lg, cd, *args)


def _outproj_kernel(attc_ref, attl_ref, retc_ref, retl_ref, x_ref, wo_ref, g1_ref, sh2_ref, sc2_ref, nf_ref,
                    wrh_ref, wrl_ref, br_ref, x1_ref, h2t_ref, ti_ref, tw_ref):
    is_ctx = pl.program_id(0) < T_CTX // ROW_TILE
    att = jnp.where(is_ctx, attc_ref[...], attl_ref[...])
    ret = jnp.where(is_ctx, retc_ref[...], retl_ref[...])
    y = (jnp.dot(att.astype(BF16), wo_ref[0:ATTN_WIDTH, :], preferred_element_type=F32)
         + jnp.dot(ret.astype(BF16), wo_ref[ATTN_WIDTH:, :], preferred_element_type=F32))
    x1 = x_ref[...] + g1_ref[0] * y
    x1_ref[...] = x1
    h2 = _rms_rows(x1, nf_ref[...]) * (1.0 + sc2_ref[0]) + sh2_ref[0]
    for s in range(SLABS):
        h2t_ref[pl.ds(s, ROW_TILE, stride=SLABS), :] = h2[:, s * LANES:(s + 1) * LANES]
    hh = h2.astype(BF16)
    hl = (h2 - hh.astype(F32)).astype(BF16)
    wrh = wrh_ref[...]
    logits = (jnp.dot(hh, wrh, preferred_element_type=F32) + jnp.dot(hl, wrh, preferred_element_type=F32)
              + jnp.dot(hh, wrl_ref[...], preferred_element_type=F32) + br_ref[...])
    lane = lax.broadcasted_iota(jnp.int32, logits.shape, 1)
    vals, idxs = [], []
    cur = logits
    for _ in range(TOP_K):
        m = jnp.max(cur, axis=-1, keepdims=True)
        idx = jnp.min(jnp.where(cur == m, lane, LANES), axis=-1, keepdims=True)
        vals.append(m)
        idxs.append(idx)
        cur = jnp.where(lane == idx, -jnp.inf, cur)
    es = [jnp.exp(v - vals[0]) for v in vals]
    den = es[0] + es[1] + es[2] + es[3]
    ti = jnp.zeros(logits.shape, jnp.int32)
    tw = jnp.zeros(logits.shape, F32)
    for k in range(TOP_K):
        ti = jnp.where(lane == k, idxs[k], ti)
        tw = jnp.where(lane == k, es[k] / den, tw)
    ti_ref[...] = ti
    tw_ref[...] = tw


def _output_projection(att_c, att_l, ret_c, ret_l, x, w_out_bf, mod_l, nf, wrh, wrl, br):
    n_ctx = T_CTX // ROW_TILE
    rows = lambda w: pl.BlockSpec((ROW_TILE, w), lambda i: (i, 0))
    ctx_rows = lambda w: pl.BlockSpec((ROW_TILE, w), lambda i: (jnp.minimum(i, n_ctx - 1), 0))
    lat_rows = lambda w: pl.BlockSpec((ROW_TILE, w), lambda i: (jnp.maximum(i - n_ctx, 0), 0))
    return pl.pallas_call(
        _outproj_kernel,
        grid=(T_ALL // ROW_TILE,),
        in_specs=[ctx_rows(ATTN_WIDTH), lat_rows(ATTN_WIDTH), ctx_rows(RET_WIDTH), lat_rows(RET_WIDTH),
                  rows(D_MODEL), _full((D_MODEL, D_MODEL)),
                  _mod_spec(2), _mod_spec(3), _mod_spec(4), _full((1, D_MODEL)),
                  _full((D_MODEL, LANES)), _full((D_MODEL, LANES)), _full((1, LANES))],
        out_specs=[rows(D_MODEL), pl.BlockSpec((ROW_TILE * SLABS, LANES), lambda i: (i, 0)), rows(LANES), rows(LANES)],
        out_shape=[jax.ShapeDtypeStruct((T_ALL, D_MODEL), F32), jax.ShapeDtypeStruct((T_ALL * SLABS, LANES), F32),
                   jax.ShapeDtypeStruct((T_ALL, LANES), jnp.int32), jax.ShapeDtypeStruct((T_ALL, LANES), F32)],
        compiler_params=_params(("arbitrary",)),
        name="outproj_router",
    )(att_c, att_l, ret_c, ret_l, x, w_out_bf, mod_l, mod_l, mod_l, nf, wrh, wrl, br)


def _token_tile(ref, t):
    return ref.at[pl.ds(pl.multiple_of(t * SLABS, SLABS), SLABS), :]


def _step_is_used(i, nused_ref):
    return i * MOE_STEP_BLOCKS < nused_ref[0]


def _dispatch_kernel(nused_ref, src_ref, h2t_hbm, o_ref, hres, xg, sem):
    i = pl.program_id(0)

    @pl.when(i == 0)
    def _():
        cp = pltpu.make_async_copy(h2t_hbm, hres.at[pl.ds(0, T_ALL * SLABS), :], sem.at[0])
        cp.start()
        _token_tile(hres, T_ALL)[...] = jnp.zeros((SLABS, LANES), F32)
        cp.wait()

    @pl.when(_step_is_used(i, nused_ref))
    def _():
        def body(r, carry):
            _token_tile(xg, r)[...] = _token_tile(hres, src_ref[0, 0, r])[...]
            return carry

        lax.fori_loop(0, MOE_STEP_ROWS, body, 0, unroll=8)
        for s in range(SLABS):
            o_ref[:, s * LANES:(s + 1) * LANES] = xg[pl.ds(s, MOE_STEP_ROWS, stride=SLABS), :].astype(BF16)

    @pl.when(jnp.logical_not(_step_is_used(i, nused_ref)))
    def _():
        o_ref[...] = jnp.zeros(o_ref.shape, BF16)


def _smem_rows(width):
    return pl.BlockSpec((1, 1, width), lambda i, nused: (jnp.minimum(i, N_MOE_STEPS - 1), 0, 0),
                        memory_space=pltpu.SMEM)


def _moe_dispatch(n_used, src3, h2t):
    return pl.pallas_call(
        _dispatch_kernel,
        grid_spec=pltpu.PrefetchScalarGridSpec(
            num_scalar_prefetch=1,
            grid=(N_MOE_STEPS,),
            in_specs=[_smem_rows(MOE_STEP_ROWS), pl.BlockSpec(memory_space=pl.ANY)],
            out_specs=pl.BlockSpec((MOE_STEP_ROWS, D_MODEL), lambda i, nused: (i, 0)),
            scratch_shapes=[pltpu.VMEM((N_TILES * SLABS, LANES), F32), pltpu.VMEM((MOE_STEP_ROWS * SLABS, LANES), F32),
                            pltpu.SemaphoreType.DMA((1,))],
        ),
        out_shape=jax.ShapeDtypeStruct((N_PAD, D_MODEL), BF16),
        compiler_params=_params(("arbitrary",)),
        name="moe_dispatch",
    )(n_used, src3, h2t)


X_SLOTS = 8
X_AHEAD = 6
Y_SLOTS = 4


def _block_rows(ref, g):
    return ref.at[pl.ds(pl.multiple_of(g * MOE_ROWS, MOE_ROWS), MOE_ROWS), :]


def _experts_kernel(blk0_ref, nblk_ref, nused_ref, xs_hbm, wgu_ref, bgu_ref, wdn_ref, bdn_ref, ys_hbm,
                    xbuf, ybuf, wgu_bf, wdn_bf, xsem, ysem):
    e = pl.program_id(0)
    b0, nb, nused = blk0_ref[e], nblk_ref[e], nused_ref[0]

    def x_copy(g):
        slot = g % X_SLOTS
        return pltpu.make_async_copy(_block_rows(xs_hbm, g), xbuf.at[slot], xsem.at[slot])

    def y_copy(g):
        slot = g % Y_SLOTS
        return pltpu.make_async_copy(ybuf.at[slot], _block_rows(ys_hbm, g), ysem.at[slot])

    @pl.when(e == 0)
    def _():
        for k in range(X_AHEAD):
            @pl.when(k < nused)
            def _():
                x_copy(k).start()

    @pl.when(nb > 0)
    def _():
        wgu_bf[...] = wgu_ref[...].astype(BF16)
        wdn_bf[...] = wdn_ref[...].astype(BF16)

        def body(j, carry):
            g = b0 + j
            x_copy(g).wait()

            @pl.when(g + X_AHEAD < nused)
            def _():
                x_copy(g + X_AHEAD).start()

            gu = jnp.dot(xbuf[g % X_SLOTS], wgu_bf[...], preferred_element_type=F32) + bgu_ref[...]
            x_glu = jnp.minimum(gu[:, :D_FF], SWIGLU_LIMIT)
            x_lin = jnp.clip(gu[:, D_FF:], -SWIGLU_LIMIT, SWIGLU_LIMIT)
            act = x_glu * jax.nn.sigmoid(SWIGLU_ALPHA * x_glu) * (x_lin + 1.0)
            out = jnp.dot(act.astype(BF16), wdn_bf[...], preferred_element_type=F32) + bdn_ref[...]

            @pl.when(g >= Y_SLOTS)
            def _():
                y_copy(g - Y_SLOTS).wait()

            ybuf[g % Y_SLOTS] = out
            y_copy(g).start()
            return carry

        lax.fori_loop(0, nb, body, 0)

    @pl.when(e == N_EXPERTS - 1)
    def _():
        for k in range(1, Y_SLOTS + 1):
            @pl.when(nused >= k)
            def _():
                y_copy(nused - k).wait()

        ybuf[0] = jnp.zeros((MOE_ROWS, D_MODEL), F32)

        def fill(g, carry):
            cp = pltpu.make_async_copy(ybuf.at[0], _block_rows(ys_hbm, g), ysem.at[0])
            cp.start()
            cp.wait()
            return carry

        lax.fori_loop(nused, N_MOE_BLOCKS, fill, 0)


def _moe_experts(blk0, nblk, n_used, xs, w_gu, b_gu, w_dn, b_dn, layer):
    wspec = lambda rows, cols: pl.BlockSpec((None, None, rows, cols), lambda e, *_: (layer, e, 0, 0))
    return pl.pallas_call(
        _experts_kernel,
        grid_spec=pltpu.PrefetchScalarGridSpec(
            num_scalar_prefetch=3,
            grid=(N_EXPERTS,),
            in_specs=[pl.BlockSpec(memory_space=pl.ANY),
                      wspec(D_MODEL, 2 * D_FF), wspec(1, 2 * D_FF), wspec(D_FF, D_MODEL), wspec(1, D_MODEL)],
            out_specs=pl.BlockSpec(memory_space=pl.ANY),
            scratch_shapes=[pltpu.VMEM((X_SLOTS, MOE_ROWS, D_MODEL), BF16), pltpu.VMEM((Y_SLOTS, MOE_ROWS, D_MODEL), F32),
                            pltpu.VMEM((D_MODEL, 2 * D_FF), BF16), pltpu.VMEM((D_FF, D_MODEL), BF16),
                            pltpu.SemaphoreType.DMA((X_SLOTS,)), pltpu.SemaphoreType.DMA((Y_SLOTS,))],
        ),
        out_shape=jax.ShapeDtypeStruct((N_PAD, D_MODEL), F32),
        compiler_params=_params(("arbitrary",)),
        name="moe_experts",
    )(blk0, nblk, n_used, xs, w_gu, b_gu.reshape(DEPTH, N_EXPERTS, 1, 2 * D_FF),
      w_dn, b_dn.reshape(DEPTH, N_EXPERTS, 1, D_MODEL))


N_ROW_TILES = T_ALL // ROW_TILE
ZERO_ROWS = ROW_TILE * SLABS


def _combine_kernel(nused_ref, src_ref, w_ref, ys_ref, x1_ref, g2_ref, o_ref, yres, ot):
    i = pl.program_id(0)

    @pl.when(i == 0)
    def _():
        def zero(j, carry):
            yres[pl.ds(pl.multiple_of(j * ZERO_ROWS, ZERO_ROWS), ZERO_ROWS), :] = jnp.zeros((ZERO_ROWS, LANES), F32)
            return carry
        lax.fori_loop(0, N_ROW_TILES, zero, 0)
        _token_tile(yres, T_ALL)[...] = jnp.zeros((SLABS, LANES), F32)

    @pl.when(jnp.logical_and(i < N_MOE_STEPS, _step_is_used(i, nused_ref)))
    def _():
        for s in range(SLABS):
            ot[pl.ds(s, MOE_STEP_ROWS, stride=SLABS), :] = ys_ref[:, s * LANES:(s + 1) * LANES]

        def group(g, carry):
            rows = [g * GROUP + j for j in range(GROUP)]
            toks = [src_ref[0, 0, r] for r in rows]
            ws = [w_ref[0, 0, r] for r in rows]
            new = [_token_tile(yres, t)[...] + w * _token_tile(ot, r)[...] for t, w, r in zip(toks, ws, rows)]
            for t, v in zip(toks, new):
                _token_tile(yres, t)[...] = v
            return carry

        lax.fori_loop(0, MOE_STEP_ROWS // GROUP, group, 0)

    @pl.when(i >= N_MOE_STEPS)
    def _():
        base = (i - N_MOE_STEPS) * ZERO_ROWS
        for s in range(SLABS):
            cols = slice(s * LANES, (s + 1) * LANES)
            y = yres[pl.ds(base + s, ROW_TILE, stride=SLABS), :]
            o_ref[:, cols] = x1_ref[:, cols] + g2_ref[0][:, cols] * y


def _moe_combine(n_used, src3, w3, ys, x1, mod_l):
    tile = lambda i: jnp.maximum(i - N_MOE_STEPS, 0)
    last_used = lambda nused: jnp.maximum(nused[0] - 1, 0) // MOE_STEP_BLOCKS
    return pl.pallas_call(
        _combine_kernel,
        grid_spec=pltpu.PrefetchScalarGridSpec(
            num_scalar_prefetch=1,
            grid=(N_MOE_STEPS + N_ROW_TILES,),
            in_specs=[_smem_rows(MOE_STEP_ROWS), _smem_rows(MOE_STEP_ROWS),
                      pl.BlockSpec((MOE_STEP_ROWS, D_MODEL), lambda i, nused: (jnp.minimum(i, last_used(nused)), 0)),
                      pl.BlockSpec((ROW_TILE, D_MODEL), lambda i, nused: (tile(i), 0)),
                      pl.BlockSpec((1, 1, D_MODEL), lambda i, nused: (_cond_of_tile(tile(i)), 0, 5))],
            out_specs=pl.BlockSpec((ROW_TILE, D_MODEL), lambda i, nused: (tile(i), 0)),
            scratch_shapes=[pltpu.VMEM((N_TILES * SLABS, LANES), F32),
                            pltpu.VMEM((MOE_STEP_ROWS * SLABS, LANES), F32)],
        ),
        out_shape=jax.ShapeDtypeStruct((T_ALL, D_MODEL), F32),
        compiler_params=_params(("arbitrary",)),
        name="moe_combine",
    )(n_used, src3, w3, ys, x1, mod_l)


def _routing_tables(top_idx, top_w):
    flat_e = top_idx.reshape(N_ASSIGN)
    experts = jnp.arange(N_EXPERTS, dtype=jnp.int32)
    counts = jnp.sum((flat_e[:, None] == experts[None, :]).astype(jnp.int32), axis=0)
    pad = (-counts) % MOE_ROWS
    spare = jnp.arange(MOE_ROWS, dtype=jnp.int32)
    pad_keys = jnp.where(spare[None, :] < pad[:, None], experts[:, None], N_EXPERTS).reshape(-1)
    keys = jnp.concatenate([flat_e, pad_keys])
    toks = jnp.concatenate([jnp.arange(N_ASSIGN, dtype=jnp.int32) // TOP_K,
                            jnp.full((N_PAD - N_ASSIGN,), T_ALL, jnp.int32)])
    wts = jnp.concatenate([top_w.reshape(N_ASSIGN), jnp.zeros((N_PAD - N_ASSIGN,), F32)])
    _, src, w_sorted = lax.sort((keys, toks, wts), num_keys=1, is_stable=True)
    padded = counts + pad
    blk_end = jnp.cumsum(padded) // MOE_ROWS
    nblk = padded // MOE_ROWS
    return (src.reshape(N_MOE_STEPS, 1, MOE_STEP_ROWS), w_sorted.reshape(N_MOE_STEPS, 1, MOE_STEP_ROWS),
            (blk_end - nblk).astype(jnp.int32), nblk.astype(jnp.int32), blk_end[-1:].astype(jnp.int32))


def kernel(x_prompt, x_sample, cache_attn_k, cache_attn_v, state_ret, c, c_ctx, norm_mix, norm_ffn, w_ada, b_ada,
           w_in, q_norm, k_norm, attn_sink, ret_decay, ret_norm, w_out, w_router, b_router, w_gate_up, b_gate_up,
           w_down, b_down):
    x = jnp.concatenate([x_prompt.reshape(T_CTX, D_MODEL), x_sample.reshape(T_LAT, D_MODEL)], axis=0)
    cond = jnp.zeros((COND_PAD, D_MODEL), F32).at[0].set(c_ctx).at[1:N_COND].set(c)
    mod = _modulation(cond, w_ada, b_ada)[:, :N_COND].reshape(DEPTH, N_COND, 1, 6 * D_MODEL)

    cache_k = cache_attn_k.reshape(DEC_BATCH, DEPTH, PAST_LEN, KV_WIDTH)
    cache_v = cache_attn_v.reshape(DEC_BATCH, DEPTH, PAST_LEN, KV_WIDTH)
    cosq, sinq, cosk, sin_k = _rope_tables()
    grp = jnp.arange(ATTN_WIDTH) // HEAD_DIM
    avgq = jnp.where(grp[:, None] == grp[None, :], 1.0 / HEAD_DIM, 0.0).astype(BF16)
    avgk = avgq[:KV_WIDTH, :KV_WIDTH]
    log_gamma = jax.nn.log_sigmoid(ret_decay.astype(F32))
    chunk_decay = jnp.exp(log_gamma * CHUNK)

    new_k, new_v, new_s = [], [], []
    for l in range(DEPTH):
        mod_l = mod[l]
        q, k, v, rq, rk, rv, sg = _input_projection(
            x, norm_mix[l].reshape(1, D_MODEL), mod_l, w_in[l].astype(BF16),
            jnp.tile(q_norm[l], N_Q_HEADS).reshape(1, ATTN_WIDTH), jnp.tile(k_norm[l], N_KV_HEADS).reshape(1, KV_WIDTH),
            avgq, avgk)
        new_k.append(k[:T_CTX].reshape(BATCH, SEQ, N_KV_HEADS, HEAD_DIM))
        new_v.append(v[:T_CTX].reshape(BATCH, SEQ, N_KV_HEADS, HEAD_DIM))
        sink = attn_sink[l].astype(F32)
        att_c = _context_attention(sink, q, k, v)
        att_l = _latent_attention(sink, q, k, v, cache_k, cache_v, l, cosq, sinq, cosk, sin_k)
        lg = log_gamma[l].reshape(2 * N_RET_HEADS)
        cd = chunk_decay[l].reshape(2 * N_RET_HEADS)
        gn = ret_norm[l].reshape(1, RET_WIDTH)
        ret_c, s_fin = _retention(lg, cd, rq, rk, rv, sg, gn, None, l, n_seq=BATCH, seq_len=SEQ, row0=0,
                                  write_state=True)
        (ret_l,) = _retention(lg, cd, rq, rk, rv, sg, gn, state_ret, l, n_seq=DEC_BATCH, seq_len=DEC_SEQ,
                              row0=T_CTX, write_state=False)
        new_s.append(s_fin)
        wr = jnp.zeros((D_MODEL, LANES), F32).at[:, :N_EXPERTS].set(w_router[l])
        wrh = wr.astype(BF16)
        wrl = (wr - wrh.astype(F32)).astype(BF16)
        br = jnp.full((1, LANES), NEG_BIG, F32).at[0, :N_EXPERTS].set(b_router[l])
        x1, h2t, ti, tw = _output_projection(att_c, att_l, ret_c, ret_l, x, w_out[l].astype(BF16), mod_l,
                                             norm_ffn[l].reshape(1, D_MODEL), wrh, wrl, br)
        src3, w3, blk0, nblk, n_used = _routing_tables(ti[:, :TOP_K], tw[:, :TOP_K])
        xs = _moe_dispatch(n_used, src3, h2t)
        ys = _moe_experts(blk0, nblk, n_used, xs, w_gate_up, b_gate_up, w_down, b_down, l)
        x = _moe_combine(n_used, src3, w3, ys, x1, mod_l)

    y_prompt = x[:T_CTX].reshape(BATCH, SEQ, D_MODEL)
    y_sample = x[T_CTX:].reshape(DEC_BATCH, DEC_SEQ, D_MODEL)
    return (y_prompt, y_sample, jnp.stack(new_k, axis=1), jnp.stack(new_v, axis=1), jnp.stack(new_s, axis=1))
```

```python
import functools

import jax
import jax.numpy as jnp
from jax import lax
from jax.experimental import pallas as pl
from jax.experimental.pallas import tpu as pltpu

F32 = jnp.float32
BF16 = jnp.bfloat16

D_MODEL = 1024
DEPTH = 4
BATCH, SEQ = 16, 256
DEC_BATCH, DEC_SEQ = 2, 1024
PAST_LEN = 512
GRID_W = 64
HEAD_DIM = 64
N_Q_HEADS = 8
N_KV_HEADS = 2
ATTN_WIDTH = N_Q_HEADS * HEAD_DIM
KV_WIDTH = N_KV_HEADS * HEAD_DIM
WINDOW = 128
BLOCK = 128
ROPE_BASE = 10000.0
N_RET_HEADS = 4
RET_DK = 128
RET_WIDTH = N_RET_HEADS * RET_DK
CHUNK = 128
IN_WIDTH = ATTN_WIDTH + 2 * KV_WIDTH + 4 * RET_WIDTH
N_EXPERTS = 32
TOP_K = 4
D_FF = D_MODEL
SWIGLU_LIMIT = 7.0
SWIGLU_ALPHA = 1.702
EPS = 1e-6

T_CTX = BATCH * SEQ
T_LAT = DEC_BATCH * DEC_SEQ
T_ALL = T_CTX + T_LAT
N_COND = 1 + DEC_BATCH
COND_PAD = 8
LANES = 128
ROW_TILE = 512
MOE_ROWS = 128
N_ASSIGN = T_ALL * TOP_K
N_MOE_BLOCKS = N_ASSIGN // MOE_ROWS + N_EXPERTS
N_PAD = N_MOE_BLOCKS * MOE_ROWS
MOE_STEP_BLOCKS = 4
MOE_STEP_ROWS = MOE_STEP_BLOCKS * MOE_ROWS
N_MOE_STEPS = N_MOE_BLOCKS // MOE_STEP_BLOCKS
SLABS = D_MODEL // LANES
GROUP = 8
N_TILES = T_ALL + 1
NEG_BIG = -1e30
VMEM_LIMIT = 48 * 1024 * 1024
EXPERT_VMEM_LIMIT = 56 * 1024 * 1024


def _cond_of_tile(i):
    return jnp.where(i < T_CTX // ROW_TILE, 0, 1 + (i - T_CTX // ROW_TILE) // (DEC_SEQ // ROW_TILE))


def _params(sem, vmem=VMEM_LIMIT):
    return pltpu.CompilerParams(dimension_semantics=sem, vmem_limit_bytes=vmem)


def _mod_kernel(c_ref, w_ref, b_ref, o_ref):
    c = c_ref[...]
    s = (c * jax.nn.sigmoid(c)).astype(BF16)
    o_ref[0] = jnp.dot(s, w_ref[0].astype(BF16), preferred_element_type=F32) + b_ref[0]


def _modulation(cond, w_ada, b_ada):
    n_col = 6 * D_MODEL // D_MODEL
    return pl.pallas_call(
        _mod_kernel,
        grid=(DEPTH, n_col),
        in_specs=[
            pl.BlockSpec((COND_PAD, D_MODEL), lambda l, j: (0, 0)),
            pl.BlockSpec((1, D_MODEL, D_MODEL), lambda l, j: (l, 0, j)),
            pl.BlockSpec((1, 1, D_MODEL), lambda l, j: (l, 0, j)),
        ],
        out_specs=pl.BlockSpec((1, COND_PAD, D_MODEL), lambda l, j: (l, 0, j)),
        out_shape=jax.ShapeDtypeStruct((DEPTH, COND_PAD, 6 * D_MODEL), F32),
        compiler_params=_params(("arbitrary", "arbitrary")),
        name="modulation",
    )(cond, w_ada, b_ada.reshape(DEPTH, 1, 6 * D_MODEL))


def _rms_rows(x, g):
    ms = jnp.mean(x * x, axis=-1, keepdims=True)
    return x * lax.rsqrt(ms + EPS) * g


def _group_rmsnorm(a, avg_ref, g):
    sq = a * a
    hi = sq.astype(BF16)
    lo = (sq - hi.astype(F32)).astype(BF16)
    avg = avg_ref[...]
    ms = jnp.dot(hi, avg, preferred_element_type=F32) + jnp.dot(lo, avg, preferred_element_type=F32)
    return a * lax.rsqrt(ms + EPS) * g


def _inproj_kernel(x_ref, g_ref, sh_ref, sc_ref, w_ref, qn_ref, kn_ref, avgq_ref, avgk_ref,
                   q_ref, k_ref, v_ref, rq_ref, rk_ref, rv_ref, sg_ref):
    h = _rms_rows(x_ref[...], g_ref[...]) * (1.0 + sc_ref[0]) + sh_ref[0]
    hb = h.astype(BF16)

    def proj(lo, width):
        return jnp.dot(hb, w_ref[:, lo:lo + width], preferred_element_type=F32)

    o = 0
    q_ref[...] = _group_rmsnorm(proj(o, ATTN_WIDTH), avgq_ref, qn_ref[...])
    o += ATTN_WIDTH
    k_ref[...] = _group_rmsnorm(proj(o, KV_WIDTH), avgk_ref, kn_ref[...])
    o += KV_WIDTH
    v_ref[...] = proj(o, KV_WIDTH)
    o += KV_WIDTH
    rq_ref[...] = proj(o, RET_WIDTH) * (RET_DK ** -0.5)
    o += RET_WIDTH
    rk_ref[...] = proj(o, RET_WIDTH)
    o += RET_WIDTH
    rv_ref[...] = proj(o, RET_WIDTH)
    o += RET_WIDTH
    rg = proj(o, RET_WIDTH)
    sg_ref[...] = rg * jax.nn.sigmoid(rg)


def _mod_spec(col):
    return pl.BlockSpec((1, 1, D_MODEL), lambda i, col=col: (_cond_of_tile(i), 0, col))


def _full(shape):
    return pl.BlockSpec(shape, lambda *_: (0,) * len(shape))


def _input_projection(x, norm_g, mod_l, w_in_bf, qn, kn, avgq, avgk):
    rows = lambda w: pl.BlockSpec((ROW_TILE, w), lambda i: (i, 0))
    widths = (ATTN_WIDTH, KV_WIDTH, KV_WIDTH, RET_WIDTH, RET_WIDTH, RET_WIDTH, RET_WIDTH)
    return pl.pallas_call(
        _inproj_kernel,
        grid=(T_ALL // ROW_TILE,),
        in_specs=[rows(D_MODEL), _full((1, D_MODEL)), _mod_spec(0), _mod_spec(1),
                  _full((D_MODEL, IN_WIDTH)), _full((1, ATTN_WIDTH)), _full((1, KV_WIDTH)),
                  _full((ATTN_WIDTH, ATTN_WIDTH)), _full((KV_WIDTH, KV_WIDTH))],
        out_specs=[rows(w) for w in widths],
        out_shape=[jax.ShapeDtypeStruct((T_ALL, w), F32) for w in widths],
        compiler_params=_params(("arbitrary",)),
        name="norm_inproj",
    )(x, norm_g, mod_l, mod_l, w_in_bf, qn, kn, avgq, avgk)


def _attend(q, kall, vall, valid, sink_ref, o_ref):
    scale = HEAD_DIM ** -0.5
    lane = lax.broadcasted_iota(jnp.int32, (1, LANES), 1)
    low = lane < HEAD_DIM
    k_at, v_at = [], []
    for g in range(N_KV_HEADS):
        keep = low if g == 0 else jnp.logical_not(low)
        kg = jnp.where(keep, kall, 0.0)
        vg = jnp.where(keep, vall, 0.0)
        kr = pltpu.roll(kg, HEAD_DIM, 1)
        vr = pltpu.roll(vg, HEAD_DIM, 1)
        pair_k = (kg, kr) if g == 0 else (kr, kg)
        pair_v = (vg, vr) if g == 0 else (vr, vg)
        k_at.append([t.astype(BF16) for t in pair_k])
        v_at.append([t.astype(BF16) for t in pair_v])
    for j in range(ATTN_WIDTH // LANES):
        qj = q[:, j * LANES:(j + 1) * LANES].astype(BF16)
        acc = None
        for off in range(2):
            h = 2 * j + off
            g = h // (N_Q_HEADS // N_KV_HEADS)
            s = lax.dot_general(qj, k_at[g][off], (((1,), (1,)), ((), ())), preferred_element_type=F32) * scale
            if valid is not None:
                s = jnp.where(valid, s, NEG_BIG)
            sink = sink_ref[h]
            m = jnp.maximum(jnp.max(s, axis=-1, keepdims=True), sink)
            e = jnp.exp(s - m)
            den = jnp.sum(e, axis=-1, keepdims=True) + jnp.exp(sink - m)
            o = jnp.dot(e.astype(BF16), v_at[g][off], preferred_element_type=F32) / den
            acc = o if acc is None else acc + o
        o_ref[:, j * LANES:(j + 1) * LANES] = acc


def _ctx_attn_kernel(sink_ref, q_ref, k_ref, v_ref, o_ref):
    _attend(q_ref[...], k_ref[...], v_ref[...], None, sink_ref, o_ref)


def _context_attention(sink, q, k, v):
    return pl.pallas_call(
        _ctx_attn_kernel,
        grid_spec=pltpu.PrefetchScalarGridSpec(
            num_scalar_prefetch=1,
            grid=(BATCH,),
            in_specs=[pl.BlockSpec((SEQ, ATTN_WIDTH), lambda b, s: (b, 0)),
                      pl.BlockSpec((SEQ, KV_WIDTH), lambda b, s: (b, 0)),
                      pl.BlockSpec((SEQ, KV_WIDTH), lambda b, s: (b, 0))],
            out_specs=pl.BlockSpec((SEQ, ATTN_WIDTH), lambda b, s: (b, 0)),
        ),
        out_shape=jax.ShapeDtypeStruct((T_CTX, ATTN_WIDTH), F32),
        compiler_params=_params(("arbitrary",)),
        name="context_attention",
    )(sink, q, k, v)


def _rope_block(x, cos, sin_signed):
    lane = lax.broadcasted_iota(jnp.int32, (1, LANES), 1)
    first = (lane % (HEAD_DIM // 2)) < (HEAD_DIM // 4)
    swapped = jnp.where(first, pltpu.roll(x, LANES - HEAD_DIM // 4, 1), pltpu.roll(x, HEAD_DIM // 4, 1))
    return x * cos + swapped * sin_signed


LOCAL_KEYS = 3 * BLOCK


def _lat_attn_kernel(sink_ref, q_ref, k_ref, v_ref, ck_ref, cv_ref, cosq_ref, sinq_ref, cosk_ref, sin_k_ref,
                     o_ref):
    n = pl.program_id(1)
    start = pl.multiple_of(jnp.clip((n - 1) * BLOCK, 0, DEC_SEQ - LOCAL_KEYS), BLOCK)
    q = q_ref[...]
    q = jnp.concatenate(
        [_rope_block(q[:, j * LANES:(j + 1) * LANES], cosq_ref[:, j * LANES:(j + 1) * LANES],
                     sinq_ref[:, j * LANES:(j + 1) * LANES]) for j in range(ATTN_WIDTH // LANES)], axis=1)
    kw = _rope_block(k_ref[pl.ds(start, LOCAL_KEYS), :], cosk_ref[pl.ds(start, LOCAL_KEYS), :],
                     sin_k_ref[pl.ds(start, LOCAL_KEYS), :])
    vw = v_ref[pl.ds(start, LOCAL_KEYS), :]
    kall = jnp.concatenate([kw, ck_ref[0, 0]], axis=0)
    vall = jnp.concatenate([vw, cv_ref[0, 0]], axis=0)
    qpos = n * BLOCK + lax.broadcasted_iota(jnp.int32, (BLOCK, LOCAL_KEYS + PAST_LEN), 0)
    col = lax.broadcasted_iota(jnp.int32, (BLOCK, LOCAL_KEYS + PAST_LEN), 1)
    valid = jnp.logical_or(col >= LOCAL_KEYS, jnp.abs(qpos - (start + col)) <= WINDOW)
    _attend(q, kall, vall, valid, sink_ref, o_ref)


def _latent_attention(sink, q, k, v, cache_k, cache_v, layer, cosq, sinq, cosk, sin_k):
    nb = DEC_SEQ // BLOCK
    ctx_block0 = T_CTX // BLOCK
    ctx_seq0 = T_CTX // DEC_SEQ
    cache_spec = pl.BlockSpec((1, 1, PAST_LEN, KV_WIDTH), lambda b, n, s: (b, layer, 0, 0))
    return pl.pallas_call(
        _lat_attn_kernel,
        grid_spec=pltpu.PrefetchScalarGridSpec(
            num_scalar_prefetch=1,
            grid=(DEC_BATCH, nb),
            in_specs=[pl.BlockSpec((BLOCK, ATTN_WIDTH), lambda b, n, s: (ctx_block0 + b * nb + n, 0)),
                      pl.BlockSpec((DEC_SEQ, KV_WIDTH), lambda b, n, s: (ctx_seq0 + b, 0)),
                      pl.BlockSpec((DEC_SEQ, KV_WIDTH), lambda b, n, s: (ctx_seq0 + b, 0)),
                      cache_spec, cache_spec,
                      pl.BlockSpec((BLOCK, ATTN_WIDTH), lambda b, n, s: (n, 0)),
                      pl.BlockSpec((BLOCK, ATTN_WIDTH), lambda b, n, s: (n, 0)),
                      pl.BlockSpec((DEC_SEQ, KV_WIDTH), lambda b, n, s: (0, 0)),
                      pl.BlockSpec((DEC_SEQ, KV_WIDTH), lambda b, n, s: (0, 0))],
            out_specs=pl.BlockSpec((BLOCK, ATTN_WIDTH), lambda b, n, s: (b * nb + n, 0)),
        ),
        out_shape=jax.ShapeDtypeStruct((T_LAT, ATTN_WIDTH), F32),
        compiler_params=_params(("arbitrary", "arbitrary")),
        name="latent_attention",
    )(sink, q, k, v, cache_k, cache_v, cosq, sinq, cosk, sin_k)


def _rope_tables():
    t = jnp.arange(DEC_SEQ)
    nf = HEAD_DIM // 4
    inv = ROPE_BASE ** (-jnp.arange(nf, dtype=F32) / nf)

    def half(coord):
        ang = coord.astype(F32)[:, None] * inv[None, :]
        c, s = jnp.cos(ang), jnp.sin(ang)
        return jnp.concatenate([c, c], axis=1), jnp.concatenate([-s, s], axis=1)

    cr, sr = half(t // GRID_W)
    cc, sc = half(t % GRID_W)
    cos = jnp.concatenate([cr, cc], axis=1)
    sin = jnp.concatenate([sr, sc], axis=1)
    return (jnp.tile(cos, (1, N_Q_HEADS)), jnp.tile(sin, (1, N_Q_HEADS)),
            jnp.tile(cos, (1, N_KV_HEADS)), jnp.tile(sin, (1, N_KV_HEADS)))


def _ret_kernel(lg_ref, cd_ref, q_ref, k_ref, v_ref, sg_ref, gn_ref, *rest, n_chunks, has_s0, write_state):
    rest = list(rest)
    s0_ref = rest.pop(0) if has_s0 else None
    o_ref = rest.pop(0)
    sf_ref = rest.pop(0) if write_state else None
    acc_ref = rest.pop(0)
    row = lax.broadcasted_iota(jnp.int32, (CHUNK, CHUNK), 0).astype(F32)
    col = lax.broadcasted_iota(jnp.int32, (CHUNK, CHUNK), 1).astype(F32)
    rel = row - col
    pos = lax.broadcasted_iota(jnp.int32, (CHUNK, 1), 0).astype(F32)

    def run(direction, h):
        cols = slice(h * RET_DK, (h + 1) * RET_DK)
        gn = gn_ref[:, cols]
        lg = lg_ref[direction * N_RET_HEADS + h]
        cd = cd_ref[direction * N_RET_HEADS + h]
        if direction == 0:
            intra = jnp.where(rel >= 0, jnp.exp(lg * rel), 0.0)
            q_dec = jnp.exp(lg * (pos + 1.0))
            k_dec = jnp.exp(lg * (CHUNK - 1.0 - pos))
            order = range(n_chunks)
        else:
            intra = jnp.where(rel <= 0, jnp.exp(-lg * rel), 0.0)
            q_dec = jnp.exp(lg * (CHUNK - pos))
            k_dec = jnp.exp(lg * pos)
            order = range(n_chunks - 1, -1, -1)
        state = s0_ref[0, 0, direction, h] if has_s0 else jnp.zeros((RET_DK, RET_DK), F32)
        for c in order:
            rows = slice(c * CHUNK, (c + 1) * CHUNK)
            qc, kc, vc = q_ref[rows, cols], k_ref[rows, cols], v_ref[rows, cols]
            qb, kb, vb = qc.astype(BF16), kc.astype(BF16), vc.astype(BF16)
            scores = lax.dot_general(qb, kb, (((1,), (1,)), ((), ())), preferred_element_type=F32) * intra
            o = (jnp.dot(scores.astype(BF16), vb, preferred_element_type=F32)
                 + jnp.dot(qb, state.astype(BF16), preferred_element_type=F32) * q_dec)
            kd_t = (kc * k_dec).T.astype(BF16)
            state = state * cd + jnp.dot(kd_t, vb, preferred_element_type=F32)
            if direction == 0:
                acc_ref[rows, cols] = o
            else:
                tot = acc_ref[rows, cols] + o
                o_ref[rows, cols] = _rms_rows(tot, gn) * sg_ref[rows, cols]
        if write_state:
            sf_ref[0, direction, h] = state

    for direction in range(2):
        for h in range(N_RET_HEADS):
            run(direction, h)


def _retention(lg, cd, rq, rk, rv, sg, gn, s0, layer, *, n_seq, seq_len, row0, write_state):
    blk0 = row0 // seq_len
    rows = pl.BlockSpec((seq_len, RET_WIDTH), lambda b, *_: (blk0 + b, 0))
    in_specs = [rows, rows, rows, rows, pl.BlockSpec((1, RET_WIDTH), lambda b, *_: (0, 0))]
    args = [rq, rk, rv, sg, gn]
    if s0 is not None:
        in_specs.append(pl.BlockSpec((1, 1, 2, N_RET_HEADS, RET_DK, RET_DK), lambda b, *_: (b, layer, 0, 0, 0, 0)))
        args.append(s0)
    out_specs = [pl.BlockSpec((seq_len, RET_WIDTH), lambda b, *_: (b, 0))]
    out_shape = [jax.ShapeDtypeStruct((n_seq * seq_len, RET_WIDTH), F32)]
    if write_state:
        out_specs.append(pl.BlockSpec((1, 2, N_RET_HEADS, RET_DK, RET_DK), lambda b, *_: (b, 0, 0, 0, 0)))
        out_shape.append(jax.ShapeDtypeStruct((n_seq, 2, N_RET_HEADS, RET_DK, RET_DK), F32))
    kern = functools.partial(_ret_kernel, n_chunks=seq_len // CHUNK, has_s0=s0 is not None,
                             write_state=write_state)
    return pl.pallas_call(
        kern,
        grid_spec=pltpu.PrefetchScalarGridSpec(
            num_scalar_prefetch=2,
            grid=(n_seq,),
            in_specs=in_specs,
            out_specs=out_specs,
            scratch_shapes=[pltpu.VMEM((seq_len, RET_WIDTH), F32)],
        ),
        out_shape=out_shape,
        compiler_params=_params(("arbitrary",)),
        name="retention_ctx" if write_state else "retention_lat",
    )(lg, cd, *args)


def _outproj_kernel(attc_ref, attl_ref, retc_ref, retl_ref, x_ref, wo_ref, g1_ref, sh2_ref, sc2_ref, nf_ref,
                    wrh_ref, wrl_ref, br_ref, x1_ref, h2t_ref, ti_ref, tw_ref):
    is_ctx = pl.program_id(0) < T_CTX // ROW_TILE
    att = jnp.where(is_ctx, attc_ref[...], attl_ref[...])
    ret = jnp.where(is_ctx, retc_ref[...], retl_ref[...])
    y = (jnp.dot(att.astype(BF16), wo_ref[0:ATTN_WIDTH, :], preferred_element_type=F32)
         + jnp.dot(ret.astype(BF16), wo_ref[ATTN_WIDTH:, :], preferred_element_type=F32))
    x1 = x_ref[...] + g1_ref[0] * y
    x1_ref[...] = x1
    h2 = _rms_rows(x1, nf_ref[...]) * (1.0 + sc2_ref[0]) + sh2_ref[0]
    for s in range(SLABS):
        h2t_ref[pl.ds(s, ROW_TILE, stride=SLABS), :] = h2[:, s * LANES:(s + 1) * LANES]
    hh = h2.astype(BF16)
    hl = (h2 - hh.astype(F32)).astype(BF16)
    wrh = wrh_ref[...]
    logits = (jnp.dot(hh, wrh, preferred_element_type=F32) + jnp.dot(hl, wrh, preferred_element_type=F32)
              + jnp.dot(hh, wrl_ref[...], preferred_element_type=F32) + br_ref[...])
    lane = lax.broadcasted_iota(jnp.int32, logits.shape, 1)
    vals, idxs = [], []
    cur = logits
    for _ in range(TOP_K):
        m = jnp.max(cur, axis=-1, keepdims=True)
        idx = jnp.min(jnp.where(cur == m, lane, LANES), axis=-1, keepdims=True)
        vals.append(m)
        idxs.append(idx)
        cur = jnp.where(lane == idx, -jnp.inf, cur)
    es = [jnp.exp(v - vals[0]) for v in vals]
    den = es[0] + es[1] + es[2] + es[3]
    ti = jnp.zeros(logits.shape, jnp.int32)
    tw = jnp.zeros(logits.shape, F32)
    for k in range(TOP_K):
        ti = jnp.where(lane == k, idxs[k], ti)
        tw = jnp.where(lane == k, es[k] / den, tw)
    ti_ref[...] = ti
    tw_ref[...] = tw


def _output_projection(att_c, att_l, ret_c, ret_l, x, w_out_bf, mod_l, nf, wrh, wrl, br):
    n_ctx = T_CTX // ROW_TILE
    rows = lambda w: pl.BlockSpec((ROW_TILE, w), lambda i: (i, 0))
    ctx_rows = lambda w: pl.BlockSpec((ROW_TILE, w), lambda i: (jnp.minimum(i, n_ctx - 1), 0))
    lat_rows = lambda w: pl.BlockSpec((ROW_TILE, w), lambda i: (jnp.maximum(i - n_ctx, 0), 0))
    return pl.pallas_call(
        _outproj_kernel,
        grid=(T_ALL // ROW_TILE,),
        in_specs=[ctx_rows(ATTN_WIDTH), lat_rows(ATTN_WIDTH), ctx_rows(RET_WIDTH), lat_rows(RET_WIDTH),
                  rows(D_MODEL), _full((D_MODEL, D_MODEL)),
                  _mod_spec(2), _mod_spec(3), _mod_spec(4), _full((1, D_MODEL)),
                  _full((D_MODEL, LANES)), _full((D_MODEL, LANES)), _full((1, LANES))],
        out_specs=[rows(D_MODEL), pl.BlockSpec((ROW_TILE * SLABS, LANES), lambda i: (i, 0)), rows(LANES), rows(LANES)],
        out_shape=[jax.ShapeDtypeStruct((T_ALL, D_MODEL), F32), jax.ShapeDtypeStruct((T_ALL * SLABS, LANES), F32),
                   jax.ShapeDtypeStruct((T_ALL, LANES), jnp.int32), jax.ShapeDtypeStruct((T_ALL, LANES), F32)],
        compiler_params=_params(("arbitrary",)),
        name="outproj_router",
    )(att_c, att_l, ret_c, ret_l, x, w_out_bf, mod_l, mod_l, mod_l, nf, wrh, wrl, br)


def _token_tile(ref, t):
    return ref.at[pl.ds(pl.multiple_of(t * SLABS, SLABS), SLABS), :]


def _step_is_used(i, nused_ref):
    return i * MOE_STEP_BLOCKS < nused_ref[0]


def _dispatch_kernel(nused_ref, src_ref, h2t_hbm, o_ref, hres, xg, sem):
    i = pl.program_id(0)

    @pl.when(i == 0)
    def _():
        cp = pltpu.make_async_copy(h2t_hbm, hres.at[pl.ds(0, T_ALL * SLABS), :], sem.at[0])
        cp.start()
        _token_tile(hres, T_ALL)[...] = jnp.zeros((SLABS, LANES), F32)
        cp.wait()

    @pl.when(_step_is_used(i, nused_ref))
    def _():
        def body(r, carry):
            _token_tile(xg, r)[...] = _token_tile(hres, src_ref[0, 0, r])[...]
            return carry

        lax.fori_loop(0, MOE_STEP_ROWS, body, 0, unroll=8)
        for s in range(SLABS):
            o_ref[:, s * LANES:(s + 1) * LANES] = xg[pl.ds(s, MOE_STEP_ROWS, stride=SLABS), :].astype(BF16)

    @pl.when(jnp.logical_not(_step_is_used(i, nused_ref)))
    def _():
        o_ref[...] = jnp.zeros(o_ref.shape, BF16)


def _smem_rows(width):
    return pl.BlockSpec((1, 1, width), lambda i, nused: (jnp.minimum(i, N_MOE_STEPS - 1), 0, 0),
                        memory_space=pltpu.SMEM)


def _moe_dispatch(n_used, src3, h2t):
    return pl.pallas_call(
        _dispatch_kernel,
        grid_spec=pltpu.PrefetchScalarGridSpec(
            num_scalar_prefetch=1,
            grid=(N_MOE_STEPS,),
            in_specs=[_smem_rows(MOE_STEP_ROWS), pl.BlockSpec(memory_space=pl.ANY)],
            out_specs=pl.BlockSpec((MOE_STEP_ROWS, D_MODEL), lambda i, nused: (i, 0)),
            scratch_shapes=[pltpu.VMEM((N_TILES * SLABS, LANES), F32), pltpu.VMEM((MOE_STEP_ROWS * SLABS, LANES), F32),
                            pltpu.SemaphoreType.DMA((1,))],
        ),
        out_shape=jax.ShapeDtypeStruct((N_PAD, D_MODEL), BF16),
        compiler_params=_params(("arbitrary",)),
        name="moe_dispatch",
    )(n_used, src3, h2t)


X_SLOTS = 8
X_AHEAD = 6
Y_SLOTS = 4


def _block_rows(ref, g):
    return ref.at[pl.ds(pl.multiple_of(g * MOE_ROWS, MOE_ROWS), MOE_ROWS), :]


TILE_ROWS = MOE_ROWS * SLABS


def _tile_block(ref, g):
    return ref.at[pl.ds(pl.multiple_of(g * TILE_ROWS, TILE_ROWS), TILE_ROWS), :]


def _experts_kernel(blk0_ref, nblk_ref, nused_ref, src_ref, h2t_hbm, wgu_ref, bgu_ref, wdn_ref, bdn_ref, yst_hbm,
                    xbuf, ybuf, wgu_bf, wdn_bf, xsem, ysem):
    e = pl.program_id(0)
    b0, nb, nused = blk0_ref[e], nblk_ref[e], nused_ref[0]

    def row_copy(g, slot, r):
        tok = jnp.minimum(src_ref[g * MOE_ROWS + r], T_ALL - 1)
        dst = xbuf.at[pl.ds(pl.multiple_of(slot * TILE_ROWS + r * SLABS, SLABS), SLABS), :]
        return pltpu.make_async_copy(_token_tile(h2t_hbm, tok), dst, xsem.at[slot])

    def gather_wait(slot):
        pltpu.make_async_copy(_tile_block(h2t_hbm, 0), _tile_block(xbuf, slot), xsem.at[slot]).wait()

    def y_copy(g):
        slot = g % Y_SLOTS
        return pltpu.make_async_copy(_tile_block(ybuf, slot), _tile_block(yst_hbm, g), ysem.at[slot])

    @pl.when(e == 0)
    def _():
        for k in range(X_AHEAD):
            @pl.when(k < nused)
            def _():
                def issue(r, carry):
                    row_copy(k, k, r).start()
                    return carry
                lax.fori_loop(0, MOE_ROWS, issue, 0)

    @pl.when(nb > 0)
    def _():
        wgu_bf[...] = wgu_ref[...].astype(BF16)
        wdn_bf[...] = wdn_ref[...].astype(BF16)

        def body(j, carry):
            g = b0 + j
            slot = g % X_SLOTS
            gather_wait(slot)

            x = jnp.concatenate(
                [xbuf[pl.ds(slot * TILE_ROWS + s, MOE_ROWS, stride=SLABS), :].astype(BF16) for s in range(SLABS)],
                axis=1)

            ahead = g + X_AHEAD
            ahead_slot = jnp.where(ahead < nused, ahead % X_SLOTS, X_SLOTS + ahead - nused)
            ahead_blk = jnp.minimum(ahead, nused - 1)
            for r in range(MOE_ROWS):
                row_copy(ahead_blk, ahead_slot, r).start()

            gu = jnp.dot(x, wgu_bf[...], preferred_element_type=F32) + bgu_ref[...]
            x_glu = jnp.minimum(gu[:, :D_FF], SWIGLU_LIMIT)
            x_lin = jnp.clip(gu[:, D_FF:], -SWIGLU_LIMIT, SWIGLU_LIMIT)
            act = x_glu * jax.nn.sigmoid(SWIGLU_ALPHA * x_glu) * (x_lin + 1.0)
            out = jnp.dot(act.astype(BF16), wdn_bf[...], preferred_element_type=F32) + bdn_ref[...]

            @pl.when(g >= Y_SLOTS)
            def _():
                y_copy(g - Y_SLOTS).wait()

            yslot = g % Y_SLOTS
            for s in range(SLABS):
                ybuf[pl.ds(yslot * TILE_ROWS + s, MOE_ROWS, stride=SLABS), :] = out[:, s * LANES:(s + 1) * LANES]
            y_copy(g).start()
            return carry

        lax.fori_loop(0, nb, body, 0)

    @pl.when(e == N_EXPERTS - 1)
    def _():
        for k in range(X_AHEAD):
            @pl.when(k < nused)
            def _():
                gather_wait(X_SLOTS + X_AHEAD - 1 - k)

        for k in range(1, Y_SLOTS + 1):
            @pl.when(nused >= k)
            def _():
                y_copy(nused - k).wait()

        _tile_block(ybuf, 0)[...] = jnp.zeros((TILE_ROWS, LANES), F32)

        def fill(g, carry):
            cp = pltpu.make_async_copy(_tile_block(ybuf, 0), _tile_block(yst_hbm, g), ysem.at[0])
            cp.start()
            cp.wait()
            return carry

        lax.fori_loop(nused, N_MOE_BLOCKS, fill, 0)


def _moe_experts(blk0, nblk, n_used, src, h2t, w_gu, b_gu, w_dn, b_dn, layer):
    wspec = lambda rows, cols: pl.BlockSpec((None, None, rows, cols), lambda e, *_: (layer, e, 0, 0))
    return pl.pallas_call(
        _experts_kernel,
        grid_spec=pltpu.PrefetchScalarGridSpec(
            num_scalar_prefetch=4,
            grid=(N_EXPERTS,),
            in_specs=[pl.BlockSpec(memory_space=pl.ANY),
                      wspec(D_MODEL, 2 * D_FF), wspec(1, 2 * D_FF), wspec(D_FF, D_MODEL), wspec(1, D_MODEL)],
            out_specs=pl.BlockSpec(memory_space=pl.ANY),
            scratch_shapes=[pltpu.VMEM(((X_SLOTS + X_AHEAD) * TILE_ROWS, LANES), F32),
                            pltpu.VMEM((Y_SLOTS * TILE_ROWS, LANES), F32),
                            pltpu.VMEM((D_MODEL, 2 * D_FF), BF16), pltpu.VMEM((D_FF, D_MODEL), BF16),
                            pltpu.SemaphoreType.DMA((X_SLOTS + X_AHEAD,)), pltpu.SemaphoreType.DMA((Y_SLOTS,))],
        ),
        out_shape=jax.ShapeDtypeStruct((N_PAD * SLABS, LANES), F32),
        compiler_params=_params(("arbitrary",), vmem=EXPERT_VMEM_LIMIT),
        name="moe_experts",
    )(blk0, nblk, n_used, src, h2t, w_gu, b_gu.reshape(DEPTH, N_EXPERTS, 1, 2 * D_FF),
      w_dn, b_dn.reshape(DEPTH, N_EXPERTS, 1, D_MODEL))


N_ROW_TILES = T_ALL // ROW_TILE
ZERO_ROWS = ROW_TILE * SLABS


def _combine_kernel(nused_ref, src_ref, w_ref, ys_ref, x1_ref, g2_ref, o_ref, yres):
    i = pl.program_id(0)

    @pl.when(i == 0)
    def _():
        def zero(j, carry):
            yres[pl.ds(pl.multiple_of(j * ZERO_ROWS, ZERO_ROWS), ZERO_ROWS), :] = jnp.zeros((ZERO_ROWS, LANES), F32)
            return carry
        lax.fori_loop(0, N_ROW_TILES, zero, 0)
        _token_tile(yres, T_ALL)[...] = jnp.zeros((SLABS, LANES), F32)

    @pl.when(jnp.logical_and(i < N_MOE_STEPS, _step_is_used(i, nused_ref)))
    def _():
        def group(g, carry):
            rows = [g * GROUP + j for j in range(GROUP)]
            toks = [src_ref[0, 0, r] for r in rows]
            ws = [w_ref[0, 0, r] for r in rows]
            new = [_token_tile(yres, t)[...] + w * _token_tile(ys_ref, r)[...] for t, w, r in zip(toks, ws, rows)]
            for t, v in zip(toks, new):
                _token_tile(yres, t)[...] = v
            return carry

        lax.fori_loop(0, MOE_STEP_ROWS // GROUP, group, 0)

    @pl.when(i >= N_MOE_STEPS)
    def _():
        base = (i - N_MOE_STEPS) * ZERO_ROWS
        for s in range(SLABS):
            cols = slice(s * LANES, (s + 1) * LANES)
            y = yres[pl.ds(base + s, ROW_TILE, stride=SLABS), :]
            o_ref[:, cols] = x1_ref[:, cols] + g2_ref[0][:, cols] * y


def _moe_combine(n_used, src3, w3, ys, x1, mod_l):
    tile = lambda i: jnp.maximum(i - N_MOE_STEPS, 0)
    last_used = lambda nused: jnp.maximum(nused[0] - 1, 0) // MOE_STEP_BLOCKS
    return pl.pallas_call(
        _combine_kernel,
        grid_spec=pltpu.PrefetchScalarGridSpec(
            num_scalar_prefetch=1,
            grid=(N_MOE_STEPS + N_ROW_TILES,),
            in_specs=[_smem_rows(MOE_STEP_ROWS), _smem_rows(MOE_STEP_ROWS),
                      pl.BlockSpec((MOE_STEP_ROWS * SLABS, LANES), lambda i, nused: (jnp.minimum(i, last_used(nused)), 0)),
                      pl.BlockSpec((ROW_TILE, D_MODEL), lambda i, nused: (tile(i), 0)),
                      pl.BlockSpec((1, 1, D_MODEL), lambda i, nused: (_cond_of_tile(tile(i)), 0, 5))],
            out_specs=pl.BlockSpec((ROW_TILE, D_MODEL), lambda i, nused: (tile(i), 0)),
            scratch_shapes=[pltpu.VMEM((N_TILES * SLABS, LANES), F32)],
        ),
        out_shape=jax.ShapeDtypeStruct((T_ALL, D_MODEL), F32),
        compiler_params=_params(("arbitrary",)),
        name="moe_combine",
    )(n_used, src3, w3, ys, x1, mod_l)


def _routing_tables(top_idx, top_w):
    flat_e = top_idx.reshape(N_ASSIGN)
    experts = jnp.arange(N_EXPERTS, dtype=jnp.int32)
    counts = jnp.sum((flat_e[:, None] == experts[None, :]).astype(jnp.int32), axis=0)
    pad = (-counts) % MOE_ROWS
    spare = jnp.arange(MOE_ROWS, dtype=jnp.int32)
    pad_keys = jnp.where(spare[None, :] < pad[:, None], experts[:, None], N_EXPERTS).reshape(-1)
    keys = jnp.concatenate([flat_e, pad_keys])
    toks = jnp.concatenate([jnp.arange(N_ASSIGN, dtype=jnp.int32) // TOP_K,
                            jnp.full((N_PAD - N_ASSIGN,), T_ALL, jnp.int32)])
    wts = jnp.concatenate([top_w.reshape(N_ASSIGN), jnp.zeros((N_PAD - N_ASSIGN,), F32)])
    _, src, w_sorted = lax.sort((keys, toks, wts), num_keys=1, is_stable=True)
    padded = counts + pad
    blk_end = jnp.cumsum(padded) // MOE_ROWS
    nblk = padded // MOE_ROWS
    return (src.reshape(N_MOE_STEPS, 1, MOE_STEP_ROWS), w_sorted.reshape(N_MOE_STEPS, 1, MOE_STEP_ROWS),
            (blk_end - nblk).astype(jnp.int32), nblk.astype(jnp.int32), blk_end[-1:].astype(jnp.int32))


def kernel(x_prompt, x_sample, cache_attn_k, cache_attn_v, state_ret, c, c_ctx, norm_mix, norm_ffn, w_ada, b_ada,
           w_in, q_norm, k_norm, attn_sink, ret_decay, ret_norm, w_out, w_router, b_router, w_gate_up, b_gate_up,
           w_down, b_down):
    x = jnp.concatenate([x_prompt.reshape(T_CTX, D_MODEL), x_sample.reshape(T_LAT, D_MODEL)], axis=0)
    cond = jnp.zeros((COND_PAD, D_MODEL), F32).at[0].set(c_ctx).at[1:N_COND].set(c)
    mod = _modulation(cond, w_ada, b_ada)[:, :N_COND].reshape(DEPTH, N_COND, 1, 6 * D_MODEL)

    cache_k = cache_attn_k.reshape(DEC_BATCH, DEPTH, PAST_LEN, KV_WIDTH)
    cache_v = cache_attn_v.reshape(DEC_BATCH, DEPTH, PAST_LEN, KV_WIDTH)
    cosq, sinq, cosk, sin_k = _rope_tables()
    grp = jnp.arange(ATTN_WIDTH) // HEAD_DIM
    avgq = jnp.where(grp[:, None] == grp[None, :], 1.0 / HEAD_DIM, 0.0).astype(BF16)
    avgk = avgq[:KV_WIDTH, :KV_WIDTH]
    log_gamma = jax.nn.log_sigmoid(ret_decay.astype(F32))
    chunk_decay = jnp.exp(log_gamma * CHUNK)

    new_k, new_v, new_s = [], [], []
    for l in range(DEPTH):
        mod_l = mod[l]
        q, k, v, rq, rk, rv, sg = _input_projection(
            x, norm_mix[l].reshape(1, D_MODEL), mod_l, w_in[l].astype(BF16),
            jnp.tile(q_norm[l], N_Q_HEADS).reshape(1, ATTN_WIDTH), jnp.tile(k_norm[l], N_KV_HEADS).reshape(1, KV_WIDTH),
            avgq, avgk)
        new_k.append(k[:T_CTX].reshape(BATCH, SEQ, N_KV_HEADS, HEAD_DIM))
        new_v.append(v[:T_CTX].reshape(BATCH, SEQ, N_KV_HEADS, HEAD_DIM))
        sink = attn_sink[l].astype(F32)
        att_c = _context_attention(sink, q, k, v)
        att_l = _latent_attention(sink, q, k, v, cache_k, cache_v, l, cosq, sinq, cosk, sin_k)
        lg = log_gamma[l].reshape(2 * N_RET_HEADS)
        cd = chunk_decay[l].reshape(2 * N_RET_HEADS)
        gn = ret_norm[l].reshape(1, RET_WIDTH)
        ret_c, s_fin = _retention(lg, cd, rq, rk, rv, sg, gn, None, l, n_seq=BATCH, seq_len=SEQ, row0=0,
                                  write_state=True)
        (ret_l,) = _retention(lg, cd, rq, rk, rv, sg, gn, state_ret, l, n_seq=DEC_BATCH, seq_len=DEC_SEQ,
                              row0=T_CTX, write_state=False)
        new_s.append(s_fin)
        wr = jnp.zeros((D_MODEL, LANES), F32).at[:, :N_EXPERTS].set(w_router[l])
        wrh = wr.astype(BF16)
        wrl = (wr - wrh.astype(F32)).astype(BF16)
        br = jnp.full((1, LANES), NEG_BIG, F32).at[0, :N_EXPERTS].set(b_router[l])
        x1, h2t, ti, tw = _output_projection(att_c, att_l, ret_c, ret_l, x, w_out[l].astype(BF16), mod_l,
                                             norm_ffn[l].reshape(1, D_MODEL), wrh, wrl, br)
        src3, w3, blk0, nblk, n_used = _routing_tables(ti[:, :TOP_K], tw[:, :TOP_K])
        ys = _moe_experts(blk0, nblk, n_used, src3.reshape(N_PAD), h2t, w_gate_up, b_gate_up, w_down, b_down, l)
        x = _moe_combine(n_used, src3, w3, ys, x1, mod_l)

    y_prompt = x[:T_CTX].reshape(BATCH, SEQ, D_MODEL)
    y_sample = x[T_CTX:].reshape(DEC_BATCH, DEC_SEQ, D_MODEL)
    return (y_prompt, y_sample, jnp.stack(new_k, axis=1), jnp.stack(new_v, axis=1), jnp.stack(new_s, axis=1))
```

```python
import functools

import jax
import jax.numpy as jnp
from jax import lax
from jax.experimental import pallas as pl
from jax.experimental.pallas import tpu as pltpu

F32 = jnp.float32
BF16 = jnp.bfloat16

D_MODEL = 1024
DEPTH = 4
BATCH, SEQ = 16, 256
DEC_BATCH, DEC_SEQ = 2, 1024
PAST_LEN = 512
GRID_W = 64
HEAD_DIM = 64
N_Q_HEADS = 8
N_KV_HEADS = 2
ATTN_WIDTH = N_Q_HEADS * HEAD_DIM
KV_WIDTH = N_KV_HEADS * HEAD_DIM
WINDOW = 128
BLOCK = 128
ROPE_BASE = 10000.0
N_RET_HEADS = 4
RET_DK = 128
RET_WIDTH = N_RET_HEADS * RET_DK
CHUNK = 128
IN_WIDTH = ATTN_WIDTH + 2 * KV_WIDTH + 4 * RET_WIDTH
N_EXPERTS = 32
TOP_K = 4
D_FF = D_MODEL
SWIGLU_LIMIT = 7.0
SWIGLU_ALPHA = 1.702
EPS = 1e-6

T_CTX = BATCH * SEQ
T_LAT = DEC_BATCH * DEC_SEQ
T_ALL = T_CTX + T_LAT
N_COND = 1 + DEC_BATCH
COND_PAD = 8
LANES = 128
ROW_TILE = 512
MOE_ROWS = 128
N_ASSIGN = T_ALL * TOP_K
N_MOE_BLOCKS = N_ASSIGN // MOE_ROWS + N_EXPERTS
N_PAD = N_MOE_BLOCKS * MOE_ROWS
MOE_STEP_BLOCKS = 4
MOE_STEP_ROWS = MOE_STEP_BLOCKS * MOE_ROWS
N_MOE_STEPS = N_MOE_BLOCKS // MOE_STEP_BLOCKS
SLABS = D_MODEL // LANES
GROUP = 8
N_TILES = T_ALL + 1
NEG_BIG = -1e30
VMEM_LIMIT = 48 * 1024 * 1024
EXPERT_VMEM_LIMIT = 56 * 1024 * 1024


def _cond_of_tile(i):
    return jnp.where(i < T_CTX // ROW_TILE, 0, 1 + (i - T_CTX // ROW_TILE) // (DEC_SEQ // ROW_TILE))


def _params(sem, vmem=VMEM_LIMIT):
    return pltpu.CompilerParams(dimension_semantics=sem, vmem_limit_bytes=vmem)


def _mod_kernel(c_ref, w_ref, b_ref, o_ref):
    c = c_ref[...]
    s = (c * jax.nn.sigmoid(c)).astype(BF16)
    o_ref[0] = jnp.dot(s, w_ref[0].astype(BF16), preferred_element_type=F32) + b_ref[0]


def _modulation(cond, w_ada, b_ada):
    n_col = 6 * D_MODEL // D_MODEL
    return pl.pallas_call(
        _mod_kernel,
        grid=(DEPTH, n_col),
        in_specs=[
            pl.BlockSpec((COND_PAD, D_MODEL), lambda l, j: (0, 0)),
            pl.BlockSpec((1, D_MODEL, D_MODEL), lambda l, j: (l, 0, j)),
            pl.BlockSpec((1, 1, D_MODEL), lambda l, j: (l, 0, j)),
        ],
        out_specs=pl.BlockSpec((1, COND_PAD, D_MODEL), lambda l, j: (l, 0, j)),
        out_shape=jax.ShapeDtypeStruct((DEPTH, COND_PAD, 6 * D_MODEL), F32),
        compiler_params=_params(("arbitrary", "arbitrary")),
        name="modulation",
    )(cond, w_ada, b_ada.reshape(DEPTH, 1, 6 * D_MODEL))


def _rms_rows(x, g):
    ms = jnp.mean(x * x, axis=-1, keepdims=True)
    return x * lax.rsqrt(ms + EPS) * g


def _group_rmsnorm(a, avg_ref, g):
    ms = jnp.dot((a * a).astype(BF16), avg_ref[...], preferred_element_type=F32)
    return a * lax.rsqrt(ms + EPS) * g


def _inproj_kernel(x_ref, g_ref, sh_ref, sc_ref, w_ref, qn_ref, kn_ref, avgq_ref, avgk_ref,
                   q_ref, k_ref, v_ref, rq_ref, rk_ref, rv_ref, sg_ref):
    h = _rms_rows(x_ref[...], g_ref[...]) * (1.0 + sc_ref[0]) + sh_ref[0]
    hb = h.astype(BF16)

    def proj(lo, width):
        return jnp.dot(hb, w_ref[:, lo:lo + width], preferred_element_type=F32)

    o = 0
    q_ref[...] = _group_rmsnorm(proj(o, ATTN_WIDTH), avgq_ref, qn_ref[...])
    o += ATTN_WIDTH
    k_ref[...] = _group_rmsnorm(proj(o, KV_WIDTH), avgk_ref, kn_ref[...])
    o += KV_WIDTH
    v_ref[...] = proj(o, KV_WIDTH)
    o += KV_WIDTH
    rq_ref[...] = proj(o, RET_WIDTH) * (RET_DK ** -0.5)
    o += RET_WIDTH
    rk_ref[...] = proj(o, RET_WIDTH)
    o += RET_WIDTH
    rv_ref[...] = proj(o, RET_WIDTH)
    o += RET_WIDTH
    rg = proj(o, RET_WIDTH)
    sg_ref[...] = rg * jax.nn.sigmoid(rg)


def _mod_spec(col):
    return pl.BlockSpec((1, 1, D_MODEL), lambda i, col=col: (_cond_of_tile(i), 0, col))


def _full(shape):
    return pl.BlockSpec(shape, lambda *_: (0,) * len(shape))


def _input_projection(x, norm_g, mod_l, w_in_bf, qn, kn, avgq, avgk):
    rows = lambda w: pl.BlockSpec((ROW_TILE, w), lambda i: (i, 0))
    widths = (ATTN_WIDTH, KV_WIDTH, KV_WIDTH, RET_WIDTH, RET_WIDTH, RET_WIDTH, RET_WIDTH)
    return pl.pallas_call(
        _inproj_kernel,
        grid=(T_ALL // ROW_TILE,),
        in_specs=[rows(D_MODEL), _full((1, D_MODEL)), _mod_spec(0), _mod_spec(1),
                  _full((D_MODEL, IN_WIDTH)), _full((1, ATTN_WIDTH)), _full((1, KV_WIDTH)),
                  _full((ATTN_WIDTH, ATTN_WIDTH)), _full((KV_WIDTH, KV_WIDTH))],
        out_specs=[rows(w) for w in widths],
        out_shape=[jax.ShapeDtypeStruct((T_ALL, w), F32) for w in widths],
        compiler_params=_params(("arbitrary",)),
        name="norm_inproj",
    )(x, norm_g, mod_l, mod_l, w_in_bf, qn, kn, avgq, avgk)


def _attend(q, kall, vall, valid_of, sink_ref, o_ref):
    m_rows, n_keys = q.shape[0], kall.shape[0]
    scale = HEAD_DIM ** -0.5
    lane = lax.broadcasted_iota(jnp.int32, (1, LANES), 1)
    low = lane < HEAD_DIM
    row = lax.broadcasted_iota(jnp.int32, (2 * m_rows, 1), 0)
    second = row >= m_rows
    valid = None
    if valid_of is not None:
        qrow = lax.broadcasted_iota(jnp.int32, (2 * m_rows, n_keys), 0)
        qrow = jnp.where(qrow >= m_rows, qrow - m_rows, qrow)
        valid = valid_of(qrow, lax.broadcasted_iota(jnp.int32, (2 * m_rows, n_keys), 1))
    for g in range(N_KV_HEADS):
        keep = low if g == 0 else jnp.logical_not(low)
        kg = jnp.where(keep, kall, 0.0)
        vg = jnp.where(keep, vall, 0.0)
        kr = pltpu.roll(kg, HEAD_DIM, 1)
        vr = pltpu.roll(vg, HEAD_DIM, 1)
        k_at = (kg, kr) if g == 0 else (kr, kg)
        v_at = (vg, vr) if g == 0 else (vr, vg)
        kcat = jnp.concatenate(k_at, axis=0).astype(BF16)
        qg = jnp.concatenate([q[:, (2 * g + b) * LANES:(2 * g + b + 1) * LANES] for b in range(2)],
                             axis=0).astype(BF16)
        s_all = lax.dot_general(qg, kcat, (((1,), (1,)), ((), ())), preferred_element_type=F32) * scale
        acc = None
        for off in range(2):
            s = s_all[:, off * n_keys:(off + 1) * n_keys]
            if valid is not None:
                s = jnp.where(valid, s, NEG_BIG)
            h0 = 4 * g + off
            sink = jnp.where(second, sink_ref[h0 + 2], sink_ref[h0])
            m = jnp.maximum(jnp.max(s, axis=-1, keepdims=True), sink)
            e = jnp.exp(s - m)
            den = jnp.sum(e, axis=-1, keepdims=True) + jnp.exp(sink - m)
            o = jnp.dot(e.astype(BF16), v_at[off].astype(BF16), preferred_element_type=F32) / den
            acc = o if acc is None else acc + o
        for b in range(2):
            o_ref[:, (2 * g + b) * LANES:(2 * g + b + 1) * LANES] = acc[b * m_rows:(b + 1) * m_rows]


def _ctx_attn_kernel(sink_ref, q_ref, k_ref, v_ref, o_ref):
    _attend(q_ref[...], k_ref[...], v_ref[...], None, sink_ref, o_ref)


def _context_attention(sink, q, k, v):
    return pl.pallas_call(
        _ctx_attn_kernel,
        grid_spec=pltpu.PrefetchScalarGridSpec(
            num_scalar_prefetch=1,
            grid=(BATCH,),
            in_specs=[pl.BlockSpec((SEQ, ATTN_WIDTH), lambda b, s: (b, 0)),
                      pl.BlockSpec((SEQ, KV_WIDTH), lambda b, s: (b, 0)),
                      pl.BlockSpec((SEQ, KV_WIDTH), lambda b, s: (b, 0))],
            out_specs=pl.BlockSpec((SEQ, ATTN_WIDTH), lambda b, s: (b, 0)),
        ),
        out_shape=jax.ShapeDtypeStruct((T_CTX, ATTN_WIDTH), F32),
        compiler_params=_params(("arbitrary",)),
        name="context_attention",
    )(sink, q, k, v)


def _rope_block(x, cos, sin_signed):
    lane = lax.broadcasted_iota(jnp.int32, (1, LANES), 1)
    first = (lane % (HEAD_DIM // 2)) < (HEAD_DIM // 4)
    swapped = jnp.where(first, pltpu.roll(x, LANES - HEAD_DIM // 4, 1), pltpu.roll(x, HEAD_DIM // 4, 1))
    return x * cos + swapped * sin_signed


LOCAL_KEYS = 3 * BLOCK


def _lat_attn_kernel(sink_ref, q_ref, k_ref, v_ref, ck_ref, cv_ref, cosq_ref, sinq_ref, cosk_ref, sin_k_ref,
                     o_ref):
    n = pl.program_id(1)
    start = pl.multiple_of(jnp.clip((n - 1) * BLOCK, 0, DEC_SEQ - LOCAL_KEYS), BLOCK)
    q = q_ref[...]
    q = jnp.concatenate(
        [_rope_block(q[:, j * LANES:(j + 1) * LANES], cosq_ref[:, j * LANES:(j + 1) * LANES],
                     sinq_ref[:, j * LANES:(j + 1) * LANES]) for j in range(ATTN_WIDTH // LANES)], axis=1)
    kw = _rope_block(k_ref[pl.ds(start, LOCAL_KEYS), :], cosk_ref[pl.ds(start, LOCAL_KEYS), :],
                     sin_k_ref[pl.ds(start, LOCAL_KEYS), :])
    vw = v_ref[pl.ds(start, LOCAL_KEYS), :]
    kall = jnp.concatenate([kw, ck_ref[0, 0]], axis=0)
    vall = jnp.concatenate([vw, cv_ref[0, 0]], axis=0)

    def valid_of(qrow, col):
        return jnp.logical_or(col >= LOCAL_KEYS, jnp.abs(n * BLOCK + qrow - (start + col)) <= WINDOW)

    _attend(q, kall, vall, valid_of, sink_ref, o_ref)


def _latent_attention(sink, q, k, v, cache_k, cache_v, layer, cosq, sinq, cosk, sin_k):
    nb = DEC_SEQ // BLOCK
    ctx_block0 = T_CTX // BLOCK
    ctx_seq0 = T_CTX // DEC_SEQ
    cache_spec = pl.BlockSpec((1, 1, PAST_LEN, KV_WIDTH), lambda b, n, s: (b, layer, 0, 0))
    return pl.pallas_call(
        _lat_attn_kernel,
        grid_spec=pltpu.PrefetchScalarGridSpec(
            num_scalar_prefetch=1,
            grid=(DEC_BATCH, nb),
            in_specs=[pl.BlockSpec((BLOCK, ATTN_WIDTH), lambda b, n, s: (ctx_block0 + b * nb + n, 0)),
                      pl.BlockSpec((DEC_SEQ, KV_WIDTH), lambda b, n, s: (ctx_seq0 + b, 0)),
                      pl.BlockSpec((DEC_SEQ, KV_WIDTH), lambda b, n, s: (ctx_seq0 + b, 0)),
                      cache_spec, cache_spec,
                      pl.BlockSpec((BLOCK, ATTN_WIDTH), lambda b, n, s: (n, 0)),
                      pl.BlockSpec((BLOCK, ATTN_WIDTH), lambda b, n, s: (n, 0)),
                      pl.BlockSpec((DEC_SEQ, KV_WIDTH), lambda b, n, s: (0, 0)),
                      pl.BlockSpec((DEC_SEQ, KV_WIDTH), lambda b, n, s: (0, 0))],
            out_specs=pl.BlockSpec((BLOCK, ATTN_WIDTH), lambda b, n, s: (b * nb + n, 0)),
        ),
        out_shape=jax.ShapeDtypeStruct((T_LAT, ATTN_WIDTH), F32),
        compiler_params=_params(("arbitrary", "arbitrary")),
        name="latent_attention",
    )(sink, q, k, v, cache_k, cache_v, cosq, sinq, cosk, sin_k)


def _rope_tables():
    t = jnp.arange(DEC_SEQ)
    nf = HEAD_DIM // 4
    inv = ROPE_BASE ** (-jnp.arange(nf, dtype=F32) / nf)

    def half(coord):
        ang = coord.astype(F32)[:, None] * inv[None, :]
        c, s = jnp.cos(ang), jnp.sin(ang)
        return jnp.concatenate([c, c], axis=1), jnp.concatenate([-s, s], axis=1)

    cr, sr = half(t // GRID_W)
    cc, sc = half(t % GRID_W)
    cos = jnp.concatenate([cr, cc], axis=1)
    sin = jnp.concatenate([sr, sc], axis=1)
    return (jnp.tile(cos, (1, N_Q_HEADS)), jnp.tile(sin, (1, N_Q_HEADS)),
            jnp.tile(cos, (1, N_KV_HEADS)), jnp.tile(sin, (1, N_KV_HEADS)))


def _ret_kernel(lg_ref, cd_ref, q_ref, k_ref, v_ref, sg_ref, gn_ref, *rest, n_chunks, has_s0, write_state):
    rest = list(rest)
    s0_ref = rest.pop(0) if has_s0 else None
    o_ref = rest.pop(0)
    sf_ref = rest.pop(0) if write_state else None
    acc_ref = rest.pop(0)
    row = lax.broadcasted_iota(jnp.int32, (CHUNK, CHUNK), 0).astype(F32)
    col = lax.broadcasted_iota(jnp.int32, (CHUNK, CHUNK), 1).astype(F32)
    rel = row - col
    pos = lax.broadcasted_iota(jnp.int32, (CHUNK, 1), 0).astype(F32)

    def run(direction, h):
        cols = slice(h * RET_DK, (h + 1) * RET_DK)
        gn = gn_ref[:, cols]
        lg = lg_ref[direction * N_RET_HEADS + h]
        cd = cd_ref[direction * N_RET_HEADS + h]
        if direction == 0:
            intra = jnp.where(rel >= 0, jnp.exp(lg * rel), 0.0)
            q_dec = jnp.exp(lg * (pos + 1.0))
            k_dec = jnp.exp(lg * (CHUNK - 1.0 - pos))
            order = range(n_chunks)
        else:
            intra = jnp.where(rel <= 0, jnp.exp(-lg * rel), 0.0)
            q_dec = jnp.exp(lg * (CHUNK - pos))
            k_dec = jnp.exp(lg * pos)
            order = range(n_chunks - 1, -1, -1)
        state = s0_ref[0, 0, direction, h] if has_s0 else jnp.zeros((RET_DK, RET_DK), F32)
        for c in order:
            rows = slice(c * CHUNK, (c + 1) * CHUNK)
            qc, kc, vc = q_ref[rows, cols], k_ref[rows, cols], v_ref[rows, cols]
            qb, kb, vb = qc.astype(BF16), kc.astype(BF16), vc.astype(BF16)
            scores = lax.dot_general(qb, kb, (((1,), (1,)), ((), ())), preferred_element_type=F32) * intra
            o = (jnp.dot(scores.astype(BF16), vb, preferred_element_type=F32)
                 + jnp.dot(qb, state.astype(BF16), preferred_element_type=F32) * q_dec)
            kd_t = (kc * k_dec).T.astype(BF16)
            state = state * cd + jnp.dot(kd_t, vb, preferred_element_type=F32)
            if direction == 0:
                acc_ref[rows, cols] = o
            else:
                tot = acc_ref[rows, cols] + o
                o_ref[rows, cols] = _rms_rows(tot, gn) * sg_ref[rows, cols]
        if write_state:
            sf_ref[0, direction, h] = state

    for direction in range(2):
        for h in range(N_RET_HEADS):
            run(direction, h)


def _retention(lg, cd, rq, rk, rv, sg, gn, s0, layer, *, n_seq, seq_len, row0, write_state):
    blk0 = row0 // seq_len
    rows = pl.BlockSpec((seq_len, RET_WIDTH), lambda b, *_: (blk0 + b, 0))
    in_specs = [rows, rows, rows, rows, pl.BlockSpec((1, RET_WIDTH), lambda b, *_: (0, 0))]
    args = [rq, rk, rv, sg, gn]
    if s0 is not None:
        in_specs.append(pl.BlockSpec((1, 1, 2, N_RET_HEADS, RET_DK, RET_DK), lambda b, *_: (b, layer, 0, 0, 0, 0)))
        args.append(s0)
    out_specs = [pl.BlockSpec((seq_len, RET_WIDTH), lambda b, *_: (b, 0))]
    out_shape = [jax.ShapeDtypeStruct((n_seq * seq_len, RET_WIDTH), F32)]
    if write_state:
        out_specs.append(pl.BlockSpec((1, 2, N_RET_HEADS, RET_DK, RET_DK), lambda b, *_: (b, 0, 0, 0, 0)))
        out_shape.append(jax.ShapeDtypeStruct((n_seq, 2, N_RET_HEADS, RET_DK, RET_DK), F32))
    kern = functools.partial(_ret_kernel, n_chunks=seq_len // CHUNK, has_s0=s0 is not None,
                             write_state=write_state)
    return pl.pallas_call(
        kern,
        grid_spec=pltpu.PrefetchScalarGridSpec(
            num_scalar_prefetch=2,
            grid=(n_seq,),
            in_specs=in_specs,
            out_specs=out_specs,
            scratch_shapes=[pltpu.VMEM((seq_len, RET_WIDTH), F32)],
        ),
        out_shape=out_shape,
        compiler_params=_params(("arbitrary",)),
        name="retention_ctx" if write_state else "retention_lat",
    )(lg, cd, *args)


def _outproj_kernel(attc_ref, attl_ref, retc_ref, retl_ref, x_ref, wo_ref, g1_ref, sh2_ref, sc2_ref, nf_ref,
                    wrh_ref, wrl_ref, br_ref, x1_ref, h2t_ref, ti_ref, tw_ref):
    is_ctx = pl.program_id(0) < T_CTX // ROW_TILE
    att = jnp.where(is_ctx, attc_ref[...], attl_ref[...])
    ret = jnp.where(is_ctx, retc_ref[...], retl_ref[...])
    y = (jnp.dot(att.astype(BF16), wo_ref[0:ATTN_WIDTH, :], preferred_element_type=F32)
         + jnp.dot(ret.astype(BF16), wo_ref[ATTN_WIDTH:, :], preferred_element_type=F32))
    x1 = x_ref[...] + g1_ref[0] * y
    x1_ref[...] = x1
    h2 = _rms_rows(x1, nf_ref[...]) * (1.0 + sc2_ref[0]) + sh2_ref[0]
    for s in range(SLABS):
        h2t_ref[pl.ds(s, ROW_TILE, stride=SLABS), :] = h2[:, s * LANES:(s + 1) * LANES]
    hh = h2.astype(BF16)
    hl = (h2 - hh.astype(F32)).astype(BF16)
    wrh = wrh_ref[...]
    logits = (jnp.dot(hh, wrh, preferred_element_type=F32) + jnp.dot(hl, wrh, preferred_element_type=F32)
              + jnp.dot(hh, wrl_ref[...], preferred_element_type=F32) + br_ref[...])
    lane = lax.broadcasted_iota(jnp.int32, logits.shape, 1)
    vals, idxs = [], []
    cur = logits
    for _ in range(TOP_K):
        m = jnp.max(cur, axis=-1, keepdims=True)
        idx = jnp.min(jnp.where(cur == m, lane, LANES), axis=-1, keepdims=True)
        vals.append(m)
        idxs.append(idx)
        cur = jnp.where(lane == idx, -jnp.inf, cur)
    es = [jnp.exp(v - vals[0]) for v in vals]
    den = es[0] + es[1] + es[2] + es[3]
    ti = jnp.zeros(logits.shape, jnp.int32)
    tw = jnp.zeros(logits.shape, F32)
    for k in range(TOP_K):
        ti = jnp.where(lane == k, idxs[k], ti)
        tw = jnp.where(lane == k, es[k] / den, tw)
    ti_ref[...] = ti
    tw_ref[...] = tw


def _output_projection(att_c, att_l, ret_c, ret_l, x, w_out_bf, mod_l, nf, wrh, wrl, br):
    n_ctx = T_CTX // ROW_TILE
    rows = lambda w: pl.BlockSpec((ROW_TILE, w), lambda i: (i, 0))
    ctx_rows = lambda w: pl.BlockSpec((ROW_TILE, w), lambda i: (jnp.minimum(i, n_ctx - 1), 0))
    lat_rows = lambda w: pl.BlockSpec((ROW_TILE, w), lambda i: (jnp.maximum(i - n_ctx, 0), 0))
    return pl.pallas_call(
        _outproj_kernel,
        grid=(T_ALL // ROW_TILE,),
        in_specs=[ctx_rows(ATTN_WIDTH), lat_rows(ATTN_WIDTH), ctx_rows(RET_WIDTH), lat_rows(RET_WIDTH),
                  rows(D_MODEL), _full((D_MODEL, D_MODEL)),
                  _mod_spec(2), _mod_spec(3), _mod_spec(4), _full((1, D_MODEL)),
                  _full((D_MODEL, LANES)), _full((D_MODEL, LANES)), _full((1, LANES))],
        out_specs=[rows(D_MODEL), pl.BlockSpec((ROW_TILE * SLABS, LANES), lambda i: (i, 0)), rows(LANES), rows(LANES)],
        out_shape=[jax.ShapeDtypeStruct((T_ALL, D_MODEL), F32), jax.ShapeDtypeStruct((T_ALL * SLABS, LANES), F32),
                   jax.ShapeDtypeStruct((T_ALL, LANES), jnp.int32), jax.ShapeDtypeStruct((T_ALL, LANES), F32)],
        compiler_params=_params(("arbitrary",)),
        name="outproj_router",
    )(att_c, att_l, ret_c, ret_l, x, w_out_bf, mod_l, mod_l, mod_l, nf, wrh, wrl, br)


def _token_tile(ref, t):
    return ref.at[pl.ds(pl.multiple_of(t * SLABS, SLABS), SLABS), :]


def _step_is_used(i, nused_ref):
    return i * MOE_STEP_BLOCKS < nused_ref[0]


def _dispatch_kernel(nused_ref, src_ref, h2t_hbm, o_ref, hres, xg, sem):
    i = pl.program_id(0)

    @pl.when(i == 0)
    def _():
        cp = pltpu.make_async_copy(h2t_hbm, hres.at[pl.ds(0, T_ALL * SLABS), :], sem.at[0])
        cp.start()
        _token_tile(hres, T_ALL)[...] = jnp.zeros((SLABS, LANES), F32)
        cp.wait()

    @pl.when(_step_is_used(i, nused_ref))
    def _():
        def body(r, carry):
            _token_tile(xg, r)[...] = _token_tile(hres, src_ref[0, 0, r])[...]
            return carry

        lax.fori_loop(0, MOE_STEP_ROWS, body, 0, unroll=8)
        for s in range(SLABS):
            o_ref[:, s * LANES:(s + 1) * LANES] = xg[pl.ds(s, MOE_STEP_ROWS, stride=SLABS), :].astype(BF16)

    @pl.when(jnp.logical_not(_step_is_used(i, nused_ref)))
    def _():
        o_ref[...] = jnp.zeros(o_ref.shape, BF16)


def _smem_rows(width):
    return pl.BlockSpec((1, 1, width), lambda i, nused: (jnp.minimum(i, N_MOE_STEPS - 1), 0, 0),
                        memory_space=pltpu.SMEM)


def _moe_dispatch(n_used, src3, h2t):
    return pl.pallas_call(
        _dispatch_kernel,
        grid_spec=pltpu.PrefetchScalarGridSpec(
            num_scalar_prefetch=1,
            grid=(N_MOE_STEPS,),
            in_specs=[_smem_rows(MOE_STEP_ROWS), pl.BlockSpec(memory_space=pl.ANY)],
            out_specs=pl.BlockSpec((MOE_STEP_ROWS, D_MODEL), lambda i, nused: (i, 0)),
            scratch_shapes=[pltpu.VMEM((N_TILES * SLABS, LANES), F32), pltpu.VMEM((MOE_STEP_ROWS * SLABS, LANES), F32),
                            pltpu.SemaphoreType.DMA((1,))],
        ),
        out_shape=jax.ShapeDtypeStruct((N_PAD, D_MODEL), BF16),
        compiler_params=_params(("arbitrary",)),
        name="moe_dispatch",
    )(n_used, src3, h2t)


X_SLOTS = 8
X_AHEAD = 6
Y_SLOTS = 4


def _block_rows(ref, g):
    return ref.at[pl.ds(pl.multiple_of(g * MOE_ROWS, MOE_ROWS), MOE_ROWS), :]


TILE_ROWS = MOE_ROWS * SLABS


def _tile_block(ref, g):
    return ref.at[pl.ds(pl.multiple_of(g * TILE_ROWS, TILE_ROWS), TILE_ROWS), :]


def _experts_kernel(blk0_ref, nblk_ref, nused_ref, src_ref, h2t_hbm, wgu_ref, bgu_ref, wdn_ref, bdn_ref, yst_hbm,
                    xbuf, ybuf, wgu_bf, wdn_bf, xsem, ysem):
    e = pl.program_id(0)
    b0, nb, nused = blk0_ref[e], nblk_ref[e], nused_ref[0]

    def row_copy(g, slot, r):
        tok = jnp.minimum(src_ref[g * MOE_ROWS + r], T_ALL - 1)
        dst = xbuf.at[pl.ds(pl.multiple_of(slot * TILE_ROWS + r * SLABS, SLABS), SLABS), :]
        return pltpu.make_async_copy(_token_tile(h2t_hbm, tok), dst, xsem.at[slot])

    def gather_wait(slot):
        pltpu.make_async_copy(_tile_block(h2t_hbm, 0), _tile_block(xbuf, slot), xsem.at[slot]).wait()

    def y_copy(g):
        slot = g % Y_SLOTS
        return pltpu.make_async_copy(_tile_block(ybuf, slot), _tile_block(yst_hbm, g), ysem.at[slot])

    @pl.when(e == 0)
    def _():
        for k in range(X_AHEAD):
            @pl.when(k < nused)
            def _():
                def issue(r, carry):
                    row_copy(k, k, r).start()
                    return carry
                lax.fori_loop(0, MOE_ROWS, issue, 0)

    @pl.when(nb > 0)
    def _():
        wgu_bf[...] = wgu_ref[...].astype(BF16)
        wdn_bf[...] = wdn_ref[...].astype(BF16)

        def process(blocks):
            xs = []
            for g in blocks:
                slot = g % X_SLOTS
                gather_wait(slot)
                xs.append(jnp.concatenate(
                    [xbuf[pl.ds(slot * TILE_ROWS + s, MOE_ROWS, stride=SLABS), :].astype(BF16)
                     for s in range(SLABS)], axis=1))

            for g in blocks:
                ahead = g + X_AHEAD
                ahead_slot = jnp.where(ahead < nused, ahead % X_SLOTS, X_SLOTS + ahead - nused)
                ahead_blk = jnp.minimum(ahead, nused - 1)
                for r in range(MOE_ROWS):
                    row_copy(ahead_blk, ahead_slot, r).start()

            outs = []
            for x in xs:
                gu = jnp.dot(x, wgu_bf[...], preferred_element_type=F32) + bgu_ref[...]
                x_glu = jnp.minimum(gu[:, :D_FF], SWIGLU_LIMIT)
                x_lin = jnp.clip(gu[:, D_FF:], -SWIGLU_LIMIT, SWIGLU_LIMIT)
                act = x_glu * jax.nn.sigmoid(SWIGLU_ALPHA * x_glu) * (x_lin + 1.0)
                outs.append(jnp.dot(act.astype(BF16), wdn_bf[...], preferred_element_type=F32) + bdn_ref[...])

            for g in blocks:
                @pl.when(g >= Y_SLOTS)
                def _():
                    y_copy(g - Y_SLOTS).wait()

            for g, out in zip(blocks, outs):
                yslot = g % Y_SLOTS
                for s in range(SLABS):
                    ybuf[pl.ds(yslot * TILE_ROWS + s, MOE_ROWS, stride=SLABS), :] = out[:, s * LANES:(s + 1) * LANES]
                y_copy(g).start()

        def one(j, carry):
            process([b0 + j])
            return carry

        lax.fori_loop(0, nb, one, 0)

    @pl.when(e == N_EXPERTS - 1)
    def _():
        for k in range(X_AHEAD):
            @pl.when(k < nused)
            def _():
                gather_wait(X_SLOTS + X_AHEAD - 1 - k)

        for k in range(1, Y_SLOTS + 1):
            @pl.when(nused >= k)
            def _():
                y_copy(nused - k).wait()

        _tile_block(ybuf, 0)[...] = jnp.zeros((TILE_ROWS, LANES), F32)

        def fill(g, carry):
            cp = pltpu.make_async_copy(_tile_block(ybuf, 0), _tile_block(yst_hbm, g), ysem.at[0])
            cp.start()
            cp.wait()
            return carry

        lax.fori_loop(nused, N_MOE_BLOCKS, fill, 0)


def _moe_experts(blk0, nblk, n_used, src, h2t, w_gu, b_gu, w_dn, b_dn, layer):
    wspec = lambda rows, cols: pl.BlockSpec((None, None, rows, cols), lambda e, *_: (layer, e, 0, 0))
    return pl.pallas_call(
        _experts_kernel,
        grid_spec=pltpu.PrefetchScalarGridSpec(
            num_scalar_prefetch=4,
            grid=(N_EXPERTS,),
            in_specs=[pl.BlockSpec(memory_space=pl.ANY),
                      wspec(D_MODEL, 2 * D_FF), wspec(1, 2 * D_FF), wspec(D_FF, D_MODEL), wspec(1, D_MODEL)],
            out_specs=pl.BlockSpec(memory_space=pl.ANY),
            scratch_shapes=[pltpu.VMEM(((X_SLOTS + X_AHEAD) * TILE_ROWS, LANES), F32),
                            pltpu.VMEM((Y_SLOTS * TILE_ROWS, LANES), F32),
                            pltpu.VMEM((D_MODEL, 2 * D_FF), BF16), pltpu.VMEM((D_FF, D_MODEL), BF16),
                            pltpu.SemaphoreType.DMA((X_SLOTS + X_AHEAD,)), pltpu.SemaphoreType.DMA((Y_SLOTS,))],
        ),
        out_shape=jax.ShapeDtypeStruct((N_PAD * SLABS, LANES), F32),
        compiler_params=_params(("arbitrary",), vmem=EXPERT_VMEM_LIMIT),
        name="moe_experts",
    )(blk0, nblk, n_used, src, h2t, w_gu, b_gu.reshape(DEPTH, N_EXPERTS, 1, 2 * D_FF),
      w_dn, b_dn.reshape(DEPTH, N_EXPERTS, 1, D_MODEL))


N_ROW_TILES = T_ALL // ROW_TILE
ZERO_ROWS = ROW_TILE * SLABS


def _combine_kernel(nused_ref, src_ref, w_ref, ys_ref, x1_ref, g2_ref, o_ref, yres):
    i = pl.program_id(0)

    @pl.when(i == 0)
    def _():
        def zero(j, carry):
            yres[pl.ds(pl.multiple_of(j * ZERO_ROWS, ZERO_ROWS), ZERO_ROWS), :] = jnp.zeros((ZERO_ROWS, LANES), F32)
            return carry
        lax.fori_loop(0, N_ROW_TILES, zero, 0)
        _token_tile(yres, T_ALL)[...] = jnp.zeros((SLABS, LANES), F32)

    @pl.when(jnp.logical_and(i < N_MOE_STEPS, _step_is_used(i, nused_ref)))
    def _():
        def group(g, carry):
            rows = [g * GROUP + j for j in range(GROUP)]
            toks = [src_ref[0, 0, r] for r in rows]
            ws = [w_ref[0, 0, r] for r in rows]
            new = [_token_tile(yres, t)[...] + w * _token_tile(ys_ref, r)[...] for t, w, r in zip(toks, ws, rows)]
            for t, v in zip(toks, new):
                _token_tile(yres, t)[...] = v
            return carry

        lax.fori_loop(0, MOE_STEP_ROWS // GROUP, group, 0)

    @pl.when(i >= N_MOE_STEPS)
    def _():
        base = (i - N_MOE_STEPS) * ZERO_ROWS
        for s in range(SLABS):
            cols = slice(s * LANES, (s + 1) * LANES)
            y = yres[pl.ds(base + s, ROW_TILE, stride=SLABS), :]
            o_ref[:, cols] = x1_ref[:, cols] + g2_ref[0][:, cols] * y


def _moe_combine(n_used, src3, w3, ys, x1, mod_l):
    tile = lambda i: jnp.maximum(i - N_MOE_STEPS, 0)
    last_used = lambda nused: jnp.maximum(nused[0] - 1, 0) // MOE_STEP_BLOCKS
    return pl.pallas_call(
        _combine_kernel,
        grid_spec=pltpu.PrefetchScalarGridSpec(
            num_scalar_prefetch=1,
            grid=(N_MOE_STEPS + N_ROW_TILES,),
            in_specs=[_smem_rows(MOE_STEP_ROWS), _smem_rows(MOE_STEP_ROWS),
                      pl.BlockSpec((MOE_STEP_ROWS * SLABS, LANES), lambda i, nused: (jnp.minimum(i, last_used(nused)), 0)),
                      pl.BlockSpec((ROW_TILE, D_MODEL), lambda i, nused: (tile(i), 0)),
                      pl.BlockSpec((1, 1, D_MODEL), lambda i, nused: (_cond_of_tile(tile(i)), 0, 5))],
            out_specs=pl.BlockSpec((ROW_TILE, D_MODEL), lambda i, nused: (tile(i), 0)),
            scratch_shapes=[pltpu.VMEM((N_TILES * SLABS, LANES), F32)],
        ),
        out_shape=jax.ShapeDtypeStruct((T_ALL, D_MODEL), F32),
        compiler_params=_params(("arbitrary",)),
        name="moe_combine",
    )(n_used, src3, w3, ys, x1, mod_l)


def _routing_tables(top_idx, top_w):
    flat_e = top_idx.reshape(N_ASSIGN)
    experts = jnp.arange(N_EXPERTS, dtype=jnp.int32)
    counts = jnp.sum((flat_e[:, None] == experts[None, :]).astype(jnp.int32), axis=0)
    pad = (-counts) % MOE_ROWS
    spare = jnp.arange(MOE_ROWS, dtype=jnp.int32)
    pad_keys = jnp.where(spare[None, :] < pad[:, None], experts[:, None], N_EXPERTS).reshape(-1)
    keys = jnp.concatenate([flat_e, pad_keys])
    toks = jnp.concatenate([jnp.arange(N_ASSIGN, dtype=jnp.int32) // TOP_K,
                            jnp.full((N_PAD - N_ASSIGN,), T_ALL, jnp.int32)])
    wts = jnp.concatenate([top_w.reshape(N_ASSIGN), jnp.zeros((N_PAD - N_ASSIGN,), F32)])
    _, src, w_sorted = lax.sort((keys, toks, wts), num_keys=1, is_stable=True)
    padded = counts + pad
    blk_end = jnp.cumsum(padded) // MOE_ROWS
    nblk = padded // MOE_ROWS
    return (src.reshape(N_MOE_STEPS, 1, MOE_STEP_ROWS), w_sorted.reshape(N_MOE_STEPS, 1, MOE_STEP_ROWS),
            (blk_end - nblk).astype(jnp.int32), nblk.astype(jnp.int32), blk_end[-1:].astype(jnp.int32))


def kernel(x_prompt, x_sample, cache_attn_k, cache_attn_v, state_ret, c, c_ctx, norm_mix, norm_ffn, w_ada, b_ada,
           w_in, q_norm, k_norm, attn_sink, ret_decay, ret_norm, w_out, w_router, b_router, w_gate_up, b_gate_up,
           w_down, b_down):
    x = jnp.concatenate([x_prompt.reshape(T_CTX, D_MODEL), x_sample.reshape(T_LAT, D_MODEL)], axis=0)
    cond = jnp.zeros((COND_PAD, D_MODEL), F32).at[0].set(c_ctx).at[1:N_COND].set(c)
    mod = _modulation(cond, w_ada, b_ada)[:, :N_COND].reshape(DEPTH, N_COND, 1, 6 * D_MODEL)

    cache_k = cache_attn_k.reshape(DEC_BATCH, DEPTH, PAST_LEN, KV_WIDTH)
    cache_v = cache_attn_v.reshape(DEC_BATCH, DEPTH, PAST_LEN, KV_WIDTH)
    cosq, sinq, cosk, sin_k = _rope_tables()
    grp = jnp.arange(ATTN_WIDTH) // HEAD_DIM
    avgq = jnp.where(grp[:, None] == grp[None, :], 1.0 / HEAD_DIM, 0.0).astype(BF16)
    avgk = avgq[:KV_WIDTH, :KV_WIDTH]
    log_gamma = jax.nn.log_sigmoid(ret_decay.astype(F32))
    chunk_decay = jnp.exp(log_gamma * CHUNK)

    new_k, new_v, new_s = [], [], []
    for l in range(DEPTH):
        mod_l = mod[l]
        q, k, v, rq, rk, rv, sg = _input_projection(
            x, norm_mix[l].reshape(1, D_MODEL), mod_l, w_in[l].astype(BF16),
            jnp.tile(q_norm[l], N_Q_HEADS).reshape(1, ATTN_WIDTH), jnp.tile(k_norm[l], N_KV_HEADS).reshape(1, KV_WIDTH),
            avgq, avgk)
        new_k.append(k[:T_CTX].reshape(BATCH, SEQ, N_KV_HEADS, HEAD_DIM))
        new_v.append(v[:T_CTX].reshape(BATCH, SEQ, N_KV_HEADS, HEAD_DIM))
        sink = attn_sink[l].astype(F32)
        att_c = _context_attention(sink, q, k, v)
        att_l = _latent_attention(sink, q, k, v, cache_k, cache_v, l, cosq, sinq, cosk, sin_k)
        lg = log_gamma[l].reshape(2 * N_RET_HEADS)
        cd = chunk_decay[l].reshape(2 * N_RET_HEADS)
        gn = ret_norm[l].reshape(1, RET_WIDTH)
        ret_c, s_fin = _retention(lg, cd, rq, rk, rv, sg, gn, None, l, n_seq=BATCH, seq_len=SEQ, row0=0,
                                  write_state=True)
        (ret_l,) = _retention(lg, cd, rq, rk, rv, sg, gn, state_ret, l, n_seq=DEC_BATCH, seq_len=DEC_SEQ,
                              row0=T_CTX, write_state=False)
        new_s.append(s_fin)
        wr = jnp.zeros((D_MODEL, LANES), F32).at[:, :N_EXPERTS].set(w_router[l])
        wrh = wr.astype(BF16)
        wrl = (wr - wrh.astype(F32)).astype(BF16)
        br = jnp.full((1, LANES), NEG_BIG, F32).at[0, :N_EXPERTS].set(b_router[l])
        x1, h2t, ti, tw = _output_projection(att_c, att_l, ret_c, ret_l, x, w_out[l].astype(BF16), mod_l,
                                             norm_ffn[l].reshape(1, D_MODEL), wrh, wrl, br)
        src3, w3, blk0, nblk, n_used = _routing_tables(ti[:, :TOP_K], tw[:, :TOP_K])
        ys = _moe_experts(blk0, nblk, n_used, src3.reshape(N_PAD), h2t, w_gate_up, b_gate_up, w_down, b_down, l)
        x = _moe_combine(n_used, src3, w3, ys, x1, mod_l)

    y_prompt = x[:T_CTX].reshape(BATCH, SEQ, D_MODEL)
    y_sample = x[T_CTX:].reshape(DEC_BATCH, DEC_SEQ, D_MODEL)
    return (y_prompt, y_sample, jnp.stack(new_k, axis=1), jnp.stack(new_v, axis=1), jnp.stack(new_s, axis=1))
```

```python
import functools

import jax
import jax.numpy as jnp
from jax import lax
from jax.experimental import pallas as pl
from jax.experimental.pallas import tpu as pltpu

F32 = jnp.float32
BF16 = jnp.bfloat16

D_MODEL = 1024
DEPTH = 4
BATCH, SEQ = 16, 256
DEC_BATCH, DEC_SEQ = 2, 1024
PAST_LEN = 512
GRID_W = 64
HEAD_DIM = 64
N_Q_HEADS = 8
N_KV_HEADS = 2
ATTN_WIDTH = N_Q_HEADS * HEAD_DIM
KV_WIDTH = N_KV_HEADS * HEAD_DIM
WINDOW = 128
BLOCK = 128
ROPE_BASE = 10000.0
N_RET_HEADS = 4
RET_DK = 128
RET_WIDTH = N_RET_HEADS * RET_DK
RET_CHUNK = 256
IN_WIDTH = ATTN_WIDTH + 2 * KV_WIDTH + 4 * RET_WIDTH
N_EXPERTS = 32
TOP_K = 4
D_FF = D_MODEL
SWIGLU_LIMIT = 7.0
SWIGLU_ALPHA = 1.702
EPS = 1e-6

T_CTX = BATCH * SEQ
T_LAT = DEC_BATCH * DEC_SEQ
T_ALL = T_CTX + T_LAT
N_COND = 1 + DEC_BATCH
COND_PAD = 8
LANES = 128
ROW_TILE = 512
MOE_ROWS = 128
N_ASSIGN = T_ALL * TOP_K
N_MOE_BLOCKS = N_ASSIGN // MOE_ROWS + N_EXPERTS
N_PAD = N_MOE_BLOCKS * MOE_ROWS
MOE_STEP_BLOCKS = 4
MOE_STEP_ROWS = MOE_STEP_BLOCKS * MOE_ROWS
N_MOE_STEPS = N_MOE_BLOCKS // MOE_STEP_BLOCKS
SLABS = D_MODEL // LANES
GROUP = 8
N_TILES = T_ALL + 1
NEG_BIG = -1e30
VMEM_LIMIT = 48 * 1024 * 1024
EXPERT_VMEM_LIMIT = 56 * 1024 * 1024


def _cond_of_tile(i):
    return jnp.where(i < T_CTX // ROW_TILE, 0, 1 + (i - T_CTX // ROW_TILE) // (DEC_SEQ // ROW_TILE))


def _params(sem, vmem=VMEM_LIMIT):
    return pltpu.CompilerParams(dimension_semantics=sem, vmem_limit_bytes=vmem)


def _mod_kernel(c_ref, w_ref, b_ref, o_ref):
    c = c_ref[...]
    s = (c * jax.nn.sigmoid(c)).astype(BF16)
    o_ref[0] = jnp.dot(s, w_ref[0].astype(BF16), preferred_element_type=F32) + b_ref[0]


def _modulation(cond, w_ada, b_ada):
    n_col = 6 * D_MODEL // D_MODEL
    return pl.pallas_call(
        _mod_kernel,
        grid=(DEPTH, n_col),
        in_specs=[
            pl.BlockSpec((COND_PAD, D_MODEL), lambda l, j: (0, 0)),
            pl.BlockSpec((1, D_MODEL, D_MODEL), lambda l, j: (l, 0, j)),
            pl.BlockSpec((1, 1, D_MODEL), lambda l, j: (l, 0, j)),
        ],
        out_specs=pl.BlockSpec((1, COND_PAD, D_MODEL), lambda l, j: (l, 0, j)),
        out_shape=jax.ShapeDtypeStruct((DEPTH, COND_PAD, 6 * D_MODEL), F32),
        compiler_params=_params(("arbitrary", "arbitrary")),
        name="modulation",
    )(cond, w_ada, b_ada.reshape(DEPTH, 1, 6 * D_MODEL))


def _rms_rows(x, g):
    ms = jnp.mean(x * x, axis=-1, keepdims=True)
    return x * lax.rsqrt(ms + EPS) * g


def _group_rmsnorm(a, avg_ref, g):
    ms = jnp.dot((a * a).astype(BF16), avg_ref[...], preferred_element_type=F32)
    return a * lax.rsqrt(ms + EPS) * g


def _inproj_kernel(x_ref, g_ref, sh_ref, sc_ref, w_ref, qn_ref, kn_ref, avgq_ref, avgk_ref,
                   q_ref, k_ref, v_ref, rq_ref, rk_ref, rv_ref, sg_ref):
    h = _rms_rows(x_ref[...], g_ref[...]) * (1.0 + sc_ref[0]) + sh_ref[0]
    hb = h.astype(BF16)

    def proj(lo, width):
        return jnp.dot(hb, w_ref[:, lo:lo + width], preferred_element_type=F32)

    o = 0
    q_ref[...] = _group_rmsnorm(proj(o, ATTN_WIDTH), avgq_ref, qn_ref[...])
    o += ATTN_WIDTH
    k_ref[...] = _group_rmsnorm(proj(o, KV_WIDTH), avgk_ref, kn_ref[...])
    o += KV_WIDTH
    v_ref[...] = proj(o, KV_WIDTH)
    o += KV_WIDTH
    rq_ref[...] = proj(o, RET_WIDTH) * (RET_DK ** -0.5)
    o += RET_WIDTH
    rk_ref[...] = proj(o, RET_WIDTH)
    o += RET_WIDTH
    rv_ref[...] = proj(o, RET_WIDTH)
    o += RET_WIDTH
    rg = proj(o, RET_WIDTH)
    sg_ref[...] = rg * jax.nn.sigmoid(rg)


def _mod_spec(col):
    return pl.BlockSpec((1, 1, D_MODEL), lambda i, col=col: (_cond_of_tile(i), 0, col))


def _full(shape):
    return pl.BlockSpec(shape, lambda *_: (0,) * len(shape))


def _input_projection(x, norm_g, mod_l, w_in_bf, qn, kn, avgq, avgk):
    rows = lambda w: pl.BlockSpec((ROW_TILE, w), lambda i: (i, 0))
    widths = (ATTN_WIDTH, KV_WIDTH, KV_WIDTH, RET_WIDTH, RET_WIDTH, RET_WIDTH, RET_WIDTH)
    return pl.pallas_call(
        _inproj_kernel,
        grid=(T_ALL // ROW_TILE,),
        in_specs=[rows(D_MODEL), _full((1, D_MODEL)), _mod_spec(0), _mod_spec(1),
                  _full((D_MODEL, IN_WIDTH)), _full((1, ATTN_WIDTH)), _full((1, KV_WIDTH)),
                  _full((ATTN_WIDTH, ATTN_WIDTH)), _full((KV_WIDTH, KV_WIDTH))],
        out_specs=[rows(w) for w in widths],
        out_shape=[jax.ShapeDtypeStruct((T_ALL, w), F32) for w in widths],
        compiler_params=_params(("arbitrary",)),
        name="norm_inproj",
    )(x, norm_g, mod_l, mod_l, w_in_bf, qn, kn, avgq, avgk)


def _attend(q, kall, vall, valid_of, sink_ref, o_ref):
    m_rows, n_keys = q.shape[0], kall.shape[0]
    scale = HEAD_DIM ** -0.5
    lane = lax.broadcasted_iota(jnp.int32, (1, LANES), 1)
    low = lane < HEAD_DIM
    row = lax.broadcasted_iota(jnp.int32, (2 * m_rows, 1), 0)
    second = row >= m_rows
    valid = None
    if valid_of is not None:
        qrow = lax.broadcasted_iota(jnp.int32, (2 * m_rows, n_keys), 0)
        qrow = jnp.where(qrow >= m_rows, qrow - m_rows, qrow)
        valid = valid_of(qrow, lax.broadcasted_iota(jnp.int32, (2 * m_rows, n_keys), 1))
    for g in range(N_KV_HEADS):
        keep = low if g == 0 else jnp.logical_not(low)
        kg = jnp.where(keep, kall, 0.0)
        vg = jnp.where(keep, vall, 0.0)
        kr = pltpu.roll(kg, HEAD_DIM, 1)
        vr = pltpu.roll(vg, HEAD_DIM, 1)
        k_at = (kg, kr) if g == 0 else (kr, kg)
        v_at = (vg, vr) if g == 0 else (vr, vg)
        kcat = jnp.concatenate(k_at, axis=0).astype(BF16)
        qg = jnp.concatenate([q[:, (2 * g + b) * LANES:(2 * g + b + 1) * LANES] for b in range(2)],
                             axis=0).astype(BF16)
        s_all = lax.dot_general(qg, kcat, (((1,), (1,)), ((), ())), preferred_element_type=F32) * scale
        acc = None
        for off in range(2):
            s = s_all[:, off * n_keys:(off + 1) * n_keys]
            if valid is not None:
                s = jnp.where(valid, s, NEG_BIG)
            h0 = 4 * g + off
            sink = jnp.where(second, sink_ref[h0 + 2], sink_ref[h0])
            m = jnp.maximum(jnp.max(s, axis=-1, keepdims=True), sink)
            e = jnp.exp(s - m)
            den = jnp.sum(e, axis=-1, keepdims=True) + jnp.exp(sink - m)
            o = jnp.dot(e.astype(BF16), v_at[off].astype(BF16), preferred_element_type=F32) / den
            acc = o if acc is None else acc + o
        for b in range(2):
            o_ref[:, (2 * g + b) * LANES:(2 * g + b + 1) * LANES] = acc[b * m_rows:(b + 1) * m_rows]


def _ctx_attn_kernel(sink_ref, q_ref, k_ref, v_ref, o_ref):
    _attend(q_ref[...], k_ref[...], v_ref[...], None, sink_ref, o_ref)


def _context_attention(sink, q, k, v):
    return pl.pallas_call(
        _ctx_attn_kernel,
        grid_spec=pltpu.PrefetchScalarGridSpec(
            num_scalar_prefetch=1,
            grid=(BATCH,),
            in_specs=[pl.BlockSpec((SEQ, ATTN_WIDTH), lambda b, s: (b, 0)),
                      pl.BlockSpec((SEQ, KV_WIDTH), lambda b, s: (b, 0)),
                      pl.BlockSpec((SEQ, KV_WIDTH), lambda b, s: (b, 0))],
            out_specs=pl.BlockSpec((SEQ, ATTN_WIDTH), lambda b, s: (b, 0)),
        ),
        out_shape=jax.ShapeDtypeStruct((T_CTX, ATTN_WIDTH), F32),
        compiler_params=_params(("arbitrary",)),
        name="context_attention",
    )(sink, q, k, v)


def _rope_block(x, cos, sin_signed):
    lane = lax.broadcasted_iota(jnp.int32, (1, LANES), 1)
    first = (lane % (HEAD_DIM // 2)) < (HEAD_DIM // 4)
    swapped = jnp.where(first, pltpu.roll(x, LANES - HEAD_DIM // 4, 1), pltpu.roll(x, HEAD_DIM // 4, 1))
    return x * cos + swapped * sin_signed


LOCAL_KEYS = 3 * BLOCK


def _lat_attn_kernel(sink_ref, q_ref, k_ref, v_ref, ck_ref, cv_ref, cosq_ref, sinq_ref, cosk_ref, sin_k_ref,
                     o_ref):
    n = pl.program_id(1)
    start = pl.multiple_of(jnp.clip((n - 1) * BLOCK, 0, DEC_SEQ - LOCAL_KEYS), BLOCK)
    q = q_ref[...]
    q = jnp.concatenate(
        [_rope_block(q[:, j * LANES:(j + 1) * LANES], cosq_ref[:, j * LANES:(j + 1) * LANES],
                     sinq_ref[:, j * LANES:(j + 1) * LANES]) for j in range(ATTN_WIDTH // LANES)], axis=1)
    kw = _rope_block(k_ref[pl.ds(start, LOCAL_KEYS), :], cosk_ref[pl.ds(start, LOCAL_KEYS), :],
                     sin_k_ref[pl.ds(start, LOCAL_KEYS), :])
    vw = v_ref[pl.ds(start, LOCAL_KEYS), :]
    kall = jnp.concatenate([kw, ck_ref[0, 0]], axis=0)
    vall = jnp.concatenate([vw, cv_ref[0, 0]], axis=0)

    def valid_of(qrow, col):
        return jnp.logical_or(col >= LOCAL_KEYS, jnp.abs(n * BLOCK + qrow - (start + col)) <= WINDOW)

    _attend(q, kall, vall, valid_of, sink_ref, o_ref)


def _latent_attention(sink, q, k, v, cache_k, cache_v, layer, cosq, sinq, cosk, sin_k):
    nb = DEC_SEQ // BLOCK
    ctx_block0 = T_CTX // BLOCK
    ctx_seq0 = T_CTX // DEC_SEQ
    cache_spec = pl.BlockSpec((1, 1, PAST_LEN, KV_WIDTH), lambda b, n, s: (b, layer, 0, 0))
    return pl.pallas_call(
        _lat_attn_kernel,
        grid_spec=pltpu.PrefetchScalarGridSpec(
            num_scalar_prefetch=1,
            grid=(DEC_BATCH, nb),
            in_specs=[pl.BlockSpec((BLOCK, ATTN_WIDTH), lambda b, n, s: (ctx_block0 + b * nb + n, 0)),
                      pl.BlockSpec((DEC_SEQ, KV_WIDTH), lambda b, n, s: (ctx_seq0 + b, 0)),
                      pl.BlockSpec((DEC_SEQ, KV_WIDTH), lambda b, n, s: (ctx_seq0 + b, 0)),
                      cache_spec, cache_spec,
                      pl.BlockSpec((BLOCK, ATTN_WIDTH), lambda b, n, s: (n, 0)),
                      pl.BlockSpec((BLOCK, ATTN_WIDTH), lambda b, n, s: (n, 0)),
                      pl.BlockSpec((DEC_SEQ, KV_WIDTH), lambda b, n, s: (0, 0)),
                      pl.BlockSpec((DEC_SEQ, KV_WIDTH), lambda b, n, s: (0, 0))],
            out_specs=pl.BlockSpec((BLOCK, ATTN_WIDTH), lambda b, n, s: (b * nb + n, 0)),
        ),
        out_shape=jax.ShapeDtypeStruct((T_LAT, ATTN_WIDTH), F32),
        compiler_params=_params(("arbitrary", "arbitrary")),
        name="latent_attention",
    )(sink, q, k, v, cache_k, cache_v, cosq, sinq, cosk, sin_k)


def _rope_tables():
    t = jnp.arange(DEC_SEQ)
    nf = HEAD_DIM // 4
    inv = ROPE_BASE ** (-jnp.arange(nf, dtype=F32) / nf)

    def half(coord):
        ang = coord.astype(F32)[:, None] * inv[None, :]
        c, s = jnp.cos(ang), jnp.sin(ang)
        return jnp.concatenate([c, c], axis=1), jnp.concatenate([-s, s], axis=1)

    cr, sr = half(t // GRID_W)
    cc, sc = half(t % GRID_W)
    cos = jnp.concatenate([cr, cc], axis=1)
    sin = jnp.concatenate([sr, sc], axis=1)
    return (jnp.tile(cos, (1, N_Q_HEADS)), jnp.tile(sin, (1, N_Q_HEADS)),
            jnp.tile(cos, (1, N_KV_HEADS)), jnp.tile(sin, (1, N_KV_HEADS)))


def _ret_kernel(lg_ref, cd_ref, q_ref, k_ref, v_ref, sg_ref, gn_ref, *rest, n_chunks, has_s0, write_state):
    rest = list(rest)
    s0_ref = rest.pop(0) if has_s0 else None
    o_ref = rest.pop(0)
    sf_ref = rest.pop(0) if write_state else None
    acc_ref = rest.pop(0)
    row = lax.broadcasted_iota(jnp.int32, (RET_CHUNK, RET_CHUNK), 0).astype(F32)
    col = lax.broadcasted_iota(jnp.int32, (RET_CHUNK, RET_CHUNK), 1).astype(F32)
    rel = row - col
    pos = lax.broadcasted_iota(jnp.int32, (RET_CHUNK, 1), 0).astype(F32)

    def chunk(ref, c, cols):
        return ref[c * RET_CHUNK:(c + 1) * RET_CHUNK, cols]

    def inter(direction, h, c, state):
        cols = slice(h * RET_DK, (h + 1) * RET_DK)
        lg = lg_ref[direction * N_RET_HEADS + h]
        cd = cd_ref[direction * N_RET_HEADS + h]
        if direction == 0:
            q_dec = jnp.exp(lg * (pos + 1.0))
            k_dec = jnp.exp(lg * (RET_CHUNK - 1.0 - pos))
        else:
            q_dec = jnp.exp(lg * (RET_CHUNK - pos))
            k_dec = jnp.exp(lg * pos)
        vb = chunk(v_ref, c, cols).astype(BF16)
        grow = jnp.dot((chunk(k_ref, c, cols) * k_dec).T.astype(BF16), vb, preferred_element_type=F32)
        if state is None:
            return None, grow
        o = jnp.dot(chunk(q_ref, c, cols).astype(BF16), state.astype(BF16), preferred_element_type=F32) * q_dec
        return o, state * cd + grow

    for h in range(N_RET_HEADS):
        cols = slice(h * RET_DK, (h + 1) * RET_DK)
        lgf, lgb = lg_ref[h], lg_ref[N_RET_HEADS + h]
        intra = jnp.where(rel >= 0, jnp.exp(lgf * rel), 0.0) + jnp.where(rel <= 0, jnp.exp(-lgb * rel), 0.0)
        state = s0_ref[0, 0, 0, h] if has_s0 else None
        for c in range(n_chunks):
            qb, kb = chunk(q_ref, c, cols).astype(BF16), chunk(k_ref, c, cols).astype(BF16)
            scores = lax.dot_general(qb, kb, (((1,), (1,)), ((), ())), preferred_element_type=F32) * intra
            o = jnp.dot(scores.astype(BF16), chunk(v_ref, c, cols).astype(BF16), preferred_element_type=F32)
            o_fwd, state = inter(0, h, c, state)
            acc_ref[c * RET_CHUNK:(c + 1) * RET_CHUNK, cols] = o if o_fwd is None else o + o_fwd
        if write_state:
            sf_ref[0, 0, h] = state
        state = s0_ref[0, 0, 1, h] if has_s0 else None
        gn = gn_ref[:, cols]
        for c in range(n_chunks - 1, -1, -1):
            o_bwd, state = inter(1, h, c, state)
            tot = chunk(acc_ref, c, cols)
            if o_bwd is not None:
                tot = tot + o_bwd
            o_ref[c * RET_CHUNK:(c + 1) * RET_CHUNK, cols] = _rms_rows(tot, gn) * chunk(sg_ref, c, cols)
        if write_state:
            sf_ref[0, 1, h] = state


def _retention(lg, cd, rq, rk, rv, sg, gn, s0, layer, *, n_seq, seq_len, row0, write_state):
    blk0 = row0 // seq_len
    rows = pl.BlockSpec((seq_len, RET_WIDTH), lambda b, *_: (blk0 + b, 0))
    in_specs = [rows, rows, rows, rows, pl.BlockSpec((1, RET_WIDTH), lambda b, *_: (0, 0))]
    args = [rq, rk, rv, sg, gn]
    if s0 is not None:
        in_specs.append(pl.BlockSpec((1, 1, 2, N_RET_HEADS, RET_DK, RET_DK), lambda b, *_: (b, layer, 0, 0, 0, 0)))
        args.append(s0)
    out_specs = [pl.BlockSpec((seq_len, RET_WIDTH), lambda b, *_: (b, 0))]
    out_shape = [jax.ShapeDtypeStruct((n_seq * seq_len, RET_WIDTH), F32)]
    if write_state:
        out_specs.append(pl.BlockSpec((1, 2, N_RET_HEADS, RET_DK, RET_DK), lambda b, *_: (b, 0, 0, 0, 0)))
        out_shape.append(jax.ShapeDtypeStruct((n_seq, 2, N_RET_HEADS, RET_DK, RET_DK), F32))
    kern = functools.partial(_ret_kernel, n_chunks=seq_len // RET_CHUNK, has_s0=s0 is not None,
                             write_state=write_state)
    return pl.pallas_call(
        kern,
        grid_spec=pltpu.PrefetchScalarGridSpec(
            num_scalar_prefetch=2,
            grid=(n_seq,),
            in_specs=in_specs,
            out_specs=out_specs,
            scratch_shapes=[pltpu.VMEM((seq_len, RET_WIDTH), F32)],
        ),
        out_shape=out_shape,
        compiler_params=_params(("arbitrary",)),
        name="retention_ctx" if write_state else "retention_lat",
    )(lg, cd, *args)


def _outproj_kernel(attc_ref, attl_ref, retc_ref, retl_ref, x_ref, wo_ref, g1_ref, sh2_ref, sc2_ref, nf_ref,
                    wrh_ref, wrl_ref, br_ref, x1_ref, h2t_ref, ti_ref, tw_ref):
    is_ctx = pl.program_id(0) < T_CTX // ROW_TILE
    att = jnp.where(is_ctx, attc_ref[...], attl_ref[...])
    ret = jnp.where(is_ctx, retc_ref[...], retl_ref[...])
    y = (jnp.dot(att.astype(BF16), wo_ref[0:ATTN_WIDTH, :], preferred_element_type=F32)
         + jnp.dot(ret.astype(BF16), wo_ref[ATTN_WIDTH:, :], preferred_element_type=F32))
    x1 = x_ref[...] + g1_ref[0] * y
    x1_ref[...] = x1
    h2 = _rms_rows(x1, nf_ref[...]) * (1.0 + sc2_ref[0]) + sh2_ref[0]
    for s in range(SLABS):
        h2t_ref[pl.ds(s, ROW_TILE, stride=SLABS), :] = h2[:, s * LANES:(s + 1) * LANES]
    hh = h2.astype(BF16)
    hl = (h2 - hh.astype(F32)).astype(BF16)
    wrh = wrh_ref[...]
    logits = (jnp.dot(hh, wrh, preferred_element_type=F32) + jnp.dot(hl, wrh, preferred_element_type=F32)
              + jnp.dot(hh, wrl_ref[...], preferred_element_type=F32) + br_ref[...])
    lane = lax.broadcasted_iota(jnp.int32, logits.shape, 1)
    vals, idxs = [], []
    cur = logits
    for _ in range(TOP_K):
        m = jnp.max(cur, axis=-1, keepdims=True)
        idx = jnp.min(jnp.where(cur == m, lane, LANES), axis=-1, keepdims=True)
        vals.append(m)
        idxs.append(idx)
        cur = jnp.where(lane == idx, -jnp.inf, cur)
    es = [jnp.exp(v - vals[0]) for v in vals]
    den = es[0] + es[1] + es[2] + es[3]
    ti = jnp.zeros(logits.shape, jnp.int32)
    tw = jnp.zeros(logits.shape, F32)
    for k in range(TOP_K):
        ti = jnp.where(lane == k, idxs[k], ti)
        tw = jnp.where(lane == k, es[k] / den, tw)
    ti_ref[...] = ti
    tw_ref[...] = tw


def _output_projection(att_c, att_l, ret_c, ret_l, x, w_out_bf, mod_l, nf, wrh, wrl, br):
    n_ctx = T_CTX // ROW_TILE
    rows = lambda w: pl.BlockSpec((ROW_TILE, w), lambda i: (i, 0))
    ctx_rows = lambda w: pl.BlockSpec((ROW_TILE, w), lambda i: (jnp.minimum(i, n_ctx - 1), 0))
    lat_rows = lambda w: pl.BlockSpec((ROW_TILE, w), lambda i: (jnp.maximum(i - n_ctx, 0), 0))
    return pl.pallas_call(
        _outproj_kernel,
        grid=(T_ALL // ROW_TILE,),
        in_specs=[ctx_rows(ATTN_WIDTH), lat_rows(ATTN_WIDTH), ctx_rows(RET_WIDTH), lat_rows(RET_WIDTH),
                  rows(D_MODEL), _full((D_MODEL, D_MODEL)),
                  _mod_spec(2), _mod_spec(3), _mod_spec(4), _full((1, D_MODEL)),
                  _full((D_MODEL, LANES)), _full((D_MODEL, LANES)), _full((1, LANES))],
        out_specs=[rows(D_MODEL), pl.BlockSpec((ROW_TILE * SLABS, LANES), lambda i: (i, 0)), rows(LANES), rows(LANES)],
        out_shape=[jax.ShapeDtypeStruct((T_ALL, D_MODEL), F32), jax.ShapeDtypeStruct((T_ALL * SLABS, LANES), F32),
                   jax.ShapeDtypeStruct((T_ALL, LANES), jnp.int32), jax.ShapeDtypeStruct((T_ALL, LANES), F32)],
        compiler_params=_params(("arbitrary",)),
        name="outproj_router",
    )(att_c, att_l, ret_c, ret_l, x, w_out_bf, mod_l, mod_l, mod_l, nf, wrh, wrl, br)


def _token_tile(ref, t):
    return ref.at[pl.ds(pl.multiple_of(t * SLABS, SLABS), SLABS), :]


def _step_is_used(i, nused_ref):
    return i * MOE_STEP_BLOCKS < nused_ref[0]


def _dispatch_kernel(nused_ref, src_ref, h2t_hbm, o_ref, hres, xg, sem):
    i = pl.program_id(0)

    @pl.when(i == 0)
    def _():
        cp = pltpu.make_async_copy(h2t_hbm, hres.at[pl.ds(0, T_ALL * SLABS), :], sem.at[0])
        cp.start()
        _token_tile(hres, T_ALL)[...] = jnp.zeros((SLABS, LANES), F32)
        cp.wait()

    @pl.when(_step_is_used(i, nused_ref))
    def _():
        def body(r, carry):
            _token_tile(xg, r)[...] = _token_tile(hres, src_ref[0, 0, r])[...]
            return carry

        lax.fori_loop(0, MOE_STEP_ROWS, body, 0, unroll=8)
        for s in range(SLABS):
            o_ref[:, s * LANES:(s + 1) * LANES] = xg[pl.ds(s, MOE_STEP_ROWS, stride=SLABS), :].astype(BF16)

    @pl.when(jnp.logical_not(_step_is_used(i, nused_ref)))
    def _():
        o_ref[...] = jnp.zeros(o_ref.shape, BF16)


def _smem_rows(width):
    return pl.BlockSpec((1, 1, width), lambda i, nused: (jnp.minimum(i, N_MOE_STEPS - 1), 0, 0),
                        memory_space=pltpu.SMEM)


def _moe_dispatch(n_used, src3, h2t):
    return pl.pallas_call(
        _dispatch_kernel,
        grid_spec=pltpu.PrefetchScalarGridSpec(
            num_scalar_prefetch=1,
            grid=(N_MOE_STEPS,),
            in_specs=[_smem_rows(MOE_STEP_ROWS), pl.BlockSpec(memory_space=pl.ANY)],
            out_specs=pl.BlockSpec((MOE_STEP_ROWS, D_MODEL), lambda i, nused: (i, 0)),
            scratch_shapes=[pltpu.VMEM((N_TILES * SLABS, LANES), F32), pltpu.VMEM((MOE_STEP_ROWS * SLABS, LANES), F32),
                            pltpu.SemaphoreType.DMA((1,))],
        ),
        out_shape=jax.ShapeDtypeStruct((N_PAD, D_MODEL), BF16),
        compiler_params=_params(("arbitrary",)),
        name="moe_dispatch",
    )(n_used, src3, h2t)


X_SLOTS = 8
X_AHEAD = 6
Y_SLOTS = 4


def _block_rows(ref, g):
    return ref.at[pl.ds(pl.multiple_of(g * MOE_ROWS, MOE_ROWS), MOE_ROWS), :]


TILE_ROWS = MOE_ROWS * SLABS


def _tile_block(ref, g):
    return ref.at[pl.ds(pl.multiple_of(g * TILE_ROWS, TILE_ROWS), TILE_ROWS), :]


def _experts_kernel(blk0_ref, nblk_ref, nused_ref, src_ref, h2t_hbm, wgu_ref, bgu_ref, wdn_ref, bdn_ref, yst_hbm,
                    xbuf, ybuf, wgu_bf, wdn_bf, xsem, ysem):
    e = pl.program_id(0)
    b0, nb, nused = blk0_ref[e], nblk_ref[e], nused_ref[0]

    def row_copy(g, slot, r):
        tok = jnp.minimum(src_ref[g * MOE_ROWS + r], T_ALL - 1)
        dst = xbuf.at[pl.ds(pl.multiple_of(slot * TILE_ROWS + r * SLABS, SLABS), SLABS), :]
        return pltpu.make_async_copy(_token_tile(h2t_hbm, tok), dst, xsem.at[slot])

    def gather_wait(slot):
        pltpu.make_async_copy(_tile_block(h2t_hbm, 0), _tile_block(xbuf, slot), xsem.at[slot]).wait()

    def y_copy(g):
        slot = g % Y_SLOTS
        return pltpu.make_async_copy(_tile_block(ybuf, slot), _tile_block(yst_hbm, g), ysem.at[slot])

    @pl.when(e == 0)
    def _():
        for k in range(X_AHEAD):
            @pl.when(k < nused)
            def _():
                def issue(r, carry):
                    row_copy(k, k, r).start()
                    return carry
                lax.fori_loop(0, MOE_ROWS, issue, 0)

    @pl.when(nb > 0)
    def _():
        wgu_bf[...] = wgu_ref[...].astype(BF16)
        wdn_bf[...] = wdn_ref[...].astype(BF16)

        def process(blocks):
            xs = []
            for g in blocks:
                slot = g % X_SLOTS
                gather_wait(slot)
                xs.append(jnp.concatenate(
                    [xbuf[pl.ds(slot * TILE_ROWS + s, MOE_ROWS, stride=SLABS), :].astype(BF16)
                     for s in range(SLABS)], axis=1))

            for g in blocks:
                ahead = g + X_AHEAD
                ahead_slot = jnp.where(ahead < nused, ahead % X_SLOTS, X_SLOTS + ahead - nused)
                ahead_blk = jnp.minimum(ahead, nused - 1)
                for r in range(MOE_ROWS):
                    row_copy(ahead_blk, ahead_slot, r).start()

            outs = []
            for x in xs:
                gu = jnp.dot(x, wgu_bf[...], preferred_element_type=F32) + bgu_ref[...]
                x_glu = jnp.minimum(gu[:, :D_FF], SWIGLU_LIMIT)
                x_lin = jnp.clip(gu[:, D_FF:], -SWIGLU_LIMIT, SWIGLU_LIMIT)
                act = x_glu * jax.nn.sigmoid(SWIGLU_ALPHA * x_glu) * (x_lin + 1.0)
                outs.append(jnp.dot(act.astype(BF16), wdn_bf[...], preferred_element_type=F32) + bdn_ref[...])

            for g in blocks:
                @pl.when(g >= Y_SLOTS)
                def _():
                    y_copy(g - Y_SLOTS).wait()

            for g, out in zip(blocks, outs):
                yslot = g % Y_SLOTS
                for s in range(SLABS):
                    ybuf[pl.ds(yslot * TILE_ROWS + s, MOE_ROWS, stride=SLABS), :] = out[:, s * LANES:(s + 1) * LANES]
                y_copy(g).start()

        def one(j, carry):
            process([b0 + j])
            return carry

        lax.fori_loop(0, nb, one, 0)

    @pl.when(e == N_EXPERTS - 1)
    def _():
        for k in range(X_AHEAD):
            @pl.when(k < nused)
            def _():
                gather_wait(X_SLOTS + X_AHEAD - 1 - k)

        for k in range(1, Y_SLOTS + 1):
            @pl.when(nused >= k)
            def _():
                y_copy(nused - k).wait()

        _tile_block(ybuf, 0)[...] = jnp.zeros((TILE_ROWS, LANES), F32)

        def fill(g, carry):
            cp = pltpu.make_async_copy(_tile_block(ybuf, 0), _tile_block(yst_hbm, g), ysem.at[0])
            cp.start()
            cp.wait()
            return carry

        lax.fori_loop(nused, N_MOE_BLOCKS, fill, 0)


def _moe_experts(blk0, nblk, n_used, src, h2t, w_gu, b_gu, w_dn, b_dn, layer):
    wspec = lambda rows, cols: pl.BlockSpec((None, None, rows, cols), lambda e, *_: (layer, e, 0, 0))
    return pl.pallas_call(
        _experts_kernel,
        grid_spec=pltpu.PrefetchScalarGridSpec(
            num_scalar_prefetch=4,
            grid=(N_EXPERTS,),
            in_specs=[pl.BlockSpec(memory_space=pl.ANY),
                      wspec(D_MODEL, 2 * D_FF), wspec(1, 2 * D_FF), wspec(D_FF, D_MODEL), wspec(1, D_MODEL)],
            out_specs=pl.BlockSpec(memory_space=pl.ANY),
            scratch_shapes=[pltpu.VMEM(((X_SLOTS + X_AHEAD) * TILE_ROWS, LANES), F32),
                            pltpu.VMEM((Y_SLOTS * TILE_ROWS, LANES), F32),
                            pltpu.VMEM((D_MODEL, 2 * D_FF), BF16), pltpu.VMEM((D_FF, D_MODEL), BF16),
                            pltpu.SemaphoreType.DMA((X_SLOTS + X_AHEAD,)), pltpu.SemaphoreType.DMA((Y_SLOTS,))],
        ),
        out_shape=jax.ShapeDtypeStruct((N_PAD * SLABS, LANES), F32),
        compiler_params=_params(("arbitrary",), vmem=EXPERT_VMEM_LIMIT),
        name="moe_experts",
    )(blk0, nblk, n_used, src, h2t, w_gu, b_gu.reshape(DEPTH, N_EXPERTS, 1, 2 * D_FF),
      w_dn, b_dn.reshape(DEPTH, N_EXPERTS, 1, D_MODEL))


N_ROW_TILES = T_ALL // ROW_TILE
ZERO_ROWS = ROW_TILE * SLABS


def _combine_kernel(nused_ref, src_ref, w_ref, ys_ref, x1_ref, g2_ref, o_ref, yres):
    i = pl.program_id(0)

    @pl.when(i == 0)
    def _():
        def zero(j, carry):
            yres[pl.ds(pl.multiple_of(j * ZERO_ROWS, ZERO_ROWS), ZERO_ROWS), :] = jnp.zeros((ZERO_ROWS, LANES), F32)
            return carry
        lax.fori_loop(0, N_ROW_TILES, zero, 0)
        _token_tile(yres, T_ALL)[...] = jnp.zeros((SLABS, LANES), F32)

    @pl.when(jnp.logical_and(i < N_MOE_STEPS, _step_is_used(i, nused_ref)))
    def _():
        def group(g, carry):
            rows = [g * GROUP + j for j in range(GROUP)]
            toks = [src_ref[0, 0, r] for r in rows]
            ws = [w_ref[0, 0, r] for r in rows]
            new = [_token_tile(yres, t)[...] + w * _token_tile(ys_ref, r)[...] for t, w, r in zip(toks, ws, rows)]
            for t, v in zip(toks, new):
                _token_tile(yres, t)[...] = v
            return carry

        lax.fori_loop(0, MOE_STEP_ROWS // GROUP, group, 0)

    @pl.when(i >= N_MOE_STEPS)
    def _():
        base = (i - N_MOE_STEPS) * ZERO_ROWS
        for s in range(SLABS):
            cols = slice(s * LANES, (s + 1) * LANES)
            y = yres[pl.ds(base + s, ROW_TILE, stride=SLABS), :]
            o_ref[:, cols] = x1_ref[:, cols] + g2_ref[0][:, cols] * y


def _moe_combine(n_used, src3, w3, ys, x1, mod_l):
    tile = lambda i: jnp.maximum(i - N_MOE_STEPS, 0)
    last_used = lambda nused: jnp.maximum(nused[0] - 1, 0) // MOE_STEP_BLOCKS
    return pl.pallas_call(
        _combine_kernel,
        grid_spec=pltpu.PrefetchScalarGridSpec(
            num_scalar_prefetch=1,
            grid=(N_MOE_STEPS + N_ROW_TILES,),
            in_specs=[_smem_rows(MOE_STEP_ROWS), _smem_rows(MOE_STEP_ROWS),
                      pl.BlockSpec((MOE_STEP_ROWS * SLABS, LANES), lambda i, nused: (jnp.minimum(i, last_used(nused)), 0)),
                      pl.BlockSpec((ROW_TILE, D_MODEL), lambda i, nused: (tile(i), 0)),
                      pl.BlockSpec((1, 1, D_MODEL), lambda i, nused: (_cond_of_tile(tile(i)), 0, 5))],
            out_specs=pl.BlockSpec((ROW_TILE, D_MODEL), lambda i, nused: (tile(i), 0)),
            scratch_shapes=[pltpu.VMEM((N_TILES * SLABS, LANES), F32)],
        ),
        out_shape=jax.ShapeDtypeStruct((T_ALL, D_MODEL), F32),
        compiler_params=_params(("arbitrary",)),
        name="moe_combine",
    )(n_used, src3, w3, ys, x1, mod_l)


def _routing_tables(top_idx, top_w):
    flat_e = top_idx.reshape(N_ASSIGN)
    experts = jnp.arange(N_EXPERTS, dtype=jnp.int32)
    counts = jnp.sum((flat_e[:, None] == experts[None, :]).astype(jnp.int32), axis=0)
    pad = (-counts) % MOE_ROWS
    spare = jnp.arange(MOE_ROWS, dtype=jnp.int32)
    pad_keys = jnp.where(spare[None, :] < pad[:, None], experts[:, None], N_EXPERTS).reshape(-1)
    keys = jnp.concatenate([flat_e, pad_keys])
    toks = jnp.concatenate([jnp.arange(N_ASSIGN, dtype=jnp.int32) // TOP_K,
                            jnp.full((N_PAD - N_ASSIGN,), T_ALL, jnp.int32)])
    wts = jnp.concatenate([top_w.reshape(N_ASSIGN), jnp.zeros((N_PAD - N_ASSIGN,), F32)])
    _, src, w_sorted = lax.sort((keys, toks, wts), num_keys=1, is_stable=True)
    padded = counts + pad
    blk_end = jnp.cumsum(padded) // MOE_ROWS
    nblk = padded // MOE_ROWS
    return (src.reshape(N_MOE_STEPS, 1, MOE_STEP_ROWS), w_sorted.reshape(N_MOE_STEPS, 1, MOE_STEP_ROWS),
            (blk_end - nblk).astype(jnp.int32), nblk.astype(jnp.int32), blk_end[-1:].astype(jnp.int32))


def kernel(x_prompt, x_sample, cache_attn_k, cache_attn_v, state_ret, c, c_ctx, norm_mix, norm_ffn, w_ada, b_ada,
           w_in, q_norm, k_norm, attn_sink, ret_decay, ret_norm, w_out, w_router, b_router, w_gate_up, b_gate_up,
           w_down, b_down):
    x = jnp.concatenate([x_prompt.reshape(T_CTX, D_MODEL), x_sample.reshape(T_LAT, D_MODEL)], axis=0)
    cond = jnp.zeros((COND_PAD, D_MODEL), F32).at[0].set(c_ctx).at[1:N_COND].set(c)
    mod = _modulation(cond, w_ada, b_ada)[:, :N_COND].reshape(DEPTH, N_COND, 1, 6 * D_MODEL)

    cache_k = cache_attn_k.reshape(DEC_BATCH, DEPTH, PAST_LEN, KV_WIDTH)
    cache_v = cache_attn_v.reshape(DEC_BATCH, DEPTH, PAST_LEN, KV_WIDTH)
    cosq, sinq, cosk, sin_k = _rope_tables()
    grp = jnp.arange(ATTN_WIDTH) // HEAD_DIM
    avgq = jnp.where(grp[:, None] == grp[None, :], 1.0 / HEAD_DIM, 0.0).astype(BF16)
    avgk = avgq[:KV_WIDTH, :KV_WIDTH]
    log_gamma = jax.nn.log_sigmoid(ret_decay.astype(F32))
    chunk_decay = jnp.exp(log_gamma * RET_CHUNK)

    new_k, new_v, new_s = [], [], []
    for l in range(DEPTH):
        mod_l = mod[l]
        q, k, v, rq, rk, rv, sg = _input_projection(
            x, norm_mix[l].reshape(1, D_MODEL), mod_l, w_in[l].astype(BF16),
            jnp.tile(q_norm[l], N_Q_HEADS).reshape(1, ATTN_WIDTH), jnp.tile(k_norm[l], N_KV_HEADS).reshape(1, KV_WIDTH),
            avgq, avgk)
        new_k.append(k[:T_CTX].reshape(BATCH, SEQ, N_KV_HEADS, HEAD_DIM))
        new_v.append(v[:T_CTX].reshape(BATCH, SEQ, N_KV_HEADS, HEAD_DIM))
        sink = attn_sink[l].astype(F32)
        att_c = _context_attention(sink, q, k, v)
        att_l = _latent_attention(sink, q, k, v, cache_k, cache_v, l, cosq, sinq, cosk, sin_k)
        lg = log_gamma[l].reshape(2 * N_RET_HEADS)
        cd = chunk_decay[l].reshape(2 * N_RET_HEADS)
        gn = ret_norm[l].reshape(1, RET_WIDTH)
        ret_c, s_fin = _retention(lg, cd, rq, rk, rv, sg, gn, None, l, n_seq=BATCH, seq_len=SEQ, row0=0,
                                  write_state=True)
        (ret_l,) = _retention(lg, cd, rq, rk, rv, sg, gn, state_ret, l, n_seq=DEC_BATCH, seq_len=DEC_SEQ,
                              row0=T_CTX, write_state=False)
        new_s.append(s_fin)
        wr = jnp.zeros((D_MODEL, LANES), F32).at[:, :N_EXPERTS].set(w_router[l])
        wrh = wr.astype(BF16)
        wrl = (wr - wrh.astype(F32)).astype(BF16)
        br = jnp.full((1, LANES), NEG_BIG, F32).at[0, :N_EXPERTS].set(b_router[l])
        x1, h2t, ti, tw = _output_projection(att_c, att_l, ret_c, ret_l, x, w_out[l].astype(BF16), mod_l,
                                             norm_ffn[l].reshape(1, D_MODEL), wrh, wrl, br)
        src3, w3, blk0, nblk, n_used = _routing_tables(ti[:, :TOP_K], tw[:, :TOP_K])
        ys = _moe_experts(blk0, nblk, n_used, src3.reshape(N_PAD), h2t, w_gate_up, b_gate_up, w_down, b_down, l)
        x = _moe_combine(n_used, src3, w3, ys, x1, mod_l)

    y_prompt = x[:T_CTX].reshape(BATCH, SEQ, D_MODEL)
    y_sample = x[T_CTX:].reshape(DEC_BATCH, DEC_SEQ, D_MODEL)
    return (y_prompt, y_sample, jnp.stack(new_k, axis=1), jnp.stack(new_v, axis=1), jnp.stack(new_s, axis=1))
```

```python
import functools

import jax
import jax.numpy as jnp
from jax import lax
from jax.experimental import pallas as pl
from jax.experimental.pallas import tpu as pltpu

F32 = jnp.float32
BF16 = jnp.bfloat16

D_MODEL = 1024
DEPTH = 4
BATCH, SEQ = 16, 256
DEC_BATCH, DEC_SEQ = 2, 1024
PAST_LEN = 512
GRID_W = 64
HEAD_DIM = 64
N_Q_HEADS = 8
N_KV_HEADS = 2
ATTN_WIDTH = N_Q_HEADS * HEAD_DIM
KV_WIDTH = N_KV_HEADS * HEAD_DIM
WINDOW = 128
BLOCK = 128
ROPE_BASE = 10000.0
N_RET_HEADS = 4
RET_DK = 128
RET_WIDTH = N_RET_HEADS * RET_DK
RET_CHUNK = 256
IN_WIDTH = ATTN_WIDTH + 2 * KV_WIDTH + 4 * RET_WIDTH
N_EXPERTS = 32
TOP_K = 4
D_FF = D_MODEL
SWIGLU_LIMIT = 7.0
SWIGLU_ALPHA = 1.702
EPS = 1e-6

T_CTX = BATCH * SEQ
T_LAT = DEC_BATCH * DEC_SEQ
T_ALL = T_CTX + T_LAT
N_COND = 1 + DEC_BATCH
COND_PAD = 8
LANES = 128
ROW_TILE = 512
MOE_ROWS = 128
N_ASSIGN = T_ALL * TOP_K
N_MOE_BLOCKS = N_ASSIGN // MOE_ROWS + N_EXPERTS
N_PAD = N_MOE_BLOCKS * MOE_ROWS
MOE_STEP_BLOCKS = 4
MOE_STEP_ROWS = MOE_STEP_BLOCKS * MOE_ROWS
N_MOE_STEPS = N_MOE_BLOCKS // MOE_STEP_BLOCKS
SLABS = D_MODEL // LANES
GROUP = 8
N_TILES = T_ALL + 1
NEG_BIG = -1e30
VMEM_LIMIT = 48 * 1024 * 1024
EXPERT_VMEM_LIMIT = 56 * 1024 * 1024


def _cond_of_tile(i):
    return jnp.where(i < T_CTX // ROW_TILE, 0, 1 + (i - T_CTX // ROW_TILE) // (DEC_SEQ // ROW_TILE))


def _params(sem, vmem=VMEM_LIMIT):
    return pltpu.CompilerParams(dimension_semantics=sem, vmem_limit_bytes=vmem)


def _mod_kernel(c_ref, w_ref, b_ref, o_ref):
    c = c_ref[...]
    s = (c * jax.nn.sigmoid(c)).astype(BF16)
    o_ref[0] = jnp.dot(s, w_ref[0].astype(BF16), preferred_element_type=F32) + b_ref[0]


def _modulation(cond, w_ada, b_ada):
    n_col = 6 * D_MODEL // D_MODEL
    return pl.pallas_call(
        _mod_kernel,
        grid=(DEPTH, n_col),
        in_specs=[
            pl.BlockSpec((COND_PAD, D_MODEL), lambda l, j: (0, 0)),
            pl.BlockSpec((1, D_MODEL, D_MODEL), lambda l, j: (l, 0, j)),
            pl.BlockSpec((1, 1, D_MODEL), lambda l, j: (l, 0, j)),
        ],
        out_specs=pl.BlockSpec((1, COND_PAD, D_MODEL), lambda l, j: (l, 0, j)),
        out_shape=jax.ShapeDtypeStruct((DEPTH, COND_PAD, 6 * D_MODEL), F32),
        compiler_params=_params(("arbitrary", "arbitrary")),
        name="modulation",
    )(cond, w_ada, b_ada.reshape(DEPTH, 1, 6 * D_MODEL))


def _rms_rows(x, g):
    ms = jnp.mean(x * x, axis=-1, keepdims=True)
    return x * lax.rsqrt(ms + EPS) * g


def _group_rmsnorm(a, avg_ref, g):
    ms = jnp.dot((a * a).astype(BF16), avg_ref[...], preferred_element_type=F32)
    return a * lax.rsqrt(ms + EPS) * g


def _inproj_kernel(x_ref, g_ref, sh_ref, sc_ref, w_ref, qn_ref, kn_ref, avgq_ref, avgk_ref,
                   q_ref, k_ref, v_ref, rq_ref, rk_ref, rv_ref, sg_ref):
    h = _rms_rows(x_ref[...], g_ref[...]) * (1.0 + sc_ref[0]) + sh_ref[0]
    hb = h.astype(BF16)

    def proj(lo, width):
        return jnp.dot(hb, w_ref[:, lo:lo + width], preferred_element_type=F32)

    o = 0
    q_ref[...] = _group_rmsnorm(proj(o, ATTN_WIDTH), avgq_ref, qn_ref[...])
    o += ATTN_WIDTH
    k_ref[...] = _group_rmsnorm(proj(o, KV_WIDTH), avgk_ref, kn_ref[...])
    o += KV_WIDTH
    v_ref[...] = proj(o, KV_WIDTH)
    o += KV_WIDTH
    rq_ref[...] = proj(o, RET_WIDTH) * (RET_DK ** -0.5)
    o += RET_WIDTH
    rk_ref[...] = proj(o, RET_WIDTH)
    o += RET_WIDTH
    rv_ref[...] = proj(o, RET_WIDTH)
    o += RET_WIDTH
    rg = proj(o, RET_WIDTH)
    sg_ref[...] = rg * jax.nn.sigmoid(rg)


def _mod_spec(col):
    return pl.BlockSpec((1, 1, D_MODEL), lambda i, col=col: (_cond_of_tile(i), 0, col))


def _full(shape):
    return pl.BlockSpec(shape, lambda *_: (0,) * len(shape))


def _input_projection(x, norm_g, mod_l, w_in_bf, qn, kn, avgq, avgk):
    rows = lambda w: pl.BlockSpec((ROW_TILE, w), lambda i: (i, 0))
    widths = (ATTN_WIDTH, KV_WIDTH, KV_WIDTH, RET_WIDTH, RET_WIDTH, RET_WIDTH, RET_WIDTH)
    return pl.pallas_call(
        _inproj_kernel,
        grid=(T_ALL // ROW_TILE,),
        in_specs=[rows(D_MODEL), _full((1, D_MODEL)), _mod_spec(0), _mod_spec(1),
                  _full((D_MODEL, IN_WIDTH)), _full((1, ATTN_WIDTH)), _full((1, KV_WIDTH)),
                  _full((ATTN_WIDTH, ATTN_WIDTH)), _full((KV_WIDTH, KV_WIDTH))],
        out_specs=[rows(w) for w in widths],
        out_shape=[jax.ShapeDtypeStruct((T_ALL, w), F32) for w in widths],
        compiler_params=_params(("arbitrary",)),
        name="norm_inproj",
    )(x, norm_g, mod_l, mod_l, w_in_bf, qn, kn, avgq, avgk)


def _attend(q, kall, vall, valid_of, sink_ref, o_ref):
    m_rows, n_keys = q.shape[0], kall.shape[0]
    scale = HEAD_DIM ** -0.5
    lane = lax.broadcasted_iota(jnp.int32, (1, LANES), 1)
    low = lane < HEAD_DIM
    row = lax.broadcasted_iota(jnp.int32, (2 * m_rows, 1), 0)
    second = row >= m_rows
    valid = None
    if valid_of is not None:
        qrow = lax.broadcasted_iota(jnp.int32, (2 * m_rows, n_keys), 0)
        qrow = jnp.where(qrow >= m_rows, qrow - m_rows, qrow)
        valid = valid_of(qrow, lax.broadcasted_iota(jnp.int32, (2 * m_rows, n_keys), 1))
    for g in range(N_KV_HEADS):
        keep = low if g == 0 else jnp.logical_not(low)
        kg = jnp.where(keep, kall, 0.0)
        vg = jnp.where(keep, vall, 0.0)
        kr = pltpu.roll(kg, HEAD_DIM, 1)
        vr = pltpu.roll(vg, HEAD_DIM, 1)
        k_at = (kg, kr) if g == 0 else (kr, kg)
        v_at = (vg, vr) if g == 0 else (vr, vg)
        kcat = jnp.concatenate(k_at, axis=0).astype(BF16)
        qg = jnp.concatenate([q[:, (2 * g + b) * LANES:(2 * g + b + 1) * LANES] for b in range(2)],
                             axis=0).astype(BF16)
        s_all = lax.dot_general(qg, kcat, (((1,), (1,)), ((), ())), preferred_element_type=F32) * scale
        acc = None
        for off in range(2):
            s = s_all[:, off * n_keys:(off + 1) * n_keys]
            if valid is not None:
                s = jnp.where(valid, s, NEG_BIG)
            h0 = 4 * g + off
            sink = jnp.where(second, sink_ref[h0 + 2], sink_ref[h0])
            m = jnp.maximum(jnp.max(s, axis=-1, keepdims=True), sink)
            e = jnp.exp(s - m)
            den = jnp.sum(e, axis=-1, keepdims=True) + jnp.exp(sink - m)
            o = jnp.dot(e.astype(BF16), v_at[off].astype(BF16), preferred_element_type=F32) / den
            acc = o if acc is None else acc + o
        for b in range(2):
            o_ref[:, (2 * g + b) * LANES:(2 * g + b + 1) * LANES] = acc[b * m_rows:(b + 1) * m_rows]


def _ctx_attn_kernel(sink_ref, q_ref, k_ref, v_ref, o_ref):
    _attend(q_ref[...], k_ref[...], v_ref[...], None, sink_ref, o_ref)


def _context_attention(sink, q, k, v):
    return pl.pallas_call(
        _ctx_attn_kernel,
        grid_spec=pltpu.PrefetchScalarGridSpec(
            num_scalar_prefetch=1,
            grid=(BATCH,),
            in_specs=[pl.BlockSpec((SEQ, ATTN_WIDTH), lambda b, s: (b, 0)),
                      pl.BlockSpec((SEQ, KV_WIDTH), lambda b, s: (b, 0)),
                      pl.BlockSpec((SEQ, KV_WIDTH), lambda b, s: (b, 0))],
            out_specs=pl.BlockSpec((SEQ, ATTN_WIDTH), lambda b, s: (b, 0)),
        ),
        out_shape=jax.ShapeDtypeStruct((T_CTX, ATTN_WIDTH), F32),
        compiler_params=_params(("arbitrary",)),
        name="context_attention",
    )(sink, q, k, v)


def _rope_block(x, cos, sin_signed):
    lane = lax.broadcasted_iota(jnp.int32, (1, LANES), 1)
    first = (lane % (HEAD_DIM // 2)) < (HEAD_DIM // 4)
    swapped = jnp.where(first, pltpu.roll(x, LANES - HEAD_DIM // 4, 1), pltpu.roll(x, HEAD_DIM // 4, 1))
    return x * cos + swapped * sin_signed


LOCAL_KEYS = 3 * BLOCK


def _lat_attn_kernel(sink_ref, q_ref, k_ref, v_ref, ck_ref, cv_ref, cosq_ref, sinq_ref, cosk_ref, sin_k_ref,
                     o_ref):
    n = pl.program_id(1)
    start = pl.multiple_of(jnp.clip((n - 1) * BLOCK, 0, DEC_SEQ - LOCAL_KEYS), BLOCK)
    q = q_ref[...]
    q = jnp.concatenate(
        [_rope_block(q[:, j * LANES:(j + 1) * LANES], cosq_ref[:, j * LANES:(j + 1) * LANES],
                     sinq_ref[:, j * LANES:(j + 1) * LANES]) for j in range(ATTN_WIDTH // LANES)], axis=1)
    kw = _rope_block(k_ref[pl.ds(start, LOCAL_KEYS), :], cosk_ref[pl.ds(start, LOCAL_KEYS), :],
                     sin_k_ref[pl.ds(start, LOCAL_KEYS), :])
    vw = v_ref[pl.ds(start, LOCAL_KEYS), :]
    kall = jnp.concatenate([kw, ck_ref[0, 0]], axis=0)
    vall = jnp.concatenate([vw, cv_ref[0, 0]], axis=0)

    def valid_of(qrow, col):
        return jnp.logical_or(col >= LOCAL_KEYS, jnp.abs(n * BLOCK + qrow - (start + col)) <= WINDOW)

    _attend(q, kall, vall, valid_of, sink_ref, o_ref)


def _latent_attention(sink, q, k, v, cache_k, cache_v, layer, cosq, sinq, cosk, sin_k):
    nb = DEC_SEQ // BLOCK
    ctx_block0 = T_CTX // BLOCK
    ctx_seq0 = T_CTX // DEC_SEQ
    cache_spec = pl.BlockSpec((1, 1, PAST_LEN, KV_WIDTH), lambda b, n, s: (b, layer, 0, 0))
    return pl.pallas_call(
        _lat_attn_kernel,
        grid_spec=pltpu.PrefetchScalarGridSpec(
            num_scalar_prefetch=1,
            grid=(DEC_BATCH, nb),
            in_specs=[pl.BlockSpec((BLOCK, ATTN_WIDTH), lambda b, n, s: (ctx_block0 + b * nb + n, 0)),
                      pl.BlockSpec((DEC_SEQ, KV_WIDTH), lambda b, n, s: (ctx_seq0 + b, 0)),
                      pl.BlockSpec((DEC_SEQ, KV_WIDTH), lambda b, n, s: (ctx_seq0 + b, 0)),
                      cache_spec, cache_spec,
                      pl.BlockSpec((BLOCK, ATTN_WIDTH), lambda b, n, s: (n, 0)),
                      pl.BlockSpec((BLOCK, ATTN_WIDTH), lambda b, n, s: (n, 0)),
                      pl.BlockSpec((DEC_SEQ, KV_WIDTH), lambda b, n, s: (0, 0)),
                      pl.BlockSpec((DEC_SEQ, KV_WIDTH), lambda b, n, s: (0, 0))],
            out_specs=pl.BlockSpec((BLOCK, ATTN_WIDTH), lambda b, n, s: (b * nb + n, 0)),
        ),
        out_shape=jax.ShapeDtypeStruct((T_LAT, ATTN_WIDTH), F32),
        compiler_params=_params(("arbitrary", "arbitrary")),
        name="latent_attention",
    )(sink, q, k, v, cache_k, cache_v, cosq, sinq, cosk, sin_k)


def _rope_tables():
    t = jnp.arange(DEC_SEQ)
    nf = HEAD_DIM // 4
    inv = ROPE_BASE ** (-jnp.arange(nf, dtype=F32) / nf)

    def half(coord):
        ang = coord.astype(F32)[:, None] * inv[None, :]
        c, s = jnp.cos(ang), jnp.sin(ang)
        return jnp.concatenate([c, c], axis=1), jnp.concatenate([-s, s], axis=1)

    cr, sr = half(t // GRID_W)
    cc, sc = half(t % GRID_W)
    cos = jnp.concatenate([cr, cc], axis=1)
    sin = jnp.concatenate([sr, sc], axis=1)
    return (jnp.tile(cos, (1, N_Q_HEADS)), jnp.tile(sin, (1, N_Q_HEADS)),
            jnp.tile(cos, (1, N_KV_HEADS)), jnp.tile(sin, (1, N_KV_HEADS)))


def _ret_kernel(lg_ref, cd_ref, q_ref, k_ref, v_ref, sg_ref, gn_ref, *rest, n_chunks, has_s0, write_state):
    rest = list(rest)
    s0_ref = rest.pop(0) if has_s0 else None
    o_ref = rest.pop(0)
    sf_ref = rest.pop(0) if write_state else None
    acc_ref = rest.pop(0)
    row = lax.broadcasted_iota(jnp.int32, (RET_CHUNK, RET_CHUNK), 0).astype(F32)
    col = lax.broadcasted_iota(jnp.int32, (RET_CHUNK, RET_CHUNK), 1).astype(F32)
    rel = row - col
    pos = lax.broadcasted_iota(jnp.int32, (RET_CHUNK, 1), 0).astype(F32)

    def chunk(ref, c, cols):
        return ref[c * RET_CHUNK:(c + 1) * RET_CHUNK, cols]

    def inter(direction, h, c, state):
        cols = slice(h * RET_DK, (h + 1) * RET_DK)
        lg = lg_ref[direction * N_RET_HEADS + h]
        cd = cd_ref[direction * N_RET_HEADS + h]
        if direction == 0:
            q_dec = jnp.exp(lg * (pos + 1.0))
            k_dec = jnp.exp(lg * (RET_CHUNK - 1.0 - pos))
        else:
            q_dec = jnp.exp(lg * (RET_CHUNK - pos))
            k_dec = jnp.exp(lg * pos)
        vb = chunk(v_ref, c, cols).astype(BF16)
        grow = jnp.dot((chunk(k_ref, c, cols) * k_dec).T.astype(BF16), vb, preferred_element_type=F32)
        if state is None:
            return None, grow
        o = jnp.dot(chunk(q_ref, c, cols).astype(BF16), state.astype(BF16), preferred_element_type=F32) * q_dec
        return o, state * cd + grow

    for h in range(N_RET_HEADS):
        cols = slice(h * RET_DK, (h + 1) * RET_DK)
        lgf, lgb = lg_ref[h], lg_ref[N_RET_HEADS + h]
        intra = jnp.where(rel >= 0, jnp.exp(lgf * rel), 0.0) + jnp.where(rel <= 0, jnp.exp(-lgb * rel), 0.0)
        state = s0_ref[0, 0, 0, h] if has_s0 else None
        for c in range(n_chunks):
            qb, kb = chunk(q_ref, c, cols).astype(BF16), chunk(k_ref, c, cols).astype(BF16)
            scores = lax.dot_general(qb, kb, (((1,), (1,)), ((), ())), preferred_element_type=F32) * intra
            o = jnp.dot(scores.astype(BF16), chunk(v_ref, c, cols).astype(BF16), preferred_element_type=F32)
            o_fwd, state = inter(0, h, c, state)
            acc_ref[c * RET_CHUNK:(c + 1) * RET_CHUNK, cols] = o if o_fwd is None else o + o_fwd
        if write_state:
            sf_ref[0, 0, h] = state
        state = s0_ref[0, 0, 1, h] if has_s0 else None
        gn = gn_ref[:, cols]
        for c in range(n_chunks - 1, -1, -1):
            o_bwd, state = inter(1, h, c, state)
            tot = chunk(acc_ref, c, cols)
            if o_bwd is not None:
                tot = tot + o_bwd
            o_ref[c * RET_CHUNK:(c + 1) * RET_CHUNK, cols] = _rms_rows(tot, gn) * chunk(sg_ref, c, cols)
        if write_state:
            sf_ref[0, 1, h] = state


def _retention(lg, cd, rq, rk, rv, sg, gn, s0, layer, *, n_seq, seq_len, row0, write_state):
    blk0 = row0 // seq_len
    rows = pl.BlockSpec((seq_len, RET_WIDTH), lambda b, *_: (blk0 + b, 0))
    in_specs = [rows, rows, rows, rows, pl.BlockSpec((1, RET_WIDTH), lambda b, *_: (0, 0))]
    args = [rq, rk, rv, sg, gn]
    if s0 is not None:
        in_specs.append(pl.BlockSpec((1, 1, 2, N_RET_HEADS, RET_DK, RET_DK), lambda b, *_: (b, layer, 0, 0, 0, 0)))
        args.append(s0)
    out_specs = [pl.BlockSpec((seq_len, RET_WIDTH), lambda b, *_: (b, 0))]
    out_shape = [jax.ShapeDtypeStruct((n_seq * seq_len, RET_WIDTH), F32)]
    if write_state:
        out_specs.append(pl.BlockSpec((1, 2, N_RET_HEADS, RET_DK, RET_DK), lambda b, *_: (b, 0, 0, 0, 0)))
        out_shape.append(jax.ShapeDtypeStruct((n_seq, 2, N_RET_HEADS, RET_DK, RET_DK), F32))
    kern = functools.partial(_ret_kernel, n_chunks=seq_len // RET_CHUNK, has_s0=s0 is not None,
                             write_state=write_state)
    return pl.pallas_call(
        kern,
        grid_spec=pltpu.PrefetchScalarGridSpec(
            num_scalar_prefetch=2,
            grid=(n_seq,),
            in_specs=in_specs,
            out_specs=out_specs,
            scratch_shapes=[pltpu.VMEM((seq_len, RET_WIDTH), F32)],
        ),
        out_shape=out_shape,
        compiler_params=_params(("arbitrary",)),
        name="retention_ctx" if write_state else "retention_lat",
    )(lg, cd, *args)


def _outproj_kernel(attc_ref, attl_ref, retc_ref, retl_ref, x_ref, wo_ref, g1_ref, sh2_ref, sc2_ref, nf_ref,
                    wrh_ref, wrl_ref, br_ref, x1_ref, h2t_ref, ti_ref, tw_ref):
    is_ctx = pl.program_id(0) < T_CTX // ROW_TILE
    att = jnp.where(is_ctx, attc_ref[...], attl_ref[...])
    ret = jnp.where(is_ctx, retc_ref[...], retl_ref[...])
    y = (jnp.dot(att.astype(BF16), wo_ref[0:ATTN_WIDTH, :], preferred_element_type=F32)
         + jnp.dot(ret.astype(BF16), wo_ref[ATTN_WIDTH:, :], preferred_element_type=F32))
    x1 = x_ref[...] + g1_ref[0] * y
    x1_ref[...] = x1
    h2 = _rms_rows(x1, nf_ref[...]) * (1.0 + sc2_ref[0]) + sh2_ref[0]
    for s in range(SLABS):
        h2t_ref[pl.ds(s, ROW_TILE, stride=SLABS), :] = h2[:, s * LANES:(s + 1) * LANES]
    hh = h2.astype(BF16)
    hl = (h2 - hh.astype(F32)).astype(BF16)
    wrh = wrh_ref[...]
    logits = (jnp.dot(hh, wrh, preferred_element_type=F32) + jnp.dot(hl, wrh, preferred_element_type=F32)
              + jnp.dot(hh, wrl_ref[...], preferred_element_type=F32) + br_ref[...])
    lane = lax.broadcasted_iota(jnp.int32, logits.shape, 1)
    vals, idxs = [], []
    cur = logits
    for _ in range(TOP_K):
        m = jnp.max(cur, axis=-1, keepdims=True)
        idx = jnp.min(jnp.where(cur == m, lane, LANES), axis=-1, keepdims=True)
        vals.append(m)
        idxs.append(idx)
        cur = jnp.where(lane == idx, -jnp.inf, cur)
    es = [jnp.exp(v - vals[0]) for v in vals]
    den = es[0] + es[1] + es[2] + es[3]
    ti = jnp.zeros(logits.shape, jnp.int32)
    tw = jnp.zeros(logits.shape, F32)
    for k in range(TOP_K):
        ti = jnp.where(lane == k, idxs[k], ti)
        tw = jnp.where(lane == k, es[k] / den, tw)
    ti_ref[...] = ti
    tw_ref[...] = tw


def _output_projection(att_c, att_l, ret_c, ret_l, x, w_out_bf, mod_l, nf, wrh, wrl, br):
    n_ctx = T_CTX // ROW_TILE
    rows = lambda w: pl.BlockSpec((ROW_TILE, w), lambda i: (i, 0))
    ctx_rows = lambda w: pl.BlockSpec((ROW_TILE, w), lambda i: (jnp.minimum(i, n_ctx - 1), 0))
    lat_rows = lambda w: pl.BlockSpec((ROW_TILE, w), lambda i: (jnp.maximum(i - n_ctx, 0), 0))
    return pl.pallas_call(
        _outproj_kernel,
        grid=(T_ALL // ROW_TILE,),
        in_specs=[ctx_rows(ATTN_WIDTH), lat_rows(ATTN_WIDTH), ctx_rows(RET_WIDTH), lat_rows(RET_WIDTH),
                  rows(D_MODEL), _full((D_MODEL, D_MODEL)),
                  _mod_spec(2), _mod_spec(3), _mod_spec(4), _full((1, D_MODEL)),
                  _full((D_MODEL, LANES)), _full((D_MODEL, LANES)), _full((1, LANES))],
        out_specs=[rows(D_MODEL), pl.BlockSpec((ROW_TILE * SLABS, LANES), lambda i: (i, 0)), rows(LANES), rows(LANES)],
        out_shape=[jax.ShapeDtypeStruct((T_ALL, D_MODEL), F32), jax.ShapeDtypeStruct((T_ALL * SLABS, LANES), F32),
                   jax.ShapeDtypeStruct((T_ALL, LANES), jnp.int32), jax.ShapeDtypeStruct((T_ALL, LANES), F32)],
        compiler_params=_params(("arbitrary",)),
        name="outproj_router",
    )(att_c, att_l, ret_c, ret_l, x, w_out_bf, mod_l, mod_l, mod_l, nf, wrh, wrl, br)


def _token_tile(ref, t):
    return ref.at[pl.ds(pl.multiple_of(t * SLABS, SLABS), SLABS), :]


def _step_is_used(i, nused_ref):
    return i * MOE_STEP_BLOCKS < nused_ref[0]


def _smem_rows(width):
    return pl.BlockSpec((1, 1, width), lambda i, nused: (jnp.minimum(i, N_MOE_STEPS - 1), 0, 0),
                        memory_space=pltpu.SMEM)


X_SLOTS = 8
X_AHEAD = 6
Y_SLOTS = 4


TILE_ROWS = MOE_ROWS * SLABS


def _tile_block(ref, g):
    return ref.at[pl.ds(pl.multiple_of(g * TILE_ROWS, TILE_ROWS), TILE_ROWS), :]


def _experts_kernel(blk0_ref, nblk_ref, nused_ref, src_ref, h2t_hbm, rw_ref, wgu_ref, bgu_ref, wdn_ref, bdn_ref,
                    yst_hbm, xbuf, ybuf, wgu_bf, wdn_bf, xsem, ysem):
    e = pl.program_id(0)
    b0, nb, nused = blk0_ref[e], nblk_ref[e], nused_ref[0]

    def row_copy(g, slot, r):
        tok = jnp.minimum(src_ref[g * MOE_ROWS + r], T_ALL - 1)
        dst = xbuf.at[pl.ds(pl.multiple_of(slot * TILE_ROWS + r * SLABS, SLABS), SLABS), :]
        return pltpu.make_async_copy(_token_tile(h2t_hbm, tok), dst, xsem.at[slot])

    def gather_wait(slot):
        pltpu.make_async_copy(_tile_block(h2t_hbm, 0), _tile_block(xbuf, slot), xsem.at[slot]).wait()

    def y_copy(g):
        slot = g % Y_SLOTS
        return pltpu.make_async_copy(_tile_block(ybuf, slot), _tile_block(yst_hbm, g), ysem.at[slot])

    @pl.when(e == 0)
    def _():
        for k in range(X_AHEAD):
            @pl.when(k < nused)
            def _():
                def issue(r, carry):
                    row_copy(k, k, r).start()
                    return carry
                lax.fori_loop(0, MOE_ROWS, issue, 0)

    @pl.when(nb > 0)
    def _():
        wgu_bf[...] = wgu_ref[...].astype(BF16)
        wdn_bf[...] = wdn_ref[...].astype(BF16)

        def one(j, carry):
            g = b0 + j
            slot = g % X_SLOTS
            gather_wait(slot)
            x = jnp.concatenate(
                [xbuf[pl.ds(slot * TILE_ROWS + s, MOE_ROWS, stride=SLABS), :].astype(BF16) for s in range(SLABS)],
                axis=1)

            ahead = g + X_AHEAD
            ahead_slot = jnp.where(ahead < nused, ahead % X_SLOTS, X_SLOTS + ahead - nused)
            ahead_blk = jnp.minimum(ahead, nused - 1)
            for r in range(MOE_ROWS):
                row_copy(ahead_blk, ahead_slot, r).start(priority=r % 2)

            gu = jnp.dot(x, wgu_bf[...], preferred_element_type=F32) + bgu_ref[...]
            x_glu = jnp.minimum(gu[:, :D_FF], SWIGLU_LIMIT)
            x_lin = jnp.clip(gu[:, D_FF:], -SWIGLU_LIMIT, SWIGLU_LIMIT)
            act = x_glu * jax.nn.sigmoid(SWIGLU_ALPHA * x_glu) * (x_lin + 1.0)
            out = jnp.dot(act.astype(BF16), wdn_bf[...], preferred_element_type=F32) + bdn_ref[...]
            rw = jnp.broadcast_to(rw_ref[pl.ds(g, 1), :], (SLABS, MOE_ROWS)).T[:, 0:1]
            out = out * rw

            @pl.when(g >= Y_SLOTS)
            def _():
                y_copy(g - Y_SLOTS).wait()

            yslot = g % Y_SLOTS
            for s in range(SLABS):
                ybuf[pl.ds(yslot * TILE_ROWS + s, MOE_ROWS, stride=SLABS), :] = out[:, s * LANES:(s + 1) * LANES]
            y_copy(g).start()
            return carry

        lax.fori_loop(0, nb, one, 0)

    @pl.when(e == N_EXPERTS - 1)
    def _():
        for k in range(X_AHEAD):
            @pl.when(k < nused)
            def _():
                gather_wait(X_SLOTS + X_AHEAD - 1 - k)

        for k in range(1, Y_SLOTS + 1):
            @pl.when(nused >= k)
            def _():
                y_copy(nused - k).wait()

        _tile_block(ybuf, 0)[...] = jnp.zeros((TILE_ROWS, LANES), F32)

        def fill(g, carry):
            cp = pltpu.make_async_copy(_tile_block(ybuf, 0), _tile_block(yst_hbm, g), ysem.at[0])
            cp.start()
            cp.wait()
            return carry

        lax.fori_loop(nused, N_MOE_BLOCKS, fill, 0)


def _moe_experts(blk0, nblk, n_used, src, h2t, row_w, w_gu, b_gu, w_dn, b_dn, layer):
    wspec = lambda rows, cols: pl.BlockSpec((None, None, rows, cols), lambda e, *_: (layer, e, 0, 0))
    return pl.pallas_call(
        _experts_kernel,
        grid_spec=pltpu.PrefetchScalarGridSpec(
            num_scalar_prefetch=4,
            grid=(N_EXPERTS,),
            in_specs=[pl.BlockSpec(memory_space=pl.ANY), _full((N_MOE_BLOCKS, MOE_ROWS)),
                      wspec(D_MODEL, 2 * D_FF), wspec(1, 2 * D_FF), wspec(D_FF, D_MODEL), wspec(1, D_MODEL)],
            out_specs=pl.BlockSpec(memory_space=pl.ANY),
            scratch_shapes=[pltpu.VMEM(((X_SLOTS + X_AHEAD) * TILE_ROWS, LANES), F32),
                            pltpu.VMEM((Y_SLOTS * TILE_ROWS, LANES), F32),
                            pltpu.VMEM((D_MODEL, 2 * D_FF), BF16), pltpu.VMEM((D_FF, D_MODEL), BF16),
                            pltpu.SemaphoreType.DMA((X_SLOTS + X_AHEAD,)), pltpu.SemaphoreType.DMA((Y_SLOTS,))],
        ),
        out_shape=jax.ShapeDtypeStruct((N_PAD * SLABS, LANES), F32),
        compiler_params=_params(("arbitrary",), vmem=EXPERT_VMEM_LIMIT),
        name="moe_experts",
    )(blk0, nblk, n_used, src, h2t, row_w, w_gu, b_gu.reshape(DEPTH, N_EXPERTS, 1, 2 * D_FF),
      w_dn, b_dn.reshape(DEPTH, N_EXPERTS, 1, D_MODEL))


N_ROW_TILES = T_ALL // ROW_TILE
ZERO_ROWS = ROW_TILE * SLABS


def _combine_kernel(nused_ref, off_ref, ys_ref, x1_ref, g2_ref, o_ref, yres):
    i = pl.program_id(0)

    @pl.when(i == 0)
    def _():
        def zero(j, carry):
            yres[pl.ds(pl.multiple_of(j * ZERO_ROWS, ZERO_ROWS), ZERO_ROWS), :] = jnp.zeros((ZERO_ROWS, LANES), F32)
            return carry
        lax.fori_loop(0, N_ROW_TILES, zero, 0)
        _token_tile(yres, T_ALL)[...] = jnp.zeros((SLABS, LANES), F32)

    @pl.when(jnp.logical_and(i < N_MOE_STEPS, _step_is_used(i, nused_ref)))
    def _():
        def group(g, carry):
            rows = [g * GROUP + j for j in range(GROUP)]
            tiles = [yres.at[pl.ds(pl.multiple_of(off_ref[0, 0, r], SLABS), SLABS), :] for r in rows]
            new = [t[...] + _token_tile(ys_ref, r)[...] for t, r in zip(tiles, rows)]
            for t, v in zip(tiles, new):
                t[...] = v
            return carry

        lax.fori_loop(0, MOE_STEP_ROWS // GROUP, group, 0)

    @pl.when(i >= N_MOE_STEPS)
    def _():
        base = (i - N_MOE_STEPS) * ZERO_ROWS
        for s in range(SLABS):
            cols = slice(s * LANES, (s + 1) * LANES)
            y = yres[pl.ds(base + s, ROW_TILE, stride=SLABS), :]
            o_ref[:, cols] = x1_ref[:, cols] + g2_ref[0][:, cols] * y


def _moe_combine(n_used, off3, ys, x1, mod_l):
    tile = lambda i: jnp.maximum(i - N_MOE_STEPS, 0)
    last_used = lambda nused: jnp.maximum(nused[0] - 1, 0) // MOE_STEP_BLOCKS
    return pl.pallas_call(
        _combine_kernel,
        grid_spec=pltpu.PrefetchScalarGridSpec(
            num_scalar_prefetch=1,
            grid=(N_MOE_STEPS + N_ROW_TILES,),
            in_specs=[_smem_rows(MOE_STEP_ROWS),
                      pl.BlockSpec((MOE_STEP_ROWS * SLABS, LANES), lambda i, nused: (jnp.minimum(i, last_used(nused)), 0)),
                      pl.BlockSpec((ROW_TILE, D_MODEL), lambda i, nused: (tile(i), 0)),
                      pl.BlockSpec((1, 1, D_MODEL), lambda i, nused: (_cond_of_tile(tile(i)), 0, 5))],
            out_specs=pl.BlockSpec((ROW_TILE, D_MODEL), lambda i, nused: (tile(i), 0)),
            scratch_shapes=[pltpu.VMEM((N_TILES * SLABS, LANES), F32)],
        ),
        out_shape=jax.ShapeDtypeStruct((T_ALL, D_MODEL), F32),
        compiler_params=_params(("arbitrary",)),
        name="moe_combine",
    )(n_used, off3, ys, x1, mod_l)


def _routing_tables(top_idx, top_w):
    flat_e = top_idx.reshape(N_ASSIGN)
    experts = jnp.arange(N_EXPERTS, dtype=jnp.int32)
    counts = jnp.sum((flat_e[:, None] == experts[None, :]).astype(jnp.int32), axis=0)
    pad = (-counts) % MOE_ROWS
    spare = jnp.arange(MOE_ROWS, dtype=jnp.int32)
    pad_keys = jnp.where(spare[None, :] < pad[:, None], experts[:, None], N_EXPERTS).reshape(-1)
    keys = jnp.concatenate([flat_e, pad_keys])
    toks = jnp.concatenate([jnp.arange(N_ASSIGN, dtype=jnp.int32) // TOP_K,
                            jnp.full((N_PAD - N_ASSIGN,), T_ALL, jnp.int32)])
    wts = jnp.concatenate([top_w.reshape(N_ASSIGN), jnp.zeros((N_PAD - N_ASSIGN,), F32)])
    _, src, w_sorted = lax.sort((keys, toks, wts), num_keys=1, is_stable=True)
    padded = counts + pad
    blk_end = jnp.cumsum(padded) // MOE_ROWS
    nblk = padded // MOE_ROWS
    return (src, (src * SLABS).reshape(N_MOE_STEPS, 1, MOE_STEP_ROWS), w_sorted.reshape(N_MOE_BLOCKS, MOE_ROWS),
            (blk_end - nblk).astype(jnp.int32), nblk.astype(jnp.int32), blk_end[-1:].astype(jnp.int32))


def kernel(x_prompt, x_sample, cache_attn_k, cache_attn_v, state_ret, c, c_ctx, norm_mix, norm_ffn, w_ada, b_ada,
           w_in, q_norm, k_norm, attn_sink, ret_decay, ret_norm, w_out, w_router, b_router, w_gate_up, b_gate_up,
           w_down, b_down):
    x = jnp.concatenate([x_prompt.reshape(T_CTX, D_MODEL), x_sample.reshape(T_LAT, D_MODEL)], axis=0)
    cond = jnp.zeros((COND_PAD, D_MODEL), F32).at[0].set(c_ctx).at[1:N_COND].set(c)
    mod = _modulation(cond, w_ada, b_ada)[:, :N_COND].reshape(DEPTH, N_COND, 1, 6 * D_MODEL)

    cache_k = cache_attn_k.reshape(DEC_BATCH, DEPTH, PAST_LEN, KV_WIDTH)
    cache_v = cache_attn_v.reshape(DEC_BATCH, DEPTH, PAST_LEN, KV_WIDTH)
    cosq, sinq, cosk, sin_k = _rope_tables()
    grp = jnp.arange(ATTN_WIDTH) // HEAD_DIM
    avgq = jnp.where(grp[:, None] == grp[None, :], 1.0 / HEAD_DIM, 0.0).astype(BF16)
    avgk = avgq[:KV_WIDTH, :KV_WIDTH]
    log_gamma = jax.nn.log_sigmoid(ret_decay.astype(F32))
    chunk_decay = jnp.exp(log_gamma * RET_CHUNK)

    new_k, new_v, new_s = [], [], []
    for l in range(DEPTH):
        mod_l = mod[l]
        q, k, v, rq, rk, rv, sg = _input_projection(
            x, norm_mix[l].reshape(1, D_MODEL), mod_l, w_in[l].astype(BF16),
            jnp.tile(q_norm[l], N_Q_HEADS).reshape(1, ATTN_WIDTH), jnp.tile(k_norm[l], N_KV_HEADS).reshape(1, KV_WIDTH),
            avgq, avgk)
        new_k.append(k[:T_CTX].reshape(BATCH, SEQ, N_KV_HEADS, HEAD_DIM))
        new_v.append(v[:T_CTX].reshape(BATCH, SEQ, N_KV_HEADS, HEAD_DIM))
        sink = attn_sink[l].astype(F32)
        att_c = _context_attention(sink, q, k, v)
        att_l = _latent_attention(sink, q, k, v, cache_k, cache_v, l, cosq, sinq, cosk, sin_k)
        lg = log_gamma[l].reshape(2 * N_RET_HEADS)
        cd = chunk_decay[l].reshape(2 * N_RET_HEADS)
        gn = ret_norm[l].reshape(1, RET_WIDTH)
        ret_c, s_fin = _retention(lg, cd, rq, rk, rv, sg, gn, None, l, n_seq=BATCH, seq_len=SEQ, row0=0,
                                  write_state=True)
        (ret_l,) = _retention(lg, cd, rq, rk, rv, sg, gn, state_ret, l, n_seq=DEC_BATCH, seq_len=DEC_SEQ,
                              row0=T_CTX, write_state=False)
        new_s.append(s_fin)
        wr = jnp.zeros((D_MODEL, LANES), F32).at[:, :N_EXPERTS].set(w_router[l])
        wrh = wr.astype(BF16)
        wrl = (wr - wrh.astype(F32)).astype(BF16)
        br = jnp.full((1, LANES), NEG_BIG, F32).at[0, :N_EXPERTS].set(b_router[l])
        x1, h2t, ti, tw = _output_projection(att_c, att_l, ret_c, ret_l, x, w_out[l].astype(BF16), mod_l,
                                             norm_ffn[l].reshape(1, D_MODEL), wrh, wrl, br)
        src, off3, row_w, blk0, nblk, n_used = _routing_tables(ti[:, :TOP_K], tw[:, :TOP_K])
        ys = _moe_experts(blk0, nblk, n_used, src, h2t, row_w, w_gate_up, b_gate_up, w_down, b_down, l)
        x = _moe_combine(n_used, off3, ys, x1, mod_l)

    y_prompt = x[:T_CTX].reshape(BATCH, SEQ, D_MODEL)
    y_sample = x[T_CTX:].reshape(DEC_BATCH, DEC_SEQ, D_MODEL)
    return (y_prompt, y_sample, jnp.stack(new_k, axis=1), jnp.stack(new_v, axis=1), jnp.stack(new_s, axis=1))
```

```python
import functools

import jax
import jax.numpy as jnp
from jax import lax
from jax.experimental import pallas as pl
from jax.experimental.pallas import tpu as pltpu

F32 = jnp.float32
BF16 = jnp.bfloat16

D_MODEL = 1024
DEPTH = 4
BATCH, SEQ = 16, 256
DEC_BATCH, DEC_SEQ = 2, 1024
PAST_LEN = 512
GRID_W = 64
HEAD_DIM = 64
N_Q_HEADS = 8
N_KV_HEADS = 2
ATTN_WIDTH = N_Q_HEADS * HEAD_DIM
KV_WIDTH = N_KV_HEADS * HEAD_DIM
WINDOW = 128
BLOCK = 128
ROPE_BASE = 10000.0
N_RET_HEADS = 4
RET_DK = 128
RET_WIDTH = N_RET_HEADS * RET_DK
RET_CHUNK = 256
IN_WIDTH = ATTN_WIDTH + 2 * KV_WIDTH + 4 * RET_WIDTH
N_EXPERTS = 32
TOP_K = 4
D_FF = D_MODEL
SWIGLU_LIMIT = 7.0
SWIGLU_ALPHA = 1.702
EPS = 1e-6

T_CTX = BATCH * SEQ
T_LAT = DEC_BATCH * DEC_SEQ
T_ALL = T_CTX + T_LAT
N_COND = 1 + DEC_BATCH
COND_PAD = 8
LANES = 128
ROW_TILE = 512
N_CTX_TILES = T_CTX // ROW_TILE
MOE_ROWS = 128
N_ASSIGN = T_ALL * TOP_K
N_MOE_BLOCKS = N_ASSIGN // MOE_ROWS + N_EXPERTS
N_PAD = N_MOE_BLOCKS * MOE_ROWS
MOE_STEP_BLOCKS = 8
MOE_STEP_ROWS = MOE_STEP_BLOCKS * MOE_ROWS
N_MOE_STEPS = N_MOE_BLOCKS // MOE_STEP_BLOCKS
SLABS = D_MODEL // LANES
GROUP = 8
N_TILES = T_ALL + 1
NEG_BIG = -1e30
VMEM_LIMIT = 48 * 1024 * 1024
EXPERT_VMEM_LIMIT = 56 * 1024 * 1024


def _cond_of_tile(i):
    return jnp.where(i < T_CTX // ROW_TILE, 0, 1 + (i - T_CTX // ROW_TILE) // (DEC_SEQ // ROW_TILE))


def _params(sem, vmem=VMEM_LIMIT):
    return pltpu.CompilerParams(dimension_semantics=sem, vmem_limit_bytes=vmem)


def _mod_kernel(c_ref, w_ref, b_ref, o_ref):
    c = c_ref[...]
    s = (c * jax.nn.sigmoid(c)).astype(BF16)
    o_ref[0] = jnp.dot(s, w_ref[0].astype(BF16), preferred_element_type=F32) + b_ref[0]


def _modulation(cond, w_ada, b_ada):
    n_col = 6 * D_MODEL // D_MODEL
    return pl.pallas_call(
        _mod_kernel,
        grid=(DEPTH, n_col),
        in_specs=[
            pl.BlockSpec((COND_PAD, D_MODEL), lambda l, j: (0, 0)),
            pl.BlockSpec((1, D_MODEL, D_MODEL), lambda l, j: (l, 0, j)),
            pl.BlockSpec((1, 1, D_MODEL), lambda l, j: (l, 0, j)),
        ],
        out_specs=pl.BlockSpec((1, COND_PAD, D_MODEL), lambda l, j: (l, 0, j)),
        out_shape=jax.ShapeDtypeStruct((DEPTH, COND_PAD, 6 * D_MODEL), F32),
        compiler_params=_params(("arbitrary", "arbitrary")),
        name="modulation",
    )(cond, w_ada, b_ada.reshape(DEPTH, 1, 6 * D_MODEL))


def _rms_rows(x, g):
    ms = jnp.mean(x * x, axis=-1, keepdims=True)
    return x * lax.rsqrt(ms + EPS) * g


def _group_rmsnorm(a, avg_ref, g):
    ms = jnp.dot((a * a).astype(BF16), avg_ref[...], preferred_element_type=F32)
    return a * lax.rsqrt(ms + EPS) * g


def _group_rows(ctx_ref, lat_ref):
    return jnp.where(pl.program_id(0) < N_CTX_TILES, ctx_ref[...], lat_ref[...])


def _ctx_rows(width):
    return pl.BlockSpec((ROW_TILE, width), lambda i: (jnp.minimum(i, N_CTX_TILES - 1), 0))


def _lat_rows(width):
    return pl.BlockSpec((ROW_TILE, width), lambda i: (jnp.maximum(i - N_CTX_TILES, 0), 0))


def _inproj_kernel(xc_ref, xl_ref, g_ref, sh_ref, sc_ref, w_ref, qn_ref, kn_ref, avgq_ref, avgk_ref,
                   q_ref, k_ref, v_ref, rq_ref, rk_ref, rv_ref, sg_ref):
    h = _rms_rows(_group_rows(xc_ref, xl_ref), g_ref[...]) * (1.0 + sc_ref[0]) + sh_ref[0]
    hb = h.astype(BF16)

    def proj(lo, width):
        return jnp.dot(hb, w_ref[:, lo:lo + width], preferred_element_type=F32)

    o = 0
    q_ref[...] = _group_rmsnorm(proj(o, ATTN_WIDTH), avgq_ref, qn_ref[...])
    o += ATTN_WIDTH
    k_ref[...] = _group_rmsnorm(proj(o, KV_WIDTH), avgk_ref, kn_ref[...])
    o += KV_WIDTH
    v_ref[...] = proj(o, KV_WIDTH)
    o += KV_WIDTH
    rq_ref[...] = proj(o, RET_WIDTH) * (RET_DK ** -0.5)
    o += RET_WIDTH
    rk_ref[...] = proj(o, RET_WIDTH)
    o += RET_WIDTH
    rv_ref[...] = proj(o, RET_WIDTH)
    o += RET_WIDTH
    rg = proj(o, RET_WIDTH)
    sg_ref[...] = rg * jax.nn.sigmoid(rg)


def _mod_spec(col):
    return pl.BlockSpec((1, 1, D_MODEL), lambda i, col=col: (_cond_of_tile(i), 0, col))


def _full(shape):
    return pl.BlockSpec(shape, lambda *_: (0,) * len(shape))


def _input_projection(x_ctx, x_lat, norm_g, mod_l, w_in_bf, qn, kn, avgq, avgk):
    rows = lambda w: pl.BlockSpec((ROW_TILE, w), lambda i: (i, 0))
    widths = (ATTN_WIDTH, KV_WIDTH, KV_WIDTH, RET_WIDTH, RET_WIDTH, RET_WIDTH, RET_WIDTH)
    return pl.pallas_call(
        _inproj_kernel,
        grid=(T_ALL // ROW_TILE,),
        in_specs=[_ctx_rows(D_MODEL), _lat_rows(D_MODEL), _full((1, D_MODEL)), _mod_spec(0), _mod_spec(1),
                  _full((D_MODEL, IN_WIDTH)), _full((1, ATTN_WIDTH)), _full((1, KV_WIDTH)),
                  _full((ATTN_WIDTH, ATTN_WIDTH)), _full((KV_WIDTH, KV_WIDTH))],
        out_specs=[rows(w) for w in widths],
        out_shape=[jax.ShapeDtypeStruct((T_ALL, w), F32) for w in widths],
        compiler_params=_params(("arbitrary",)),
        name="norm_inproj",
    )(x_ctx, x_lat, norm_g, mod_l, mod_l, w_in_bf, qn, kn, avgq, avgk)


def _attend(q, kall, vall, valid_of, sink_ref, o_ref):
    m_rows, n_keys = q.shape[0], kall.shape[0]
    scale = HEAD_DIM ** -0.5
    lane = lax.broadcasted_iota(jnp.int32, (1, LANES), 1)
    low = lane < HEAD_DIM
    row = lax.broadcasted_iota(jnp.int32, (2 * m_rows, 1), 0)
    second = row >= m_rows
    valid = None
    if valid_of is not None:
        qrow = lax.broadcasted_iota(jnp.int32, (2 * m_rows, n_keys), 0)
        qrow = jnp.where(qrow >= m_rows, qrow - m_rows, qrow)
        valid = valid_of(qrow, lax.broadcasted_iota(jnp.int32, (2 * m_rows, n_keys), 1))
    for g in range(N_KV_HEADS):
        keep = low if g == 0 else jnp.logical_not(low)
        kg = jnp.where(keep, kall, 0.0)
        vg = jnp.where(keep, vall, 0.0)
        kr = pltpu.roll(kg, HEAD_DIM, 1)
        vr = pltpu.roll(vg, HEAD_DIM, 1)
        k_at = (kg, kr) if g == 0 else (kr, kg)
        v_at = (vg, vr) if g == 0 else (vr, vg)
        kcat = jnp.concatenate(k_at, axis=0).astype(BF16)
        qg = jnp.concatenate([q[:, (2 * g + b) * LANES:(2 * g + b + 1) * LANES] for b in range(2)],
                             axis=0).astype(BF16)
        s_all = lax.dot_general(qg, kcat, (((1,), (1,)), ((), ())), preferred_element_type=F32) * scale
        acc = None
        for off in range(2):
            s = s_all[:, off * n_keys:(off + 1) * n_keys]
            if valid is not None:
                s = jnp.where(valid, s, NEG_BIG)
            h0 = 4 * g + off
            sink = jnp.where(second, sink_ref[h0 + 2], sink_ref[h0])
            m = jnp.maximum(jnp.max(s, axis=-1, keepdims=True), sink)
            e = jnp.exp(s - m)
            den = jnp.sum(e, axis=-1, keepdims=True) + jnp.exp(sink - m)
            o = jnp.dot(e.astype(BF16), v_at[off].astype(BF16), preferred_element_type=F32) / den
            acc = o if acc is None else acc + o
        for b in range(2):
            o_ref[:, (2 * g + b) * LANES:(2 * g + b + 1) * LANES] = acc[b * m_rows:(b + 1) * m_rows]


def _ctx_attn_kernel(sink_ref, q_ref, k_ref, v_ref, o_ref):
    _attend(q_ref[...], k_ref[...], v_ref[...], None, sink_ref, o_ref)


def _context_attention(sink, q, k, v):
    return pl.pallas_call(
        _ctx_attn_kernel,
        grid_spec=pltpu.PrefetchScalarGridSpec(
            num_scalar_prefetch=1,
            grid=(BATCH,),
            in_specs=[pl.BlockSpec((SEQ, ATTN_WIDTH), lambda b, s: (b, 0)),
                      pl.BlockSpec((SEQ, KV_WIDTH), lambda b, s: (b, 0)),
                      pl.BlockSpec((SEQ, KV_WIDTH), lambda b, s: (b, 0))],
            out_specs=pl.BlockSpec((SEQ, ATTN_WIDTH), lambda b, s: (b, 0)),
        ),
        out_shape=jax.ShapeDtypeStruct((T_CTX, ATTN_WIDTH), F32),
        compiler_params=_params(("arbitrary",)),
        name="context_attention",
    )(sink, q, k, v)


def _rope_block(x, cos, sin_signed):
    lane = lax.broadcasted_iota(jnp.int32, (1, LANES), 1)
    first = (lane % (HEAD_DIM // 2)) < (HEAD_DIM // 4)
    swapped = jnp.where(first, pltpu.roll(x, LANES - HEAD_DIM // 4, 1), pltpu.roll(x, HEAD_DIM // 4, 1))
    return x * cos + swapped * sin_signed


LOCAL_KEYS = 3 * BLOCK


def _lat_attn_kernel(sink_ref, q_ref, k_ref, v_ref, ck_ref, cv_ref, cosq_ref, sinq_ref, cosk_ref, sin_k_ref,
                     o_ref):
    n = pl.program_id(1)
    start = pl.multiple_of(jnp.clip((n - 1) * BLOCK, 0, DEC_SEQ - LOCAL_KEYS), BLOCK)
    q = q_ref[...]
    q = jnp.concatenate(
        [_rope_block(q[:, j * LANES:(j + 1) * LANES], cosq_ref[:, j * LANES:(j + 1) * LANES],
                     sinq_ref[:, j * LANES:(j + 1) * LANES]) for j in range(ATTN_WIDTH // LANES)], axis=1)
    kw = _rope_block(k_ref[pl.ds(start, LOCAL_KEYS), :], cosk_ref[pl.ds(start, LOCAL_KEYS), :],
                     sin_k_ref[pl.ds(start, LOCAL_KEYS), :])
    vw = v_ref[pl.ds(start, LOCAL_KEYS), :]
    kall = jnp.concatenate([kw, ck_ref[0, 0]], axis=0)
    vall = jnp.concatenate([vw, cv_ref[0, 0]], axis=0)

    def valid_of(qrow, col):
        return jnp.logical_or(col >= LOCAL_KEYS, jnp.abs(n * BLOCK + qrow - (start + col)) <= WINDOW)

    _attend(q, kall, vall, valid_of, sink_ref, o_ref)


def _latent_attention(sink, q, k, v, cache_k, cache_v, layer, cosq, sinq, cosk, sin_k):
    nb = DEC_SEQ // BLOCK
    ctx_block0 = T_CTX // BLOCK
    ctx_seq0 = T_CTX // DEC_SEQ
    cache_spec = pl.BlockSpec((1, 1, PAST_LEN, KV_WIDTH), lambda b, n, s: (b, layer, 0, 0))
    return pl.pallas_call(
        _lat_attn_kernel,
        grid_spec=pltpu.PrefetchScalarGridSpec(
            num_scalar_prefetch=1,
            grid=(DEC_BATCH, nb),
            in_specs=[pl.BlockSpec((BLOCK, ATTN_WIDTH), lambda b, n, s: (ctx_block0 + b * nb + n, 0)),
                      pl.BlockSpec((DEC_SEQ, KV_WIDTH), lambda b, n, s: (ctx_seq0 + b, 0)),
                      pl.BlockSpec((DEC_SEQ, KV_WIDTH), lambda b, n, s: (ctx_seq0 + b, 0)),
                      cache_spec, cache_spec,
                      pl.BlockSpec((BLOCK, ATTN_WIDTH), lambda b, n, s: (n, 0)),
                      pl.BlockSpec((BLOCK, ATTN_WIDTH), lambda b, n, s: (n, 0)),
                      pl.BlockSpec((DEC_SEQ, KV_WIDTH), lambda b, n, s: (0, 0)),
                      pl.BlockSpec((DEC_SEQ, KV_WIDTH), lambda b, n, s: (0, 0))],
            out_specs=pl.BlockSpec((BLOCK, ATTN_WIDTH), lambda b, n, s: (b * nb + n, 0)),
        ),
        out_shape=jax.ShapeDtypeStruct((T_LAT, ATTN_WIDTH), F32),
        compiler_params=_params(("arbitrary", "arbitrary")),
        name="latent_attention",
    )(sink, q, k, v, cache_k, cache_v, cosq, sinq, cosk, sin_k)


def _rope_tables():
    t = jnp.arange(DEC_SEQ)
    nf = HEAD_DIM // 4
    inv = ROPE_BASE ** (-jnp.arange(nf, dtype=F32) / nf)

    def half(coord):
        ang = coord.astype(F32)[:, None] * inv[None, :]
        c, s = jnp.cos(ang), jnp.sin(ang)
        return jnp.concatenate([c, c], axis=1), jnp.concatenate([-s, s], axis=1)

    cr, sr = half(t // GRID_W)
    cc, sc = half(t % GRID_W)
    cos = jnp.concatenate([cr, cc], axis=1)
    sin = jnp.concatenate([sr, sc], axis=1)
    return (jnp.tile(cos, (1, N_Q_HEADS)), jnp.tile(sin, (1, N_Q_HEADS)),
            jnp.tile(cos, (1, N_KV_HEADS)), jnp.tile(sin, (1, N_KV_HEADS)))


def _ret_kernel(lg_ref, cd_ref, q_ref, k_ref, v_ref, sg_ref, gn_ref, *rest, n_chunks, has_s0, write_state):
    rest = list(rest)
    s0_ref = rest.pop(0) if has_s0 else None
    o_ref = rest.pop(0)
    sf_ref = rest.pop(0) if write_state else None
    acc_ref = rest.pop(0)
    row = lax.broadcasted_iota(jnp.int32, (RET_CHUNK, RET_CHUNK), 0).astype(F32)
    col = lax.broadcasted_iota(jnp.int32, (RET_CHUNK, RET_CHUNK), 1).astype(F32)
    rel = row - col
    pos = lax.broadcasted_iota(jnp.int32, (RET_CHUNK, 1), 0).astype(F32)

    def chunk(ref, c, cols):
        return ref[c * RET_CHUNK:(c + 1) * RET_CHUNK, cols]

    def inter(direction, h, c, state):
        cols = slice(h * RET_DK, (h + 1) * RET_DK)
        lg = lg_ref[direction * N_RET_HEADS + h]
        cd = cd_ref[direction * N_RET_HEADS + h]
        if direction == 0:
            q_dec = jnp.exp(lg * (pos + 1.0))
            k_dec = jnp.exp(lg * (RET_CHUNK - 1.0 - pos))
        else:
            q_dec = jnp.exp(lg * (RET_CHUNK - pos))
            k_dec = jnp.exp(lg * pos)
        vb = chunk(v_ref, c, cols).astype(BF16)
        grow = jnp.dot((chunk(k_ref, c, cols) * k_dec).T.astype(BF16), vb, preferred_element_type=F32)
        if state is None:
            return None, grow
        o = jnp.dot(chunk(q_ref, c, cols).astype(BF16), state.astype(BF16), preferred_element_type=F32) * q_dec
        return o, state * cd + grow

    for h in range(N_RET_HEADS):
        cols = slice(h * RET_DK, (h + 1) * RET_DK)
        lgf, lgb = lg_ref[h], lg_ref[N_RET_HEADS + h]
        intra = jnp.where(rel >= 0, jnp.exp(lgf * rel), 0.0) + jnp.where(rel <= 0, jnp.exp(-lgb * rel), 0.0)
        state = s0_ref[0, 0, 0, h] if has_s0 else None
        for c in range(n_chunks):
            qb, kb = chunk(q_ref, c, cols).astype(BF16), chunk(k_ref, c, cols).astype(BF16)
            scores = lax.dot_general(qb, kb, (((1,), (1,)), ((), ())), preferred_element_type=F32) * intra
            o = jnp.dot(scores.astype(BF16), chunk(v_ref, c, cols).astype(BF16), preferred_element_type=F32)
            o_fwd, state = inter(0, h, c, state)
            acc_ref[c * RET_CHUNK:(c + 1) * RET_CHUNK, cols] = o if o_fwd is None else o + o_fwd
        if write_state:
            sf_ref[0, 0, h] = state
        state = s0_ref[0, 0, 1, h] if has_s0 else None
        gn = gn_ref[:, cols]
        for c in range(n_chunks - 1, -1, -1):
            o_bwd, state = inter(1, h, c, state)
            tot = chunk(acc_ref, c, cols)
            if o_bwd is not None:
                tot = tot + o_bwd
            o_ref[c * RET_CHUNK:(c + 1) * RET_CHUNK, cols] = _rms_rows(tot, gn) * chunk(sg_ref, c, cols)
        if write_state:
            sf_ref[0, 1, h] = state


def _retention(lg, cd, rq, rk, rv, sg, gn, s0, layer, *, n_seq, seq_len, row0, write_state):
    blk0 = row0 // seq_len
    rows = pl.BlockSpec((seq_len, RET_WIDTH), lambda b, *_: (blk0 + b, 0))
    in_specs = [rows, rows, rows, rows, pl.BlockSpec((1, RET_WIDTH), lambda b, *_: (0, 0))]
    args = [rq, rk, rv, sg, gn]
    if s0 is not None:
        in_specs.append(pl.BlockSpec((1, 1, 2, N_RET_HEADS, RET_DK, RET_DK), lambda b, *_: (b, layer, 0, 0, 0, 0)))
        args.append(s0)
    out_specs = [pl.BlockSpec((seq_len, RET_WIDTH), lambda b, *_: (b, 0))]
    out_shape = [jax.ShapeDtypeStruct((n_seq * seq_len, RET_WIDTH), F32)]
    if write_state:
        out_specs.append(pl.BlockSpec((1, 2, N_RET_HEADS, RET_DK, RET_DK), lambda b, *_: (b, 0, 0, 0, 0)))
        out_shape.append(jax.ShapeDtypeStruct((n_seq, 2, N_RET_HEADS, RET_DK, RET_DK), F32))
    kern = functools.partial(_ret_kernel, n_chunks=seq_len // RET_CHUNK, has_s0=s0 is not None,
                             write_state=write_state)
    return pl.pallas_call(
        kern,
        grid_spec=pltpu.PrefetchScalarGridSpec(
            num_scalar_prefetch=2,
            grid=(n_seq,),
            in_specs=in_specs,
            out_specs=out_specs,
            scratch_shapes=[pltpu.VMEM((seq_len, RET_WIDTH), F32)],
        ),
        out_shape=out_shape,
        compiler_params=_params(("arbitrary",)),
        name="retention_ctx" if write_state else "retention_lat",
    )(lg, cd, *args)


def _outproj_kernel(attc_ref, attl_ref, retc_ref, retl_ref, xc_ref, xl_ref, wo_ref, g1_ref, sh2_ref, sc2_ref, nf_ref,
                    wrh_ref, wrl_ref, br_ref, x1_ref, h2t_ref, ti_ref, tw_ref):
    att = _group_rows(attc_ref, attl_ref)
    ret = _group_rows(retc_ref, retl_ref)
    y = (jnp.dot(att.astype(BF16), wo_ref[0:ATTN_WIDTH, :], preferred_element_type=F32)
         + jnp.dot(ret.astype(BF16), wo_ref[ATTN_WIDTH:, :], preferred_element_type=F32))
    x1 = _group_rows(xc_ref, xl_ref) + g1_ref[0] * y
    x1_ref[...] = x1
    h2 = _rms_rows(x1, nf_ref[...]) * (1.0 + sc2_ref[0]) + sh2_ref[0]
    for s in range(SLABS):
        h2t_ref[pl.ds(s, ROW_TILE, stride=SLABS), :] = h2[:, s * LANES:(s + 1) * LANES]
    hh = h2.astype(BF16)
    hl = (h2 - hh.astype(F32)).astype(BF16)
    wrh = wrh_ref[...]
    logits = (jnp.dot(hh, wrh, preferred_element_type=F32) + jnp.dot(hl, wrh, preferred_element_type=F32)
              + jnp.dot(hh, wrl_ref[...], preferred_element_type=F32) + br_ref[...])
    lane = lax.broadcasted_iota(jnp.int32, logits.shape, 1)
    vals, idxs = [], []
    cur = logits
    for _ in range(TOP_K):
        m = jnp.max(cur, axis=-1, keepdims=True)
        idx = jnp.min(jnp.where(cur == m, lane, LANES), axis=-1, keepdims=True)
        vals.append(m)
        idxs.append(idx)
        cur = jnp.where(lane == idx, -jnp.inf, cur)
    es = [jnp.exp(v - vals[0]) for v in vals]
    den = es[0] + es[1] + es[2] + es[3]
    ti = jnp.zeros(logits.shape, jnp.int32)
    tw = jnp.zeros(logits.shape, F32)
    for k in range(TOP_K):
        ti = jnp.where(lane == k, idxs[k], ti)
        tw = jnp.where(lane == k, es[k] / den, tw)
    ti_ref[...] = ti
    tw_ref[...] = tw


def _output_projection(att_c, att_l, ret_c, ret_l, x_ctx, x_lat, w_out_bf, mod_l, nf, wrh, wrl, br):
    rows = lambda w: pl.BlockSpec((ROW_TILE, w), lambda i: (i, 0))
    return pl.pallas_call(
        _outproj_kernel,
        grid=(T_ALL // ROW_TILE,),
        in_specs=[_ctx_rows(ATTN_WIDTH), _lat_rows(ATTN_WIDTH), _ctx_rows(RET_WIDTH), _lat_rows(RET_WIDTH),
                  _ctx_rows(D_MODEL), _lat_rows(D_MODEL), _full((D_MODEL, D_MODEL)),
                  _mod_spec(2), _mod_spec(3), _mod_spec(4), _full((1, D_MODEL)),
                  _full((D_MODEL, LANES)), _full((D_MODEL, LANES)), _full((1, LANES))],
        out_specs=[rows(D_MODEL), pl.BlockSpec((ROW_TILE * SLABS, LANES), lambda i: (i, 0)), rows(LANES), rows(LANES)],
        out_shape=[jax.ShapeDtypeStruct((T_ALL, D_MODEL), F32), jax.ShapeDtypeStruct((T_ALL * SLABS, LANES), F32),
                   jax.ShapeDtypeStruct((T_ALL, LANES), jnp.int32), jax.ShapeDtypeStruct((T_ALL, LANES), F32)],
        compiler_params=_params(("arbitrary",)),
        name="outproj_router",
    )(att_c, att_l, ret_c, ret_l, x_ctx, x_lat, w_out_bf, mod_l, mod_l, mod_l, nf, wrh, wrl, br)


def _token_tile(ref, t):
    return ref.at[pl.ds(pl.multiple_of(t * SLABS, SLABS), SLABS), :]


def _step_is_used(i, nused_ref):
    return i * MOE_STEP_BLOCKS < nused_ref[0]


def _smem_rows(width):
    return pl.BlockSpec((1, 1, width), lambda i, nused: (jnp.minimum(i, N_MOE_STEPS - 1), 0, 0),
                        memory_space=pltpu.SMEM)


X_SLOTS = 8
X_AHEAD = 6
Y_SLOTS = 4


TILE_ROWS = MOE_ROWS * SLABS


def _tile_block(ref, g):
    return ref.at[pl.ds(pl.multiple_of(g * TILE_ROWS, TILE_ROWS), TILE_ROWS), :]


def _experts_kernel(blk0_ref, nblk_ref, nused_ref, src_ref, h2t_hbm, rw_ref, wgu_ref, bgu_ref, wdn_ref, bdn_ref,
                    yst_hbm, xbuf, ybuf, wgu_bf, wdn_bf, xsem, ysem):
    e = pl.program_id(0)
    b0, nb, nused = blk0_ref[e], nblk_ref[e], nused_ref[0]

    def row_copy(g, slot, r):
        tok = jnp.minimum(src_ref[g * MOE_ROWS + r], T_ALL - 1)
        dst = xbuf.at[pl.ds(pl.multiple_of(slot * TILE_ROWS + r * SLABS, SLABS), SLABS), :]
        return pltpu.make_async_copy(_token_tile(h2t_hbm, tok), dst, xsem.at[slot])

    def gather_wait(slot):
        pltpu.make_async_copy(_tile_block(h2t_hbm, 0), _tile_block(xbuf, slot), xsem.at[slot]).wait()

    def y_copy(g):
        slot = g % Y_SLOTS
        return pltpu.make_async_copy(_tile_block(ybuf, slot), _tile_block(yst_hbm, g), ysem.at[slot])

    @pl.when(e == 0)
    def _():
        for k in range(X_AHEAD):
            @pl.when(k < nused)
            def _():
                def issue(r, carry):
                    row_copy(k, k, r).start()
                    return carry
                lax.fori_loop(0, MOE_ROWS, issue, 0)

    @pl.when(nb > 0)
    def _():
        wgu_bf[...] = wgu_ref[...].astype(BF16)
        wdn_bf[...] = wdn_ref[...].astype(BF16)

        def process(blocks):
            xs, rws = [], []
            for g in blocks:
                slot = g % X_SLOTS
                gather_wait(slot)
                xs.append(jnp.concatenate(
                    [xbuf[pl.ds(slot * TILE_ROWS + s, MOE_ROWS, stride=SLABS), :].astype(BF16)
                     for s in range(SLABS)], axis=1))
                rws.append(jnp.broadcast_to(rw_ref[pl.ds(g, 1), :], (SLABS, MOE_ROWS)).T[:, 0:1])

            for g in blocks:
                ahead = g + X_AHEAD
                ahead_slot = jnp.where(ahead < nused, ahead % X_SLOTS, X_SLOTS + ahead - nused)
                ahead_blk = jnp.minimum(ahead, nused - 1)
                for r in range(MOE_ROWS):
                    row_copy(ahead_blk, ahead_slot, r).start(priority=r % 2)

            x = xs[0] if len(xs) == 1 else jnp.concatenate(xs, axis=0)
            rw = rws[0] if len(rws) == 1 else jnp.concatenate(rws, axis=0)
            gu = jnp.dot(x, wgu_bf[...], preferred_element_type=F32) + bgu_ref[...]
            x_glu = jnp.minimum(gu[:, :D_FF], SWIGLU_LIMIT)
            x_lin = jnp.clip(gu[:, D_FF:], -SWIGLU_LIMIT, SWIGLU_LIMIT)
            act = x_glu * jax.nn.sigmoid(SWIGLU_ALPHA * x_glu) * (x_lin + 1.0)
            out = (jnp.dot(act.astype(BF16), wdn_bf[...], preferred_element_type=F32) + bdn_ref[...]) * rw

            for g in blocks:
                @pl.when(g >= Y_SLOTS)
                def _():
                    y_copy(g - Y_SLOTS).wait()

            for i, g in enumerate(blocks):
                yslot = g % Y_SLOTS
                for s in range(SLABS):
                    ybuf[pl.ds(yslot * TILE_ROWS + s, MOE_ROWS, stride=SLABS), :] = (
                        out[i * MOE_ROWS:(i + 1) * MOE_ROWS, s * LANES:(s + 1) * LANES])
                y_copy(g).start()

        def one(j, carry):
            process([b0 + j])
            return carry

        lax.fori_loop(0, nb, one, 0)

    @pl.when(e == N_EXPERTS - 1)
    def _():
        for k in range(X_AHEAD):
            @pl.when(k < nused)
            def _():
                gather_wait(X_SLOTS + X_AHEAD - 1 - k)

        for k in range(1, Y_SLOTS + 1):
            @pl.when(nused >= k)
            def _():
                y_copy(nused - k).wait()

        _tile_block(ybuf, 0)[...] = jnp.zeros((TILE_ROWS, LANES), F32)

        def fill(g, carry):
            cp = pltpu.make_async_copy(_tile_block(ybuf, 0), _tile_block(yst_hbm, g), ysem.at[0])
            cp.start()
            cp.wait()
            return carry

        lax.fori_loop(nused, N_MOE_BLOCKS, fill, 0)


def _moe_experts(blk0, nblk, n_used, src, h2t, row_w, w_gu, b_gu, w_dn, b_dn, layer):
    wspec = lambda rows, cols: pl.BlockSpec((None, None, rows, cols), lambda e, *_: (layer, e, 0, 0))
    return pl.pallas_call(
        _experts_kernel,
        grid_spec=pltpu.PrefetchScalarGridSpec(
            num_scalar_prefetch=4,
            grid=(N_EXPERTS,),
            in_specs=[pl.BlockSpec(memory_space=pl.ANY), _full((N_MOE_BLOCKS, MOE_ROWS)),
                      wspec(D_MODEL, 2 * D_FF), wspec(1, 2 * D_FF), wspec(D_FF, D_MODEL), wspec(1, D_MODEL)],
            out_specs=pl.BlockSpec(memory_space=pl.ANY),
            scratch_shapes=[pltpu.VMEM(((X_SLOTS + X_AHEAD) * TILE_ROWS, LANES), F32),
                            pltpu.VMEM((Y_SLOTS * TILE_ROWS, LANES), F32),
                            pltpu.VMEM((D_MODEL, 2 * D_FF), BF16), pltpu.VMEM((D_FF, D_MODEL), BF16),
                            pltpu.SemaphoreType.DMA((X_SLOTS + X_AHEAD,)), pltpu.SemaphoreType.DMA((Y_SLOTS,))],
        ),
        out_shape=jax.ShapeDtypeStruct((N_PAD * SLABS, LANES), F32),
        compiler_params=_params(("arbitrary",), vmem=EXPERT_VMEM_LIMIT),
        name="moe_experts",
    )(blk0, nblk, n_used, src, h2t, row_w, w_gu, b_gu.reshape(DEPTH, N_EXPERTS, 1, 2 * D_FF),
      w_dn, b_dn.reshape(DEPTH, N_EXPERTS, 1, D_MODEL))


N_ROW_TILES = T_ALL // ROW_TILE
ZERO_ROWS = ROW_TILE * SLABS


def _combine_kernel(nused_ref, off_ref, ys_ref, x1_ref, g2_ref, oc_ref, ol_ref, yres):
    i = pl.program_id(0)

    @pl.when(i == 0)
    def _():
        def zero(j, carry):
            yres[pl.ds(pl.multiple_of(j * ZERO_ROWS, ZERO_ROWS), ZERO_ROWS), :] = jnp.zeros((ZERO_ROWS, LANES), F32)
            return carry
        lax.fori_loop(0, N_ROW_TILES, zero, 0)
        _token_tile(yres, T_ALL)[...] = jnp.zeros((SLABS, LANES), F32)

    @pl.when(jnp.logical_and(i < N_MOE_STEPS, _step_is_used(i, nused_ref)))
    def _():
        def group(g, carry):
            rows = [g * GROUP + j for j in range(GROUP)]
            tiles = [yres.at[pl.ds(pl.multiple_of(off_ref[0, 0, r], SLABS), SLABS), :] for r in rows]
            new = [t[...] + _token_tile(ys_ref, r)[...] for t, r in zip(tiles, rows)]
            for t, v in zip(tiles, new):
                t[...] = v
            return carry

        lax.fori_loop(0, MOE_STEP_ROWS // GROUP, group, 0)

    def finalize(o_ref):
        base = (i - N_MOE_STEPS) * ZERO_ROWS
        for s in range(SLABS):
            cols = slice(s * LANES, (s + 1) * LANES)
            y = yres[pl.ds(base + s, ROW_TILE, stride=SLABS), :]
            o_ref[:, cols] = x1_ref[:, cols] + g2_ref[0][:, cols] * y

    @pl.when(jnp.logical_and(i >= N_MOE_STEPS, i < N_MOE_STEPS + N_CTX_TILES))
    def _():
        finalize(oc_ref)

    @pl.when(i >= N_MOE_STEPS + N_CTX_TILES)
    def _():
        finalize(ol_ref)


def _moe_combine(n_used, off3, ys, x1, mod_l):
    tile = lambda i: jnp.maximum(i - N_MOE_STEPS, 0)
    ctx_tile = lambda i: jnp.minimum(tile(i), N_CTX_TILES - 1)
    lat_tile = lambda i: jnp.maximum(tile(i) - N_CTX_TILES, 0)
    last_used = lambda nused: jnp.maximum(nused[0] - 1, 0) // MOE_STEP_BLOCKS
    return pl.pallas_call(
        _combine_kernel,
        grid_spec=pltpu.PrefetchScalarGridSpec(
            num_scalar_prefetch=1,
            grid=(N_MOE_STEPS + N_ROW_TILES,),
            in_specs=[_smem_rows(MOE_STEP_ROWS),
                      pl.BlockSpec((MOE_STEP_ROWS * SLABS, LANES), lambda i, nused: (jnp.minimum(i, last_used(nused)), 0)),
                      pl.BlockSpec((ROW_TILE, D_MODEL), lambda i, nused: (tile(i), 0)),
                      pl.BlockSpec((1, 1, D_MODEL), lambda i, nused: (_cond_of_tile(tile(i)), 0, 5))],
            out_specs=[pl.BlockSpec((ROW_TILE, D_MODEL), lambda i, nused: (ctx_tile(i), 0)),
                       pl.BlockSpec((ROW_TILE, D_MODEL), lambda i, nused: (lat_tile(i), 0))],
            scratch_shapes=[pltpu.VMEM((N_TILES * SLABS, LANES), F32)],
        ),
        out_shape=[jax.ShapeDtypeStruct((T_CTX, D_MODEL), F32), jax.ShapeDtypeStruct((T_LAT, D_MODEL), F32)],
        compiler_params=_params(("arbitrary",), vmem=EXPERT_VMEM_LIMIT),
        name="moe_combine",
    )(n_used, off3, ys, x1, mod_l)


def _routing_tables(top_idx, top_w):
    flat_e = top_idx.reshape(N_ASSIGN)
    experts = jnp.arange(N_EXPERTS, dtype=jnp.int32)
    counts = jnp.sum((flat_e[:, None] == experts[None, :]).astype(jnp.int32), axis=0)
    pad = (-counts) % MOE_ROWS
    spare = jnp.arange(MOE_ROWS, dtype=jnp.int32)
    pad_keys = jnp.where(spare[None, :] < pad[:, None], experts[:, None], N_EXPERTS).reshape(-1)
    keys = jnp.concatenate([flat_e, pad_keys])
    toks = jnp.concatenate([jnp.arange(N_ASSIGN, dtype=jnp.int32) // TOP_K,
                            jnp.full((N_PAD - N_ASSIGN,), T_ALL, jnp.int32)])
    wts = jnp.concatenate([top_w.reshape(N_ASSIGN), jnp.zeros((N_PAD - N_ASSIGN,), F32)])
    _, src, w_sorted = lax.sort((keys, toks, wts), num_keys=1, is_stable=True)
    padded = counts + pad
    blk_end = jnp.cumsum(padded) // MOE_ROWS
    nblk = padded // MOE_ROWS
    return (src, (src * SLABS).reshape(N_MOE_STEPS, 1, MOE_STEP_ROWS), w_sorted.reshape(N_MOE_BLOCKS, MOE_ROWS),
            (blk_end - nblk).astype(jnp.int32), nblk.astype(jnp.int32), blk_end[-1:].astype(jnp.int32))


def kernel(x_prompt, x_sample, cache_attn_k, cache_attn_v, state_ret, c, c_ctx, norm_mix, norm_ffn, w_ada, b_ada,
           w_in, q_norm, k_norm, attn_sink, ret_decay, ret_norm, w_out, w_router, b_router, w_gate_up, b_gate_up,
           w_down, b_down):
    x_ctx, x_lat = x_prompt.reshape(T_CTX, D_MODEL), x_sample.reshape(T_LAT, D_MODEL)
    cond = jnp.zeros((COND_PAD, D_MODEL), F32).at[0].set(c_ctx).at[1:N_COND].set(c)
    mod = _modulation(cond, w_ada, b_ada)[:, :N_COND].reshape(DEPTH, N_COND, 1, 6 * D_MODEL)

    cache_k = cache_attn_k.reshape(DEC_BATCH, DEPTH, PAST_LEN, KV_WIDTH)
    cache_v = cache_attn_v.reshape(DEC_BATCH, DEPTH, PAST_LEN, KV_WIDTH)
    cosq, sinq, cosk, sin_k = _rope_tables()
    grp = jnp.arange(ATTN_WIDTH) // HEAD_DIM
    avgq = jnp.where(grp[:, None] == grp[None, :], 1.0 / HEAD_DIM, 0.0).astype(BF16)
    avgk = avgq[:KV_WIDTH, :KV_WIDTH]
    log_gamma = jax.nn.log_sigmoid(ret_decay.astype(F32))
    chunk_decay = jnp.exp(log_gamma * RET_CHUNK)

    new_k, new_v, new_s = [], [], []
    for l in range(DEPTH):
        mod_l = mod[l]
        q, k, v, rq, rk, rv, sg = _input_projection(
            x_ctx, x_lat, norm_mix[l].reshape(1, D_MODEL), mod_l, w_in[l].astype(BF16),
            jnp.tile(q_norm[l], N_Q_HEADS).reshape(1, ATTN_WIDTH), jnp.tile(k_norm[l], N_KV_HEADS).reshape(1, KV_WIDTH),
            avgq, avgk)
        new_k.append(k[:T_CTX].reshape(BATCH, SEQ, N_KV_HEADS, HEAD_DIM))
        new_v.append(v[:T_CTX].reshape(BATCH, SEQ, N_KV_HEADS, HEAD_DIM))
        sink = attn_sink[l].astype(F32)
        att_c = _context_attention(sink, q, k, v)
        att_l = _latent_attention(sink, q, k, v, cache_k, cache_v, l, cosq, sinq, cosk, sin_k)
        lg = log_gamma[l].reshape(2 * N_RET_HEADS)
        cd = chunk_decay[l].reshape(2 * N_RET_HEADS)
        gn = ret_norm[l].reshape(1, RET_WIDTH)
        ret_c, s_fin = _retention(lg, cd, rq, rk, rv, sg, gn, None, l, n_seq=BATCH, seq_len=SEQ, row0=0,
                                  write_state=True)
        (ret_l,) = _retention(lg, cd, rq, rk, rv, sg, gn, state_ret, l, n_seq=DEC_BATCH, seq_len=DEC_SEQ,
                              row0=T_CTX, write_state=False)
        new_s.append(s_fin)
        wr = jnp.zeros((D_MODEL, LANES), F32).at[:, :N_EXPERTS].set(w_router[l])
        wrh = wr.astype(BF16)
        wrl = (wr - wrh.astype(F32)).astype(BF16)
        br = jnp.full((1, LANES), NEG_BIG, F32).at[0, :N_EXPERTS].set(b_router[l])
        x1, h2t, ti, tw = _output_projection(att_c, att_l, ret_c, ret_l, x_ctx, x_lat, w_out[l].astype(BF16), mod_l,
                                             norm_ffn[l].reshape(1, D_MODEL), wrh, wrl, br)
        src, off3, row_w, blk0, nblk, n_used = _routing_tables(ti[:, :TOP_K], tw[:, :TOP_K])
        ys = _moe_experts(blk0, nblk, n_used, src, h2t, row_w, w_gate_up, b_gate_up, w_down, b_down, l)
        x_ctx, x_lat = _moe_combine(n_used, off3, ys, x1, mod_l)

    y_prompt = x_ctx.reshape(BATCH, SEQ, D_MODEL)
    y_sample = x_lat.reshape(DEC_BATCH, DEC_SEQ, D_MODEL)
    return (y_prompt, y_sample, jnp.stack(new_k, axis=1), jnp.stack(new_v, axis=1), jnp.stack(new_s, axis=1))
```

```python
import functools

import jax
import jax.numpy as jnp
from jax import lax
from jax.experimental import pallas as pl
from jax.experimental.pallas import tpu as pltpu

F32 = jnp.float32
BF16 = jnp.bfloat16

D_MODEL = 1024
DEPTH = 4
BATCH, SEQ = 16, 256
DEC_BATCH, DEC_SEQ = 2, 1024
PAST_LEN = 512
GRID_W = 64
HEAD_DIM = 64
N_Q_HEADS = 8
N_KV_HEADS = 2
ATTN_WIDTH = N_Q_HEADS * HEAD_DIM
KV_WIDTH = N_KV_HEADS * HEAD_DIM
WINDOW = 128
BLOCK = 128
ROPE_BASE = 10000.0
N_RET_HEADS = 4
RET_DK = 128
RET_WIDTH = N_RET_HEADS * RET_DK
RET_CHUNK = 256
IN_WIDTH = ATTN_WIDTH + 2 * KV_WIDTH + 4 * RET_WIDTH
N_EXPERTS = 32
TOP_K = 4
D_FF = D_MODEL
SWIGLU_LIMIT = 7.0
SWIGLU_ALPHA = 1.702
EPS = 1e-6

T_CTX = BATCH * SEQ
T_LAT = DEC_BATCH * DEC_SEQ
T_ALL = T_CTX + T_LAT
N_COND = 1 + DEC_BATCH
COND_PAD = 8
LANES = 128
ROW_TILE = 512
N_CTX_TILES = T_CTX // ROW_TILE
MOE_ROWS = 128
N_ASSIGN = T_ALL * TOP_K
N_MOE_BLOCKS = N_ASSIGN // MOE_ROWS + N_EXPERTS
N_PAD = N_MOE_BLOCKS * MOE_ROWS
MOE_STEP_BLOCKS = 8
MOE_STEP_ROWS = MOE_STEP_BLOCKS * MOE_ROWS
N_MOE_STEPS = N_MOE_BLOCKS // MOE_STEP_BLOCKS
SLABS = D_MODEL // LANES
GROUP = 8
N_TILES = T_ALL + 1
NEG_BIG = -1e30
VMEM_LIMIT = 48 * 1024 * 1024
EXPERT_VMEM_LIMIT = 56 * 1024 * 1024


def _cond_of_tile(i):
    return jnp.where(i < T_CTX // ROW_TILE, 0, 1 + (i - T_CTX // ROW_TILE) // (DEC_SEQ // ROW_TILE))


def _params(sem, vmem=VMEM_LIMIT):
    return pltpu.CompilerParams(dimension_semantics=sem, vmem_limit_bytes=vmem)


def _mod_kernel(c_ref, w_ref, b_ref, o_ref):
    c = c_ref[...]
    s = (c * jax.nn.sigmoid(c)).astype(BF16)
    o_ref[0] = jnp.dot(s, w_ref[0].astype(BF16), preferred_element_type=F32) + b_ref[0]


def _modulation(cond, w_ada, b_ada):
    n_col = 6 * D_MODEL // D_MODEL
    return pl.pallas_call(
        _mod_kernel,
        grid=(DEPTH, n_col),
        in_specs=[
            pl.BlockSpec((COND_PAD, D_MODEL), lambda l, j: (0, 0)),
            pl.BlockSpec((1, D_MODEL, D_MODEL), lambda l, j: (l, 0, j)),
            pl.BlockSpec((1, 1, D_MODEL), lambda l, j: (l, 0, j)),
        ],
        out_specs=pl.BlockSpec((1, COND_PAD, D_MODEL), lambda l, j: (l, 0, j)),
        out_shape=jax.ShapeDtypeStruct((DEPTH, COND_PAD, 6 * D_MODEL), F32),
        compiler_params=_params(("arbitrary", "arbitrary")),
        name="modulation",
    )(cond, w_ada, b_ada.reshape(DEPTH, 1, 6 * D_MODEL))


def _rms_rows(x, g):
    ms = jnp.mean(x * x, axis=-1, keepdims=True)
    return x * lax.rsqrt(ms + EPS) * g


def _group_rmsnorm(a, avg_ref, g):
    ms = jnp.dot((a * a).astype(BF16), avg_ref[...], preferred_element_type=F32)
    return a * lax.rsqrt(ms + EPS) * g


def _group_rows(ctx_ref, lat_ref):
    return jnp.where(pl.program_id(0) < N_CTX_TILES, ctx_ref[...], lat_ref[...])


def _ctx_rows(width):
    return pl.BlockSpec((ROW_TILE, width), lambda i: (jnp.minimum(i, N_CTX_TILES - 1), 0))


def _lat_rows(width):
    return pl.BlockSpec((ROW_TILE, width), lambda i: (jnp.maximum(i - N_CTX_TILES, 0), 0))


def _inproj_kernel(xc_ref, xl_ref, g_ref, sh_ref, sc_ref, w_ref, qn_ref, kn_ref, avgq_ref, avgk_ref,
                   q_ref, k_ref, v_ref, rq_ref, rk_ref, rv_ref, sg_ref):
    h = _rms_rows(_group_rows(xc_ref, xl_ref), g_ref[...]) * (1.0 + sc_ref[0]) + sh_ref[0]
    hb = h.astype(BF16)

    def proj(lo, width):
        return jnp.dot(hb, w_ref[:, lo:lo + width], preferred_element_type=F32)

    o = 0
    q_ref[...] = _group_rmsnorm(proj(o, ATTN_WIDTH), avgq_ref, qn_ref[...])
    o += ATTN_WIDTH
    k_ref[...] = _group_rmsnorm(proj(o, KV_WIDTH), avgk_ref, kn_ref[...])
    o += KV_WIDTH
    v_ref[...] = proj(o, KV_WIDTH)
    o += KV_WIDTH
    rq_ref[...] = proj(o, RET_WIDTH) * (RET_DK ** -0.5)
    o += RET_WIDTH
    rk_ref[...] = proj(o, RET_WIDTH)
    o += RET_WIDTH
    rv_ref[...] = proj(o, RET_WIDTH)
    o += RET_WIDTH
    rg = proj(o, RET_WIDTH)
    sg_ref[...] = rg * jax.nn.sigmoid(rg)


def _mod_spec(col):
    return pl.BlockSpec((1, 1, D_MODEL), lambda i, col=col: (_cond_of_tile(i), 0, col))


def _full(shape):
    return pl.BlockSpec(shape, lambda *_: (0,) * len(shape))


def _input_projection(x_ctx, x_lat, norm_g, mod_l, w_in_bf, qn, kn, avgq, avgk):
    rows = lambda w: pl.BlockSpec((ROW_TILE, w), lambda i: (i, 0))
    widths = (ATTN_WIDTH, KV_WIDTH, KV_WIDTH, RET_WIDTH, RET_WIDTH, RET_WIDTH, RET_WIDTH)
    return pl.pallas_call(
        _inproj_kernel,
        grid=(T_ALL // ROW_TILE,),
        in_specs=[_ctx_rows(D_MODEL), _lat_rows(D_MODEL), _full((1, D_MODEL)), _mod_spec(0), _mod_spec(1),
                  _full((D_MODEL, IN_WIDTH)), _full((1, ATTN_WIDTH)), _full((1, KV_WIDTH)),
                  _full((ATTN_WIDTH, ATTN_WIDTH)), _full((KV_WIDTH, KV_WIDTH))],
        out_specs=[rows(w) for w in widths],
        out_shape=[jax.ShapeDtypeStruct((T_ALL, w), F32) for w in widths],
        compiler_params=_params(("arbitrary",)),
        name="norm_inproj",
    )(x_ctx, x_lat, norm_g, mod_l, mod_l, w_in_bf, qn, kn, avgq, avgk)


def _attend(q, kall, vall, valid_of, sink_ref, o_ref):
    m_rows, n_keys = q.shape[0], kall.shape[0]
    scale = HEAD_DIM ** -0.5
    lane = lax.broadcasted_iota(jnp.int32, (1, LANES), 1)
    low = lane < HEAD_DIM
    row = lax.broadcasted_iota(jnp.int32, (2 * m_rows, 1), 0)
    second = row >= m_rows
    valid = None
    if valid_of is not None:
        qrow = lax.broadcasted_iota(jnp.int32, (2 * m_rows, n_keys), 0)
        qrow = jnp.where(qrow >= m_rows, qrow - m_rows, qrow)
        valid = valid_of(qrow, lax.broadcasted_iota(jnp.int32, (2 * m_rows, n_keys), 1))
    for g in range(N_KV_HEADS):
        keep = low if g == 0 else jnp.logical_not(low)
        kg = jnp.where(keep, kall, 0.0)
        vg = jnp.where(keep, vall, 0.0)
        kr = pltpu.roll(kg, HEAD_DIM, 1)
        vr = pltpu.roll(vg, HEAD_DIM, 1)
        k_at = (kg, kr) if g == 0 else (kr, kg)
        v_at = (vg, vr) if g == 0 else (vr, vg)
        kcat = jnp.concatenate(k_at, axis=0).astype(BF16)
        qg = jnp.concatenate([q[:, (2 * g + b) * LANES:(2 * g + b + 1) * LANES] for b in range(2)],
                             axis=0).astype(BF16)
        s_all = lax.dot_general(qg, kcat, (((1,), (1,)), ((), ())), preferred_element_type=F32) * scale
        acc = None
        for off in range(2):
            s = s_all[:, off * n_keys:(off + 1) * n_keys]
            if valid is not None:
                s = jnp.where(valid, s, NEG_BIG)
            h0 = 4 * g + off
            sink = jnp.where(second, sink_ref[h0 + 2], sink_ref[h0])
            m = jnp.maximum(jnp.max(s, axis=-1, keepdims=True), sink)
            e = jnp.exp(s - m)
            den = jnp.sum(e, axis=-1, keepdims=True) + jnp.exp(sink - m)
            o = jnp.dot(e.astype(BF16), v_at[off].astype(BF16), preferred_element_type=F32) / den
            acc = o if acc is None else acc + o
        for b in range(2):
            o_ref[:, (2 * g + b) * LANES:(2 * g + b + 1) * LANES] = acc[b * m_rows:(b + 1) * m_rows]


def _ctx_attn_kernel(sink_ref, q_ref, k_ref, v_ref, o_ref):
    _attend(q_ref[...], k_ref[...], v_ref[...], None, sink_ref, o_ref)


def _context_attention(sink, q, k, v):
    return pl.pallas_call(
        _ctx_attn_kernel,
        grid_spec=pltpu.PrefetchScalarGridSpec(
            num_scalar_prefetch=1,
            grid=(BATCH,),
            in_specs=[pl.BlockSpec((SEQ, ATTN_WIDTH), lambda b, s: (b, 0)),
                      pl.BlockSpec((SEQ, KV_WIDTH), lambda b, s: (b, 0)),
                      pl.BlockSpec((SEQ, KV_WIDTH), lambda b, s: (b, 0))],
            out_specs=pl.BlockSpec((SEQ, ATTN_WIDTH), lambda b, s: (b, 0)),
        ),
        out_shape=jax.ShapeDtypeStruct((T_CTX, ATTN_WIDTH), F32),
        compiler_params=_params(("arbitrary",)),
        name="context_attention",
    )(sink, q, k, v)


def _rope_block(x, cos, sin_signed):
    lane = lax.broadcasted_iota(jnp.int32, (1, LANES), 1)
    first = (lane % (HEAD_DIM // 2)) < (HEAD_DIM // 4)
    swapped = jnp.where(first, pltpu.roll(x, LANES - HEAD_DIM // 4, 1), pltpu.roll(x, HEAD_DIM // 4, 1))
    return x * cos + swapped * sin_signed


LOCAL_KEYS = 3 * BLOCK


def _lat_attn_kernel(sink_ref, q_ref, k_ref, v_ref, ck_ref, cv_ref, cosq_ref, sinq_ref, cosk_ref, sin_k_ref,
                     o_ref):
    n = pl.program_id(1)
    start = pl.multiple_of(jnp.clip((n - 1) * BLOCK, 0, DEC_SEQ - LOCAL_KEYS), BLOCK)
    q = q_ref[...]
    q = jnp.concatenate(
        [_rope_block(q[:, j * LANES:(j + 1) * LANES], cosq_ref[:, j * LANES:(j + 1) * LANES],
                     sinq_ref[:, j * LANES:(j + 1) * LANES]) for j in range(ATTN_WIDTH // LANES)], axis=1)
    kw = _rope_block(k_ref[pl.ds(start, LOCAL_KEYS), :], cosk_ref[pl.ds(start, LOCAL_KEYS), :],
                     sin_k_ref[pl.ds(start, LOCAL_KEYS), :])
    vw = v_ref[pl.ds(start, LOCAL_KEYS), :]
    kall = jnp.concatenate([kw, ck_ref[0, 0]], axis=0)
    vall = jnp.concatenate([vw, cv_ref[0, 0]], axis=0)

    def valid_of(qrow, col):
        return jnp.logical_or(col >= LOCAL_KEYS, jnp.abs(n * BLOCK + qrow - (start + col)) <= WINDOW)

    _attend(q, kall, vall, valid_of, sink_ref, o_ref)


def _latent_attention(sink, q, k, v, cache_k, cache_v, layer, cosq, sinq, cosk, sin_k):
    nb = DEC_SEQ // BLOCK
    ctx_block0 = T_CTX // BLOCK
    ctx_seq0 = T_CTX // DEC_SEQ
    cache_spec = pl.BlockSpec((1, 1, PAST_LEN, KV_WIDTH), lambda b, n, s: (b, layer, 0, 0))
    return pl.pallas_call(
        _lat_attn_kernel,
        grid_spec=pltpu.PrefetchScalarGridSpec(
            num_scalar_prefetch=1,
            grid=(DEC_BATCH, nb),
            in_specs=[pl.BlockSpec((BLOCK, ATTN_WIDTH), lambda b, n, s: (ctx_block0 + b * nb + n, 0)),
                      pl.BlockSpec((DEC_SEQ, KV_WIDTH), lambda b, n, s: (ctx_seq0 + b, 0)),
                      pl.BlockSpec((DEC_SEQ, KV_WIDTH), lambda b, n, s: (ctx_seq0 + b, 0)),
                      cache_spec, cache_spec,
                      pl.BlockSpec((BLOCK, ATTN_WIDTH), lambda b, n, s: (n, 0)),
                      pl.BlockSpec((BLOCK, ATTN_WIDTH), lambda b, n, s: (n, 0)),
                      pl.BlockSpec((DEC_SEQ, KV_WIDTH), lambda b, n, s: (0, 0)),
                      pl.BlockSpec((DEC_SEQ, KV_WIDTH), lambda b, n, s: (0, 0))],
            out_specs=pl.BlockSpec((BLOCK, ATTN_WIDTH), lambda b, n, s: (b * nb + n, 0)),
        ),
        out_shape=jax.ShapeDtypeStruct((T_LAT, ATTN_WIDTH), F32),
        compiler_params=_params(("arbitrary", "arbitrary")),
        name="latent_attention",
    )(sink, q, k, v, cache_k, cache_v, cosq, sinq, cosk, sin_k)


def _rope_tables():
    t = jnp.arange(DEC_SEQ)
    nf = HEAD_DIM // 4
    inv = ROPE_BASE ** (-jnp.arange(nf, dtype=F32) / nf)

    def half(coord):
        ang = coord.astype(F32)[:, None] * inv[None, :]
        c, s = jnp.cos(ang), jnp.sin(ang)
        return jnp.concatenate([c, c], axis=1), jnp.concatenate([-s, s], axis=1)

    cr, sr = half(t // GRID_W)
    cc, sc = half(t % GRID_W)
    cos = jnp.concatenate([cr, cc], axis=1)
    sin = jnp.concatenate([sr, sc], axis=1)
    return (jnp.tile(cos, (1, N_Q_HEADS)), jnp.tile(sin, (1, N_Q_HEADS)),
            jnp.tile(cos, (1, N_KV_HEADS)), jnp.tile(sin, (1, N_KV_HEADS)))


def _ret_kernel(lg_ref, cd_ref, q_ref, k_ref, v_ref, sg_ref, gn_ref, *rest, n_chunks, has_s0, write_state):
    rest = list(rest)
    s0_ref = rest.pop(0) if has_s0 else None
    o_ref = rest.pop(0)
    sf_ref = rest.pop(0) if write_state else None
    acc_ref = rest.pop(0)
    row = lax.broadcasted_iota(jnp.int32, (RET_CHUNK, RET_CHUNK), 0).astype(F32)
    col = lax.broadcasted_iota(jnp.int32, (RET_CHUNK, RET_CHUNK), 1).astype(F32)
    rel = row - col
    pos = lax.broadcasted_iota(jnp.int32, (RET_CHUNK, 1), 0).astype(F32)

    def chunk(ref, c, cols):
        return ref[c * RET_CHUNK:(c + 1) * RET_CHUNK, cols]

    def inter(direction, h, c, state):
        cols = slice(h * RET_DK, (h + 1) * RET_DK)
        lg = lg_ref[direction * N_RET_HEADS + h]
        cd = cd_ref[direction * N_RET_HEADS + h]
        if direction == 0:
            q_dec = jnp.exp(lg * (pos + 1.0))
            k_dec = jnp.exp(lg * (RET_CHUNK - 1.0 - pos))
        else:
            q_dec = jnp.exp(lg * (RET_CHUNK - pos))
            k_dec = jnp.exp(lg * pos)
        vb = chunk(v_ref, c, cols).astype(BF16)
        grow = jnp.dot((chunk(k_ref, c, cols) * k_dec).T.astype(BF16), vb, preferred_element_type=F32)
        if state is None:
            return None, grow
        o = jnp.dot(chunk(q_ref, c, cols).astype(BF16), state.astype(BF16), preferred_element_type=F32) * q_dec
        return o, state * cd + grow

    for h in range(N_RET_HEADS):
        cols = slice(h * RET_DK, (h + 1) * RET_DK)
        lgf, lgb = lg_ref[h], lg_ref[N_RET_HEADS + h]
        intra = jnp.where(rel >= 0, jnp.exp(lgf * rel), 0.0) + jnp.where(rel <= 0, jnp.exp(-lgb * rel), 0.0)
        state = s0_ref[0, 0, 0, h] if has_s0 else None
        for c in range(n_chunks):
            qb, kb = chunk(q_ref, c, cols).astype(BF16), chunk(k_ref, c, cols).astype(BF16)
            scores = lax.dot_general(qb, kb, (((1,), (1,)), ((), ())), preferred_element_type=F32) * intra
            o = jnp.dot(scores.astype(BF16), chunk(v_ref, c, cols).astype(BF16), preferred_element_type=F32)
            o_fwd, state = inter(0, h, c, state)
            acc_ref[c * RET_CHUNK:(c + 1) * RET_CHUNK, cols] = o if o_fwd is None else o + o_fwd
        if write_state:
            sf_ref[0, 0, h] = state
        state = s0_ref[0, 0, 1, h] if has_s0 else None
        gn = gn_ref[:, cols]
        for c in range(n_chunks - 1, -1, -1):
            o_bwd, state = inter(1, h, c, state)
            tot = chunk(acc_ref, c, cols)
            if o_bwd is not None:
                tot = tot + o_bwd
            o_ref[c * RET_CHUNK:(c + 1) * RET_CHUNK, cols] = _rms_rows(tot, gn) * chunk(sg_ref, c, cols)
        if write_state:
            sf_ref[0, 1, h] = state


def _retention(lg, cd, rq, rk, rv, sg, gn, s0, layer, *, n_seq, seq_len, row0, write_state):
    blk0 = row0 // seq_len
    rows = pl.BlockSpec((seq_len, RET_WIDTH), lambda b, *_: (blk0 + b, 0))
    in_specs = [rows, rows, rows, rows, pl.BlockSpec((1, RET_WIDTH), lambda b, *_: (0, 0))]
    args = [rq, rk, rv, sg, gn]
    if s0 is not None:
        in_specs.append(pl.BlockSpec((1, 1, 2, N_RET_HEADS, RET_DK, RET_DK), lambda b, *_: (b, layer, 0, 0, 0, 0)))
        args.append(s0)
    out_specs = [pl.BlockSpec((seq_len, RET_WIDTH), lambda b, *_: (b, 0))]
    out_shape = [jax.ShapeDtypeStruct((n_seq * seq_len, RET_WIDTH), F32)]
    if write_state:
        out_specs.append(pl.BlockSpec((1, 2, N_RET_HEADS, RET_DK, RET_DK), lambda b, *_: (b, 0, 0, 0, 0)))
        out_shape.append(jax.ShapeDtypeStruct((n_seq, 2, N_RET_HEADS, RET_DK, RET_DK), F32))
    kern = functools.partial(_ret_kernel, n_chunks=seq_len // RET_CHUNK, has_s0=s0 is not None,
                             write_state=write_state)
    return pl.pallas_call(
        kern,
        grid_spec=pltpu.PrefetchScalarGridSpec(
            num_scalar_prefetch=2,
            grid=(n_seq,),
            in_specs=in_specs,
            out_specs=out_specs,
            scratch_shapes=[pltpu.VMEM((seq_len, RET_WIDTH), F32)],
        ),
        out_shape=out_shape,
        compiler_params=_params(("arbitrary",)),
        name="retention_ctx" if write_state else "retention_lat",
    )(lg, cd, *args)


def _outproj_kernel(attc_ref, attl_ref, retc_ref, retl_ref, xc_ref, xl_ref, wo_ref, g1_ref, sh2_ref, sc2_ref, nf_ref,
                    wrh_ref, wrl_ref, br_ref, x1_ref, h2t_ref, ti_ref, tw_ref):
    att = _group_rows(attc_ref, attl_ref)
    ret = _group_rows(retc_ref, retl_ref)
    y = (jnp.dot(att.astype(BF16), wo_ref[0:ATTN_WIDTH, :], preferred_element_type=F32)
         + jnp.dot(ret.astype(BF16), wo_ref[ATTN_WIDTH:, :], preferred_element_type=F32))
    x1 = _group_rows(xc_ref, xl_ref) + g1_ref[0] * y
    x1_ref[...] = x1
    h2 = _rms_rows(x1, nf_ref[...]) * (1.0 + sc2_ref[0]) + sh2_ref[0]
    for s in range(SLABS):
        h2t_ref[pl.ds(s, ROW_TILE, stride=SLABS), :] = h2[:, s * LANES:(s + 1) * LANES]
    hh = h2.astype(BF16)
    hl = (h2 - hh.astype(F32)).astype(BF16)
    wrh = wrh_ref[...]
    logits = (jnp.dot(hh, wrh, preferred_element_type=F32) + jnp.dot(hl, wrh, preferred_element_type=F32)
              + jnp.dot(hh, wrl_ref[...], preferred_element_type=F32) + br_ref[...])
    lane = lax.broadcasted_iota(jnp.int32, logits.shape, 1).astype(F32)
    vals, idxs = [], []
    cur = logits
    for _ in range(TOP_K):
        m = jnp.max(cur, axis=-1, keepdims=True)
        idx = jnp.min(jnp.where(cur == m, lane, float(LANES)), axis=-1, keepdims=True)
        vals.append(m)
        idxs.append(idx)
        cur = jnp.where(lane == idx, -jnp.inf, cur)
    es = [jnp.exp(v - vals[0]) for v in vals]
    den = es[0] + es[1] + es[2] + es[3]
    ti = jnp.zeros(logits.shape, F32)
    tw = jnp.zeros(logits.shape, F32)
    for k in range(TOP_K):
        ti = jnp.where(lane == float(k), idxs[k], ti)
        tw = jnp.where(lane == float(k), es[k] / den, tw)
    ti_ref[...] = ti.astype(jnp.int32)
    tw_ref[...] = tw


def _output_projection(att_c, att_l, ret_c, ret_l, x_ctx, x_lat, w_out_bf, mod_l, nf, wrh, wrl, br):
    rows = lambda w: pl.BlockSpec((ROW_TILE, w), lambda i: (i, 0))
    return pl.pallas_call(
        _outproj_kernel,
        grid=(T_ALL // ROW_TILE,),
        in_specs=[_ctx_rows(ATTN_WIDTH), _lat_rows(ATTN_WIDTH), _ctx_rows(RET_WIDTH), _lat_rows(RET_WIDTH),
                  _ctx_rows(D_MODEL), _lat_rows(D_MODEL), _full((D_MODEL, D_MODEL)),
                  _mod_spec(2), _mod_spec(3), _mod_spec(4), _full((1, D_MODEL)),
                  _full((D_MODEL, LANES)), _full((D_MODEL, LANES)), _full((1, LANES))],
        out_specs=[rows(D_MODEL), pl.BlockSpec((ROW_TILE * SLABS, LANES), lambda i: (i, 0)), rows(LANES), rows(LANES)],
        out_shape=[jax.ShapeDtypeStruct((T_ALL, D_MODEL), F32), jax.ShapeDtypeStruct((T_ALL * SLABS, LANES), F32),
                   jax.ShapeDtypeStruct((T_ALL, LANES), jnp.int32), jax.ShapeDtypeStruct((T_ALL, LANES), F32)],
        compiler_params=_params(("arbitrary",)),
        name="outproj_router",
    )(att_c, att_l, ret_c, ret_l, x_ctx, x_lat, w_out_bf, mod_l, mod_l, mod_l, nf, wrh, wrl, br)


def _token_tile(ref, t):
    return ref.at[pl.ds(pl.multiple_of(t * SLABS, SLABS), SLABS), :]


def _step_is_used(i, nused_ref):
    return i * MOE_STEP_BLOCKS < nused_ref[0]


def _smem_rows(width):
    return pl.BlockSpec((1, 1, width), lambda i, nused: (jnp.minimum(i, N_MOE_STEPS - 1), 0, 0),
                        memory_space=pltpu.SMEM)


X_SLOTS = 8
X_AHEAD = 6
Y_SLOTS = 4


TILE_ROWS = MOE_ROWS * SLABS


def _tile_block(ref, g):
    return ref.at[pl.ds(pl.multiple_of(g * TILE_ROWS, TILE_ROWS), TILE_ROWS), :]


def _experts_kernel(blk0_ref, nblk_ref, nused_ref, src_ref, h2t_hbm, rw_ref, wgu_ref, bgu_ref, wdn_ref, bdn_ref,
                    yst_hbm, xbuf, ybuf, wgu_bf, wdn_bf, xsem, ysem):
    e = pl.program_id(0)
    b0, nb, nused = blk0_ref[e], nblk_ref[e], nused_ref[0]

    def row_copy(g, slot, r):
        tok = jnp.minimum(src_ref[g * MOE_ROWS + r], T_ALL - 1)
        dst = xbuf.at[pl.ds(pl.multiple_of(slot * TILE_ROWS + r * SLABS, SLABS), SLABS), :]
        return pltpu.make_async_copy(_token_tile(h2t_hbm, tok), dst, xsem.at[slot])

    def gather_wait(slot):
        pltpu.make_async_copy(_tile_block(h2t_hbm, 0), _tile_block(xbuf, slot), xsem.at[slot]).wait()

    def y_copy(g):
        slot = g % Y_SLOTS
        return pltpu.make_async_copy(_tile_block(ybuf, slot), _tile_block(yst_hbm, g), ysem.at[slot])

    @pl.when(e == 0)
    def _():
        for k in range(X_AHEAD):
            @pl.when(k < nused)
            def _():
                def issue(r, carry):
                    row_copy(k, k, r).start()
                    return carry
                lax.fori_loop(0, MOE_ROWS, issue, 0)

    @pl.when(nb > 0)
    def _():
        wgu_bf[...] = wgu_ref[...].astype(BF16)
        wdn_bf[...] = wdn_ref[...].astype(BF16)

        def process(blocks):
            xs, rws = [], []
            for g in blocks:
                slot = g % X_SLOTS
                gather_wait(slot)
                xs.append(jnp.concatenate(
                    [xbuf[pl.ds(slot * TILE_ROWS + s, MOE_ROWS, stride=SLABS), :].astype(BF16)
                     for s in range(SLABS)], axis=1))
                rws.append(jnp.broadcast_to(rw_ref[pl.ds(g, 1), :], (SLABS, MOE_ROWS)).T[:, 0:1])

            for g in blocks:
                ahead = g + X_AHEAD
                ahead_slot = jnp.where(ahead < nused, ahead % X_SLOTS, X_SLOTS + ahead - nused)
                ahead_blk = jnp.minimum(ahead, nused - 1)
                for r in range(MOE_ROWS):
                    row_copy(ahead_blk, ahead_slot, r).start(priority=r % 2)

            x = xs[0] if len(xs) == 1 else jnp.concatenate(xs, axis=0)
            rw = rws[0] if len(rws) == 1 else jnp.concatenate(rws, axis=0)
            gu = jnp.dot(x, wgu_bf[...], preferred_element_type=F32) + bgu_ref[...]
            x_glu = jnp.minimum(gu[:, :D_FF], SWIGLU_LIMIT)
            x_lin = jnp.clip(gu[:, D_FF:], -SWIGLU_LIMIT, SWIGLU_LIMIT)
            act = x_glu * jax.nn.sigmoid(SWIGLU_ALPHA * x_glu) * (x_lin + 1.0)
            out = (jnp.dot(act.astype(BF16), wdn_bf[...], preferred_element_type=F32) + bdn_ref[...]) * rw

            for g in blocks:
                @pl.when(g >= Y_SLOTS)
                def _():
                    y_copy(g - Y_SLOTS).wait()

            for i, g in enumerate(blocks):
                yslot = g % Y_SLOTS
                for s in range(SLABS):
                    ybuf[pl.ds(yslot * TILE_ROWS + s, MOE_ROWS, stride=SLABS), :] = (
                        out[i * MOE_ROWS:(i + 1) * MOE_ROWS, s * LANES:(s + 1) * LANES])
                y_copy(g).start()

        def one(j, carry):
            process([b0 + j])
            return carry

        lax.fori_loop(0, nb, one, 0)

    @pl.when(e == N_EXPERTS - 1)
    def _():
        for k in range(X_AHEAD):
            @pl.when(k < nused)
            def _():
                gather_wait(X_SLOTS + X_AHEAD - 1 - k)

        for k in range(1, Y_SLOTS + 1):
            @pl.when(nused >= k)
            def _():
                y_copy(nused - k).wait()

        _tile_block(ybuf, 0)[...] = jnp.zeros((TILE_ROWS, LANES), F32)

        def fill(g, carry):
            cp = pltpu.make_async_copy(_tile_block(ybuf, 0), _tile_block(yst_hbm, g), ysem.at[0])
            cp.start()
            cp.wait()
            return carry

        lax.fori_loop(nused, N_MOE_BLOCKS, fill, 0)


def _moe_experts(blk0, nblk, n_used, src, h2t, row_w, w_gu, b_gu, w_dn, b_dn, layer):
    wspec = lambda rows, cols: pl.BlockSpec((None, None, rows, cols), lambda e, *_: (layer, e, 0, 0))
    return pl.pallas_call(
        _experts_kernel,
        grid_spec=pltpu.PrefetchScalarGridSpec(
            num_scalar_prefetch=4,
            grid=(N_EXPERTS,),
            in_specs=[pl.BlockSpec(memory_space=pl.ANY), _full((N_MOE_BLOCKS, MOE_ROWS)),
                      wspec(D_MODEL, 2 * D_FF), wspec(1, 2 * D_FF), wspec(D_FF, D_MODEL), wspec(1, D_MODEL)],
            out_specs=pl.BlockSpec(memory_space=pl.ANY),
            scratch_shapes=[pltpu.VMEM(((X_SLOTS + X_AHEAD) * TILE_ROWS, LANES), F32),
                            pltpu.VMEM((Y_SLOTS * TILE_ROWS, LANES), F32),
                            pltpu.VMEM((D_MODEL, 2 * D_FF), BF16), pltpu.VMEM((D_FF, D_MODEL), BF16),
                            pltpu.SemaphoreType.DMA((X_SLOTS + X_AHEAD,)), pltpu.SemaphoreType.DMA((Y_SLOTS,))],
        ),
        out_shape=jax.ShapeDtypeStruct((N_PAD * SLABS, LANES), F32),
        compiler_params=_params(("arbitrary",), vmem=EXPERT_VMEM_LIMIT),
        name="moe_experts",
    )(blk0, nblk, n_used, src, h2t, row_w, w_gu, b_gu.reshape(DEPTH, N_EXPERTS, 1, 2 * D_FF),
      w_dn, b_dn.reshape(DEPTH, N_EXPERTS, 1, D_MODEL))


N_ROW_TILES = T_ALL // ROW_TILE
ZERO_ROWS = ROW_TILE * SLABS


def _combine_kernel(nused_ref, off_ref, ys_ref, x1_ref, g2_ref, oc_ref, ol_ref, yres):
    i = pl.program_id(0)

    @pl.when(i == 0)
    def _():
        def zero(j, carry):
            yres[pl.ds(pl.multiple_of(j * ZERO_ROWS, ZERO_ROWS), ZERO_ROWS), :] = jnp.zeros((ZERO_ROWS, LANES), F32)
            return carry
        lax.fori_loop(0, N_ROW_TILES, zero, 0)
        _token_tile(yres, T_ALL)[...] = jnp.zeros((SLABS, LANES), F32)

    @pl.when(jnp.logical_and(i < N_MOE_STEPS, _step_is_used(i, nused_ref)))
    def _():
        def group(g, carry):
            rows = [g * GROUP + j for j in range(GROUP)]
            tiles = [yres.at[pl.ds(pl.multiple_of(off_ref[0, 0, r], SLABS), SLABS), :] for r in rows]
            new = [t[...] + _token_tile(ys_ref, r)[...] for t, r in zip(tiles, rows)]
            for t, v in zip(tiles, new):
                t[...] = v
            return carry

        lax.fori_loop(0, MOE_STEP_ROWS // GROUP, group, 0)

    def finalize(o_ref):
        base = (i - N_MOE_STEPS) * ZERO_ROWS
        for s in range(SLABS):
            cols = slice(s * LANES, (s + 1) * LANES)
            y = yres[pl.ds(base + s, ROW_TILE, stride=SLABS), :]
            o_ref[:, cols] = x1_ref[:, cols] + g2_ref[0][:, cols] * y

    @pl.when(jnp.logical_and(i >= N_MOE_STEPS, i < N_MOE_STEPS + N_CTX_TILES))
    def _():
        finalize(oc_ref)

    @pl.when(i >= N_MOE_STEPS + N_CTX_TILES)
    def _():
        finalize(ol_ref)


def _moe_combine(n_used, off3, ys, x1, mod_l):
    tile = lambda i: jnp.maximum(i - N_MOE_STEPS, 0)
    ctx_tile = lambda i: jnp.minimum(tile(i), N_CTX_TILES - 1)
    lat_tile = lambda i: jnp.maximum(tile(i) - N_CTX_TILES, 0)
    last_used = lambda nused: jnp.maximum(nused[0] - 1, 0) // MOE_STEP_BLOCKS
    return pl.pallas_call(
        _combine_kernel,
        grid_spec=pltpu.PrefetchScalarGridSpec(
            num_scalar_prefetch=1,
            grid=(N_MOE_STEPS + N_ROW_TILES,),
            in_specs=[_smem_rows(MOE_STEP_ROWS),
                      pl.BlockSpec((MOE_STEP_ROWS * SLABS, LANES), lambda i, nused: (jnp.minimum(i, last_used(nused)), 0)),
                      pl.BlockSpec((ROW_TILE, D_MODEL), lambda i, nused: (tile(i), 0)),
                      pl.BlockSpec((1, 1, D_MODEL), lambda i, nused: (_cond_of_tile(tile(i)), 0, 5))],
            out_specs=[pl.BlockSpec((ROW_TILE, D_MODEL), lambda i, nused: (ctx_tile(i), 0)),
                       pl.BlockSpec((ROW_TILE, D_MODEL), lambda i, nused: (lat_tile(i), 0))],
            scratch_shapes=[pltpu.VMEM((N_TILES * SLABS, LANES), F32)],
        ),
        out_shape=[jax.ShapeDtypeStruct((T_CTX, D_MODEL), F32), jax.ShapeDtypeStruct((T_LAT, D_MODEL), F32)],
        compiler_params=_params(("arbitrary",), vmem=EXPERT_VMEM_LIMIT),
        name="moe_combine",
    )(n_used, off3, ys, x1, mod_l)


def _routing_tables(top_idx, top_w):
    tok_bits = T_ALL.bit_length()
    flat_e = top_idx.reshape(N_ASSIGN)
    experts = jnp.arange(N_EXPERTS, dtype=jnp.int32)
    counts = jnp.sum((flat_e[:, None] == experts[None, :]).astype(jnp.int32), axis=0)
    pad = (-counts) % MOE_ROWS
    spare = jnp.arange(MOE_ROWS, dtype=jnp.int32)
    pad_e = jnp.where(spare[None, :] < pad[:, None], experts[:, None], N_EXPERTS).reshape(-1)
    keys = jnp.concatenate([flat_e * (1 << tok_bits) + jnp.arange(N_ASSIGN, dtype=jnp.int32) // TOP_K,
                            pad_e * (1 << tok_bits) + T_ALL])
    wts = jnp.concatenate([top_w.reshape(N_ASSIGN), jnp.zeros((N_PAD - N_ASSIGN,), F32)])
    keys, w_sorted = lax.sort((keys, wts), num_keys=1)
    src = keys & ((1 << tok_bits) - 1)
    padded = counts + pad
    blk_end = jnp.cumsum(padded) // MOE_ROWS
    nblk = padded // MOE_ROWS
    return (src, (src * SLABS).reshape(N_MOE_STEPS, 1, MOE_STEP_ROWS), w_sorted.reshape(N_MOE_BLOCKS, MOE_ROWS),
            (blk_end - nblk).astype(jnp.int32), nblk.astype(jnp.int32), blk_end[-1:].astype(jnp.int32))


def kernel(x_prompt, x_sample, cache_attn_k, cache_attn_v, state_ret, c, c_ctx, norm_mix, norm_ffn, w_ada, b_ada,
           w_in, q_norm, k_norm, attn_sink, ret_decay, ret_norm, w_out, w_router, b_router, w_gate_up, b_gate_up,
           w_down, b_down):
    x_ctx, x_lat = x_prompt.reshape(T_CTX, D_MODEL), x_sample.reshape(T_LAT, D_MODEL)
    cond = jnp.zeros((COND_PAD, D_MODEL), F32).at[0].set(c_ctx).at[1:N_COND].set(c)
    mod = _modulation(cond, w_ada, b_ada)[:, :N_COND].reshape(DEPTH, N_COND, 1, 6 * D_MODEL)

    cache_k = cache_attn_k.reshape(DEC_BATCH, DEPTH, PAST_LEN, KV_WIDTH)
    cache_v = cache_attn_v.reshape(DEC_BATCH, DEPTH, PAST_LEN, KV_WIDTH)
    cosq, sinq, cosk, sin_k = _rope_tables()
    grp = jnp.arange(ATTN_WIDTH) // HEAD_DIM
    avgq = jnp.where(grp[:, None] == grp[None, :], 1.0 / HEAD_DIM, 0.0).astype(BF16)
    avgk = avgq[:KV_WIDTH, :KV_WIDTH]
    log_gamma = jax.nn.log_sigmoid(ret_decay.astype(F32))
    chunk_decay = jnp.exp(log_gamma * RET_CHUNK)

    new_k, new_v, new_s = [], [], []
    for l in range(DEPTH):
        mod_l = mod[l]
        q, k, v, rq, rk, rv, sg = _input_projection(
            x_ctx, x_lat, norm_mix[l].reshape(1, D_MODEL), mod_l, w_in[l].astype(BF16),
            jnp.tile(q_norm[l], N_Q_HEADS).reshape(1, ATTN_WIDTH), jnp.tile(k_norm[l], N_KV_HEADS).reshape(1, KV_WIDTH),
            avgq, avgk)
        new_k.append(k[:T_CTX].reshape(BATCH, SEQ, N_KV_HEADS, HEAD_DIM))
        new_v.append(v[:T_CTX].reshape(BATCH, SEQ, N_KV_HEADS, HEAD_DIM))
        sink = attn_sink[l].astype(F32)
        att_c = _context_attention(sink, q, k, v)
        att_l = _latent_attention(sink, q, k, v, cache_k, cache_v, l, cosq, sinq, cosk, sin_k)
        lg = log_gamma[l].reshape(2 * N_RET_HEADS)
        cd = chunk_decay[l].reshape(2 * N_RET_HEADS)
        gn = ret_norm[l].reshape(1, RET_WIDTH)
        ret_c, s_fin = _retention(lg, cd, rq, rk, rv, sg, gn, None, l, n_seq=BATCH, seq_len=SEQ, row0=0,
                                  write_state=True)
        (ret_l,) = _retention(lg, cd, rq, rk, rv, sg, gn, state_ret, l, n_seq=DEC_BATCH, seq_len=DEC_SEQ,
                              row0=T_CTX, write_state=False)
        new_s.append(s_fin)
        wr = jnp.zeros((D_MODEL, LANES), F32).at[:, :N_EXPERTS].set(w_router[l])
        wrh = wr.astype(BF16)
        wrl = (wr - wrh.astype(F32)).astype(BF16)
        br = jnp.full((1, LANES), NEG_BIG, F32).at[0, :N_EXPERTS].set(b_router[l])
        x1, h2t, ti, tw = _output_projection(att_c, att_l, ret_c, ret_l, x_ctx, x_lat, w_out[l].astype(BF16), mod_l,
                                             norm_ffn[l].reshape(1, D_MODEL), wrh, wrl, br)
        src, off3, row_w, blk0, nblk, n_used = _routing_tables(ti[:, :TOP_K], tw[:, :TOP_K])
        ys = _moe_experts(blk0, nblk, n_used, src, h2t, row_w, w_gate_up, b_gate_up, w_down, b_down, l)
        x_ctx, x_lat = _moe_combine(n_used, off3, ys, x1, mod_l)

    y_prompt = x_ctx.reshape(BATCH, SEQ, D_MODEL)
    y_sample = x_lat.reshape(DEC_BATCH, DEC_SEQ, D_MODEL)
    return (y_prompt, y_sample, jnp.stack(new_k, axis=1), jnp.stack(new_v, axis=1), jnp.stack(new_s, axis=1))
```

```python
import functools

import jax
import jax.numpy as jnp
from jax import lax
from jax.experimental import pallas as pl
from jax.experimental.pallas import tpu as pltpu

F32 = jnp.float32
BF16 = jnp.bfloat16

D_MODEL = 1024
DEPTH = 4
BATCH, SEQ = 16, 256
DEC_BATCH, DEC_SEQ = 2, 1024
PAST_LEN = 512
GRID_W = 64
HEAD_DIM = 64
N_Q_HEADS = 8
N_KV_HEADS = 2
ATTN_WIDTH = N_Q_HEADS * HEAD_DIM
KV_WIDTH = N_KV_HEADS * HEAD_DIM
WINDOW = 128
BLOCK = 128
ROPE_BASE = 10000.0
N_RET_HEADS = 4
RET_DK = 128
RET_WIDTH = N_RET_HEADS * RET_DK
RET_CHUNK = 256
IN_WIDTH = ATTN_WIDTH + 2 * KV_WIDTH + 4 * RET_WIDTH
N_EXPERTS = 32
TOP_K = 4
D_FF = D_MODEL
SWIGLU_LIMIT = 7.0
SWIGLU_ALPHA = 1.702
EPS = 1e-6

T_CTX = BATCH * SEQ
T_LAT = DEC_BATCH * DEC_SEQ
T_ALL = T_CTX + T_LAT
N_COND = 1 + DEC_BATCH
COND_PAD = 8
LANES = 128
ROW_TILE = 512
N_CTX_TILES = T_CTX // ROW_TILE
MOE_ROWS = 128
N_ASSIGN = T_ALL * TOP_K
N_MOE_BLOCKS = N_ASSIGN // MOE_ROWS + N_EXPERTS
N_PAD = N_MOE_BLOCKS * MOE_ROWS
MOE_STEP_BLOCKS = 8
MOE_STEP_ROWS = MOE_STEP_BLOCKS * MOE_ROWS
N_MOE_STEPS = N_MOE_BLOCKS // MOE_STEP_BLOCKS
SLABS = D_MODEL // LANES
GROUP = 8
N_TILES = T_ALL + 1
NEG_BIG = -1e30
VMEM_LIMIT = 48 * 1024 * 1024
BIG_VMEM_LIMIT = 56 * 1024 * 1024


def _cond_of_tile(i):
    return jnp.where(i < N_CTX_TILES, 0, 1 + (i - N_CTX_TILES) // (DEC_SEQ // ROW_TILE))


def _params(sem, vmem=VMEM_LIMIT):
    return pltpu.CompilerParams(dimension_semantics=sem, vmem_limit_bytes=vmem)


def _mod_kernel(c_ref, w_ref, b_ref, o_ref):
    c = c_ref[...]
    s = (c * jax.nn.sigmoid(c)).astype(BF16)
    o_ref[0] = jnp.dot(s, w_ref[0].astype(BF16), preferred_element_type=F32) + b_ref[0]


def _modulation(cond, w_ada, b_ada):
    n_col = 6 * D_MODEL // D_MODEL
    return pl.pallas_call(
        _mod_kernel,
        grid=(DEPTH, n_col),
        in_specs=[
            pl.BlockSpec((COND_PAD, D_MODEL), lambda l, j: (0, 0)),
            pl.BlockSpec((1, D_MODEL, D_MODEL), lambda l, j: (l, 0, j)),
            pl.BlockSpec((1, 1, D_MODEL), lambda l, j: (l, 0, j)),
        ],
        out_specs=pl.BlockSpec((1, COND_PAD, D_MODEL), lambda l, j: (l, 0, j)),
        out_shape=jax.ShapeDtypeStruct((DEPTH, COND_PAD, 6 * D_MODEL), F32),
        compiler_params=_params(("arbitrary", "arbitrary")),
        name="modulation",
    )(cond, w_ada, b_ada.reshape(DEPTH, 1, 6 * D_MODEL))


def _rms_rows(x, g):
    ms = jnp.mean(x * x, axis=-1, keepdims=True)
    return x * lax.rsqrt(ms + EPS) * g


def _group_rmsnorm(a, avg_ref, g):
    sq = (a * a).astype(BF16)
    avg = avg_ref[...]
    ms = jnp.concatenate([jnp.dot(sq[:, j * LANES:(j + 1) * LANES], avg, preferred_element_type=F32)
                          for j in range(a.shape[1] // LANES)], axis=1)
    return a * lax.rsqrt(ms + EPS) * g


def _group_rows(ctx_ref, lat_ref):
    return jnp.where(pl.program_id(0) < N_CTX_TILES, ctx_ref[...], lat_ref[...])


def _ctx_rows(width):
    return pl.BlockSpec((ROW_TILE, width), lambda i: (jnp.minimum(i, N_CTX_TILES - 1), 0))


def _lat_rows(width):
    return pl.BlockSpec((ROW_TILE, width), lambda i: (jnp.maximum(i - N_CTX_TILES, 0), 0))


def _inproj_kernel(xc_ref, xl_ref, g_ref, sh_ref, sc_ref, w_ref, qn_ref, kn_ref, avg_ref,
                   q_ref, k_ref, v_ref, rq_ref, rk_ref, rv_ref, sg_ref):
    h = _rms_rows(_group_rows(xc_ref, xl_ref), g_ref[...]) * (1.0 + sc_ref[0]) + sh_ref[0]
    hb = h.astype(BF16)

    def proj(lo, width):
        return jnp.dot(hb, w_ref[:, lo:lo + width], preferred_element_type=F32)

    o = 0
    q_ref[...] = _group_rmsnorm(proj(o, ATTN_WIDTH), avg_ref, qn_ref[...])
    o += ATTN_WIDTH
    k_ref[...] = _group_rmsnorm(proj(o, KV_WIDTH), avg_ref, kn_ref[...])
    o += KV_WIDTH
    v_ref[...] = proj(o, KV_WIDTH)
    o += KV_WIDTH
    rq_ref[...] = proj(o, RET_WIDTH) * (RET_DK ** -0.5)
    o += RET_WIDTH
    rk_ref[...] = proj(o, RET_WIDTH)
    o += RET_WIDTH
    rv_ref[...] = proj(o, RET_WIDTH)
    o += RET_WIDTH
    rg = proj(o, RET_WIDTH)
    sg_ref[...] = rg * jax.nn.sigmoid(rg)


def _mod_spec(col):
    return pl.BlockSpec((1, 1, D_MODEL), lambda i, col=col: (_cond_of_tile(i), 0, col))


def _full(shape):
    return pl.BlockSpec(shape, lambda *_: (0,) * len(shape))


def _input_projection(x_ctx, x_lat, norm_g, mod_l, w_in_bf, qn, kn, avg):
    rows = lambda w: pl.BlockSpec((ROW_TILE, w), lambda i: (i, 0))
    widths = (ATTN_WIDTH, KV_WIDTH, KV_WIDTH, RET_WIDTH, RET_WIDTH, RET_WIDTH, RET_WIDTH)
    return pl.pallas_call(
        _inproj_kernel,
        grid=(T_ALL // ROW_TILE,),
        in_specs=[_ctx_rows(D_MODEL), _lat_rows(D_MODEL), _full((1, D_MODEL)), _mod_spec(0), _mod_spec(1),
                  _full((D_MODEL, IN_WIDTH)), _full((1, ATTN_WIDTH)), _full((1, KV_WIDTH)),
                  _full((LANES, LANES))],
        out_specs=[rows(w) for w in widths],
        out_shape=[jax.ShapeDtypeStruct((T_ALL, w), F32) for w in widths],
        compiler_params=_params(("arbitrary",)),
        name="norm_inproj",
    )(x_ctx, x_lat, norm_g, mod_l, mod_l, w_in_bf, qn, kn, avg)


def _attend(q, kall, vall, valid_of, sink_ref, o_ref):
    m_rows, n_keys = q.shape[0], kall.shape[0]
    scale = HEAD_DIM ** -0.5
    lane = lax.broadcasted_iota(jnp.int32, (1, LANES), 1)
    low = lane < HEAD_DIM
    row = lax.broadcasted_iota(jnp.int32, (2 * m_rows, 1), 0)
    second = row >= m_rows
    valid = None
    if valid_of is not None:
        qrow = lax.broadcasted_iota(jnp.int32, (2 * m_rows, n_keys), 0)
        qrow = jnp.where(qrow >= m_rows, qrow - m_rows, qrow)
        valid = valid_of(qrow, lax.broadcasted_iota(jnp.int32, (2 * m_rows, n_keys), 1))
    for g in range(N_KV_HEADS):
        keep = low if g == 0 else jnp.logical_not(low)
        kg = jnp.where(keep, kall, 0.0)
        vg = jnp.where(keep, vall, 0.0)
        kr = pltpu.roll(kg, HEAD_DIM, 1)
        vr = pltpu.roll(vg, HEAD_DIM, 1)
        k_at = (kg, kr) if g == 0 else (kr, kg)
        v_at = (vg, vr) if g == 0 else (vr, vg)
        kcat = jnp.concatenate(k_at, axis=0).astype(BF16)
        qg = jnp.concatenate([q[:, (2 * g + b) * LANES:(2 * g + b + 1) * LANES] for b in range(2)],
                             axis=0).astype(BF16)
        s_all = lax.dot_general(qg, kcat, (((1,), (1,)), ((), ())), preferred_element_type=F32) * scale
        acc = None
        for off in range(2):
            s = s_all[:, off * n_keys:(off + 1) * n_keys]
            if valid is not None:
                s = jnp.where(valid, s, NEG_BIG)
            h0 = 4 * g + off
            sink = jnp.where(second, sink_ref[h0 + 2], sink_ref[h0])
            m = jnp.maximum(jnp.max(s, axis=-1, keepdims=True), sink)
            e = jnp.exp(s - m)
            den = jnp.sum(e, axis=-1, keepdims=True) + jnp.exp(sink - m)
            o = jnp.dot(e.astype(BF16), v_at[off].astype(BF16), preferred_element_type=F32) / den
            acc = o if acc is None else acc + o
        for b in range(2):
            o_ref[:, (2 * g + b) * LANES:(2 * g + b + 1) * LANES] = acc[b * m_rows:(b + 1) * m_rows]


def _ctx_attn_kernel(sink_ref, q_ref, k_ref, v_ref, o_ref):
    _attend(q_ref[...], k_ref[...], v_ref[...], None, sink_ref, o_ref)


def _context_attention(sink, q, k, v):
    return pl.pallas_call(
        _ctx_attn_kernel,
        grid_spec=pltpu.PrefetchScalarGridSpec(
            num_scalar_prefetch=1,
            grid=(BATCH,),
            in_specs=[pl.BlockSpec((SEQ, ATTN_WIDTH), lambda b, s: (b, 0)),
                      pl.BlockSpec((SEQ, KV_WIDTH), lambda b, s: (b, 0)),
                      pl.BlockSpec((SEQ, KV_WIDTH), lambda b, s: (b, 0))],
            out_specs=pl.BlockSpec((SEQ, ATTN_WIDTH), lambda b, s: (b, 0)),
        ),
        out_shape=jax.ShapeDtypeStruct((T_CTX, ATTN_WIDTH), F32),
        compiler_params=_params(("arbitrary",)),
        name="context_attention",
    )(sink, q, k, v)


def _rope_block(x, cos, sin_signed):
    lane = lax.broadcasted_iota(jnp.int32, (1, LANES), 1)
    first = (lane % (HEAD_DIM // 2)) < (HEAD_DIM // 4)
    swapped = jnp.where(first, pltpu.roll(x, LANES - HEAD_DIM // 4, 1), pltpu.roll(x, HEAD_DIM // 4, 1))
    return x * cos + swapped * sin_signed


LOCAL_KEYS = 3 * BLOCK


def _lat_attn_kernel(sink_ref, q_ref, k_ref, v_ref, ck_ref, cv_ref, cosq_ref, sinq_ref, cosk_ref, sin_k_ref,
                     o_ref):
    n = pl.program_id(1)
    start = pl.multiple_of(jnp.clip((n - 1) * BLOCK, 0, DEC_SEQ - LOCAL_KEYS), BLOCK)
    q = q_ref[...]
    q = jnp.concatenate(
        [_rope_block(q[:, j * LANES:(j + 1) * LANES], cosq_ref[:, j * LANES:(j + 1) * LANES],
                     sinq_ref[:, j * LANES:(j + 1) * LANES]) for j in range(ATTN_WIDTH // LANES)], axis=1)
    kw = _rope_block(k_ref[pl.ds(start, LOCAL_KEYS), :], cosk_ref[pl.ds(start, LOCAL_KEYS), :],
                     sin_k_ref[pl.ds(start, LOCAL_KEYS), :])
    vw = v_ref[pl.ds(start, LOCAL_KEYS), :]
    kall = jnp.concatenate([kw, ck_ref[0, 0]], axis=0)
    vall = jnp.concatenate([vw, cv_ref[0, 0]], axis=0)

    def valid_of(qrow, col):
        return jnp.logical_or(col >= LOCAL_KEYS, jnp.abs(n * BLOCK + qrow - (start + col)) <= WINDOW)

    _attend(q, kall, vall, valid_of, sink_ref, o_ref)


def _latent_attention(sink, q, k, v, cache_k, cache_v, layer, cosq, sinq, cosk, sin_k):
    nb = DEC_SEQ // BLOCK
    ctx_block0 = T_CTX // BLOCK
    ctx_seq0 = T_CTX // DEC_SEQ
    cache_spec = pl.BlockSpec((1, 1, PAST_LEN, KV_WIDTH), lambda b, n, s: (b, layer, 0, 0))
    return pl.pallas_call(
        _lat_attn_kernel,
        grid_spec=pltpu.PrefetchScalarGridSpec(
            num_scalar_prefetch=1,
            grid=(DEC_BATCH, nb),
            in_specs=[pl.BlockSpec((BLOCK, ATTN_WIDTH), lambda b, n, s: (ctx_block0 + b * nb + n, 0)),
                      pl.BlockSpec((DEC_SEQ, KV_WIDTH), lambda b, n, s: (ctx_seq0 + b, 0)),
                      pl.BlockSpec((DEC_SEQ, KV_WIDTH), lambda b, n, s: (ctx_seq0 + b, 0)),
                      cache_spec, cache_spec,
                      pl.BlockSpec((BLOCK, ATTN_WIDTH), lambda b, n, s: (n, 0)),
                      pl.BlockSpec((BLOCK, ATTN_WIDTH), lambda b, n, s: (n, 0)),
                      pl.BlockSpec((DEC_SEQ, KV_WIDTH), lambda b, n, s: (0, 0)),
                      pl.BlockSpec((DEC_SEQ, KV_WIDTH), lambda b, n, s: (0, 0))],
            out_specs=pl.BlockSpec((BLOCK, ATTN_WIDTH), lambda b, n, s: (b * nb + n, 0)),
        ),
        out_shape=jax.ShapeDtypeStruct((T_LAT, ATTN_WIDTH), F32),
        compiler_params=_params(("arbitrary", "arbitrary")),
        name="latent_attention",
    )(sink, q, k, v, cache_k, cache_v, cosq, sinq, cosk, sin_k)


def _rope_tables():
    t = jnp.arange(DEC_SEQ)
    nf = HEAD_DIM // 4
    inv = ROPE_BASE ** (-jnp.arange(nf, dtype=F32) / nf)

    def half(coord):
        ang = coord.astype(F32)[:, None] * inv[None, :]
        c, s = jnp.cos(ang), jnp.sin(ang)
        return jnp.concatenate([c, c], axis=1), jnp.concatenate([-s, s], axis=1)

    cr, sr = half(t // GRID_W)
    cc, sc = half(t % GRID_W)
    cos = jnp.concatenate([cr, cc], axis=1)
    sin = jnp.concatenate([sr, sc], axis=1)
    return (jnp.tile(cos, (1, N_Q_HEADS)), jnp.tile(sin, (1, N_Q_HEADS)),
            jnp.tile(cos, (1, N_KV_HEADS)), jnp.tile(sin, (1, N_KV_HEADS)))


def _ret_kernel(lg_ref, cd_ref, q_ref, k_ref, v_ref, sg_ref, gn_ref, *rest, n_chunks, has_s0, write_state):
    rest = list(rest)
    s0_ref = rest.pop(0) if has_s0 else None
    o_ref = rest.pop(0)
    sf_ref = rest.pop(0) if write_state else None
    acc_ref = rest.pop(0)
    row = lax.broadcasted_iota(jnp.int32, (RET_CHUNK, RET_CHUNK), 0).astype(F32)
    col = lax.broadcasted_iota(jnp.int32, (RET_CHUNK, RET_CHUNK), 1).astype(F32)
    rel = row - col
    pos = lax.broadcasted_iota(jnp.int32, (RET_CHUNK, 1), 0).astype(F32)

    def chunk(ref, c, cols):
        return ref[c * RET_CHUNK:(c + 1) * RET_CHUNK, cols]

    def inter(direction, h, c, state):
        cols = slice(h * RET_DK, (h + 1) * RET_DK)
        lg = lg_ref[direction * N_RET_HEADS + h]
        cd = cd_ref[direction * N_RET_HEADS + h]
        if direction == 0:
            q_dec = jnp.exp(lg * (pos + 1.0))
            k_dec = jnp.exp(lg * (RET_CHUNK - 1.0 - pos))
        else:
            q_dec = jnp.exp(lg * (RET_CHUNK - pos))
            k_dec = jnp.exp(lg * pos)
        vb = chunk(v_ref, c, cols).astype(BF16)
        grow = jnp.dot((chunk(k_ref, c, cols) * k_dec).T.astype(BF16), vb, preferred_element_type=F32)
        if state is None:
            return None, grow
        o = jnp.dot(chunk(q_ref, c, cols).astype(BF16), state.astype(BF16), preferred_element_type=F32) * q_dec
        return o, state * cd + grow

    for h in range(N_RET_HEADS):
        cols = slice(h * RET_DK, (h + 1) * RET_DK)
        lgf, lgb = lg_ref[h], lg_ref[N_RET_HEADS + h]
        intra = jnp.where(rel >= 0, jnp.exp(lgf * rel), 0.0) + jnp.where(rel <= 0, jnp.exp(-lgb * rel), 0.0)
        state = s0_ref[0, 0, 0, h] if has_s0 else None
        for c in range(n_chunks):
            qb, kb = chunk(q_ref, c, cols).astype(BF16), chunk(k_ref, c, cols).astype(BF16)
            scores = lax.dot_general(qb, kb, (((1,), (1,)), ((), ())), preferred_element_type=F32) * intra
            o = jnp.dot(scores.astype(BF16), chunk(v_ref, c, cols).astype(BF16), preferred_element_type=F32)
            o_fwd, state = inter(0, h, c, state)
            acc_ref[c * RET_CHUNK:(c + 1) * RET_CHUNK, cols] = o if o_fwd is None else o + o_fwd
        if write_state:
            sf_ref[0, 0, h] = state
        state = s0_ref[0, 0, 1, h] if has_s0 else None
        gn = gn_ref[:, cols]
        for c in range(n_chunks - 1, -1, -1):
            o_bwd, state = inter(1, h, c, state)
            tot = chunk(acc_ref, c, cols)
            if o_bwd is not None:
                tot = tot + o_bwd
            o_ref[c * RET_CHUNK:(c + 1) * RET_CHUNK, cols] = _rms_rows(tot, gn) * chunk(sg_ref, c, cols)
        if write_state:
            sf_ref[0, 1, h] = state


def _retention(lg, cd, rq, rk, rv, sg, gn, s0, layer, *, n_seq, seq_len, row0, write_state):
    blk0 = row0 // seq_len
    rows = pl.BlockSpec((seq_len, RET_WIDTH), lambda b, *_: (blk0 + b, 0))
    in_specs = [rows, rows, rows, rows, pl.BlockSpec((1, RET_WIDTH), lambda b, *_: (0, 0))]
    args = [rq, rk, rv, sg, gn]
    if s0 is not None:
        in_specs.append(pl.BlockSpec((1, 1, 2, N_RET_HEADS, RET_DK, RET_DK), lambda b, *_: (b, layer, 0, 0, 0, 0)))
        args.append(s0)
    out_specs = [pl.BlockSpec((seq_len, RET_WIDTH), lambda b, *_: (b, 0))]
    out_shape = [jax.ShapeDtypeStruct((n_seq * seq_len, RET_WIDTH), F32)]
    if write_state:
        out_specs.append(pl.BlockSpec((1, 2, N_RET_HEADS, RET_DK, RET_DK), lambda b, *_: (b, 0, 0, 0, 0)))
        out_shape.append(jax.ShapeDtypeStruct((n_seq, 2, N_RET_HEADS, RET_DK, RET_DK), F32))
    kern = functools.partial(_ret_kernel, n_chunks=seq_len // RET_CHUNK, has_s0=s0 is not None,
                             write_state=write_state)
    return pl.pallas_call(
        kern,
        grid_spec=pltpu.PrefetchScalarGridSpec(
            num_scalar_prefetch=2,
            grid=(n_seq,),
            in_specs=in_specs,
            out_specs=out_specs,
            scratch_shapes=[pltpu.VMEM((seq_len, RET_WIDTH), F32)],
        ),
        out_shape=out_shape,
        compiler_params=_params(("arbitrary",)),
        name="retention_ctx" if write_state else "retention_lat",
    )(lg, cd, *args)


def _outproj_kernel(attc_ref, attl_ref, retc_ref, retl_ref, xc_ref, xl_ref, wo_ref, g1_ref, sh2_ref, sc2_ref, nf_ref,
                    wrh_ref, wrl_ref, br_ref, x1_ref, h2t_ref, ti_ref, tw_ref):
    att = _group_rows(attc_ref, attl_ref)
    ret = _group_rows(retc_ref, retl_ref)
    y = (jnp.dot(att.astype(BF16), wo_ref[0:ATTN_WIDTH, :], preferred_element_type=F32)
         + jnp.dot(ret.astype(BF16), wo_ref[ATTN_WIDTH:, :], preferred_element_type=F32))
    x1 = _group_rows(xc_ref, xl_ref) + g1_ref[0] * y
    x1_ref[...] = x1
    h2 = _rms_rows(x1, nf_ref[...]) * (1.0 + sc2_ref[0]) + sh2_ref[0]
    for s in range(SLABS):
        h2t_ref[pl.ds(s, ROW_TILE, stride=SLABS), :] = h2[:, s * LANES:(s + 1) * LANES]
    hh = h2.astype(BF16)
    hl = (h2 - hh.astype(F32)).astype(BF16)
    wrh = wrh_ref[...]
    logits = (jnp.dot(hh, wrh, preferred_element_type=F32) + jnp.dot(hl, wrh, preferred_element_type=F32)
              + jnp.dot(hh, wrl_ref[...], preferred_element_type=F32) + br_ref[...])
    lane = lax.broadcasted_iota(jnp.int32, logits.shape, 1).astype(F32)
    vals, idxs = [], []
    cur = logits
    for _ in range(TOP_K):
        m = jnp.max(cur, axis=-1, keepdims=True)
        idx = jnp.min(jnp.where(cur == m, lane, float(LANES)), axis=-1, keepdims=True)
        vals.append(m)
        idxs.append(idx)
        cur = jnp.where(lane == idx, -jnp.inf, cur)
    es = [jnp.exp(v - vals[0]) for v in vals]
    den = es[0] + es[1] + es[2] + es[3]
    ti = jnp.zeros(logits.shape, F32)
    tw = jnp.zeros(logits.shape, F32)
    for k in range(TOP_K):
        ti = jnp.where(lane == float(k), idxs[k], ti)
        tw = jnp.where(lane == float(k), es[k] / den, tw)
    ti_ref[...] = ti.astype(jnp.int32)
    tw_ref[...] = tw


def _output_projection(att_c, att_l, ret_c, ret_l, x_ctx, x_lat, w_out_bf, mod_l, nf, wrh, wrl, br):
    rows = lambda w: pl.BlockSpec((ROW_TILE, w), lambda i: (i, 0))
    return pl.pallas_call(
        _outproj_kernel,
        grid=(T_ALL // ROW_TILE,),
        in_specs=[_ctx_rows(ATTN_WIDTH), _lat_rows(ATTN_WIDTH), _ctx_rows(RET_WIDTH), _lat_rows(RET_WIDTH),
                  _ctx_rows(D_MODEL), _lat_rows(D_MODEL), _full((D_MODEL, D_MODEL)),
                  _mod_spec(2), _mod_spec(3), _mod_spec(4), _full((1, D_MODEL)),
                  _full((D_MODEL, LANES)), _full((D_MODEL, LANES)), _full((1, LANES))],
        out_specs=[rows(D_MODEL), pl.BlockSpec((ROW_TILE * SLABS, LANES), lambda i: (i, 0)), rows(LANES), rows(LANES)],
        out_shape=[jax.ShapeDtypeStruct((T_ALL, D_MODEL), F32), jax.ShapeDtypeStruct((T_ALL * SLABS, LANES), F32),
                   jax.ShapeDtypeStruct((T_ALL, LANES), jnp.int32), jax.ShapeDtypeStruct((T_ALL, LANES), F32)],
        compiler_params=_params(("arbitrary",)),
        name="outproj_router",
    )(att_c, att_l, ret_c, ret_l, x_ctx, x_lat, w_out_bf, mod_l, mod_l, mod_l, nf, wrh, wrl, br)


def _token_tile(ref, t):
    return ref.at[pl.ds(pl.multiple_of(t * SLABS, SLABS), SLABS), :]


def _step_is_used(i, nused_ref):
    return i * MOE_STEP_BLOCKS < nused_ref[0]


def _smem_rows(width):
    return pl.BlockSpec((1, 1, width), lambda i, nused: (jnp.minimum(i, N_MOE_STEPS - 1), 0, 0),
                        memory_space=pltpu.SMEM)


X_SLOTS = 8
X_AHEAD = 6
Y_SLOTS = 4


TILE_ROWS = MOE_ROWS * SLABS


def _tile_block(ref, g):
    return ref.at[pl.ds(pl.multiple_of(g * TILE_ROWS, TILE_ROWS), TILE_ROWS), :]


def _experts_kernel(blk0_ref, nblk_ref, nused_ref, src_ref, h2t_hbm, rw_ref, wgu_ref, bgu_ref, wdn_ref, bdn_ref,
                    yst_hbm, xbuf, ybuf, wgu_bf, wdn_bf, xsem, ysem):
    e = pl.program_id(0)
    b0, nb, nused = blk0_ref[e], nblk_ref[e], nused_ref[0]

    def row_copy(g, slot, r):
        tok = jnp.minimum(src_ref[g * MOE_ROWS + r], T_ALL - 1)
        dst = xbuf.at[pl.ds(pl.multiple_of(slot * TILE_ROWS + r * SLABS, SLABS), SLABS), :]
        return pltpu.make_async_copy(_token_tile(h2t_hbm, tok), dst, xsem.at[slot])

    def gather_wait(slot):
        pltpu.make_async_copy(_tile_block(h2t_hbm, 0), _tile_block(xbuf, slot), xsem.at[slot]).wait()

    def y_copy(g):
        slot = g % Y_SLOTS
        return pltpu.make_async_copy(_tile_block(ybuf, slot), _tile_block(yst_hbm, g), ysem.at[slot])

    @pl.when(e == 0)
    def _():
        for k in range(X_AHEAD):
            @pl.when(k < nused)
            def _():
                def issue(r, carry):
                    row_copy(k, k, r).start()
                    return carry
                lax.fori_loop(0, MOE_ROWS, issue, 0)

    @pl.when(nb > 0)
    def _():
        wgu_bf[...] = wgu_ref[...].astype(BF16)
        wdn_bf[...] = wdn_ref[...].astype(BF16)

        def process(blocks):
            xs, rws = [], []
            for g in blocks:
                slot = g % X_SLOTS
                gather_wait(slot)
                xs.append(jnp.concatenate(
                    [xbuf[pl.ds(slot * TILE_ROWS + s, MOE_ROWS, stride=SLABS), :].astype(BF16)
                     for s in range(SLABS)], axis=1))
                rws.append(jnp.broadcast_to(rw_ref[pl.ds(g, 1), :], (SLABS, MOE_ROWS)).T[:, 0:1])

            for g in blocks:
                ahead = g + X_AHEAD
                ahead_slot = jnp.where(ahead < nused, ahead % X_SLOTS, X_SLOTS + ahead - nused)
                ahead_blk = jnp.minimum(ahead, nused - 1)
                for r in range(MOE_ROWS):
                    row_copy(ahead_blk, ahead_slot, r).start(priority=r % 2)

            x = xs[0] if len(xs) == 1 else jnp.concatenate(xs, axis=0)
            rw = rws[0] if len(rws) == 1 else jnp.concatenate(rws, axis=0)
            gu = jnp.dot(x, wgu_bf[...], preferred_element_type=F32) + bgu_ref[...]
            x_glu = jnp.minimum(gu[:, :D_FF], SWIGLU_LIMIT)
            x_lin = jnp.clip(gu[:, D_FF:], -SWIGLU_LIMIT, SWIGLU_LIMIT)
            act = x_glu * jax.nn.sigmoid(SWIGLU_ALPHA * x_glu) * (x_lin + 1.0)
            out = (jnp.dot(act.astype(BF16), wdn_bf[...], preferred_element_type=F32) + bdn_ref[...]) * rw

            for g in blocks:
                @pl.when(g >= Y_SLOTS)
                def _():
                    y_copy(g - Y_SLOTS).wait()

            for i, g in enumerate(blocks):
                yslot = g % Y_SLOTS
                for s in range(SLABS):
                    ybuf[pl.ds(yslot * TILE_ROWS + s, MOE_ROWS, stride=SLABS), :] = (
                        out[i * MOE_ROWS:(i + 1) * MOE_ROWS, s * LANES:(s + 1) * LANES])
                y_copy(g).start()

        def one(j, carry):
            process([b0 + j])
            return carry

        lax.fori_loop(0, nb, one, 0)

    @pl.when(e == N_EXPERTS - 1)
    def _():
        for k in range(X_AHEAD):
            @pl.when(k < nused)
            def _():
                gather_wait(X_SLOTS + X_AHEAD - 1 - k)

        for k in range(1, Y_SLOTS + 1):
            @pl.when(nused >= k)
            def _():
                y_copy(nused - k).wait()

        _tile_block(ybuf, 0)[...] = jnp.zeros((TILE_ROWS, LANES), F32)

        def fill(g, carry):
            cp = pltpu.make_async_copy(_tile_block(ybuf, 0), _tile_block(yst_hbm, g), ysem.at[0])
            cp.start()
            cp.wait()
            return carry

        lax.fori_loop(nused, N_MOE_BLOCKS, fill, 0)


def _moe_experts(blk0, nblk, n_used, src, h2t, row_w, w_gu, b_gu, w_dn, b_dn, layer):
    wspec = lambda rows, cols: pl.BlockSpec((None, None, rows, cols), lambda e, *_: (layer, e, 0, 0))
    return pl.pallas_call(
        _experts_kernel,
        grid_spec=pltpu.PrefetchScalarGridSpec(
            num_scalar_prefetch=4,
            grid=(N_EXPERTS,),
            in_specs=[pl.BlockSpec(memory_space=pl.ANY), _full((N_MOE_BLOCKS, MOE_ROWS)),
                      wspec(D_MODEL, 2 * D_FF), wspec(1, 2 * D_FF), wspec(D_FF, D_MODEL), wspec(1, D_MODEL)],
            out_specs=pl.BlockSpec(memory_space=pl.ANY),
            scratch_shapes=[pltpu.VMEM(((X_SLOTS + X_AHEAD) * TILE_ROWS, LANES), F32),
                            pltpu.VMEM((Y_SLOTS * TILE_ROWS, LANES), F32),
                            pltpu.VMEM((D_MODEL, 2 * D_FF), BF16), pltpu.VMEM((D_FF, D_MODEL), BF16),
                            pltpu.SemaphoreType.DMA((X_SLOTS + X_AHEAD,)), pltpu.SemaphoreType.DMA((Y_SLOTS,))],
        ),
        out_shape=jax.ShapeDtypeStruct((N_PAD * SLABS, LANES), F32),
        compiler_params=_params(("arbitrary",), vmem=BIG_VMEM_LIMIT),
        name="moe_experts",
    )(blk0, nblk, n_used, src, h2t, row_w, w_gu, b_gu.reshape(DEPTH, N_EXPERTS, 1, 2 * D_FF),
      w_dn, b_dn.reshape(DEPTH, N_EXPERTS, 1, D_MODEL))


N_ROW_TILES = T_ALL // ROW_TILE
ZERO_ROWS = ROW_TILE * SLABS


def _combine_kernel(nused_ref, off_ref, ys_ref, x1_ref, g2_ref, oc_ref, ol_ref, yres):
    i = pl.program_id(0)

    @pl.when(i == 0)
    def _():
        def zero(j, carry):
            yres[pl.ds(pl.multiple_of(j * ZERO_ROWS, ZERO_ROWS), ZERO_ROWS), :] = jnp.zeros((ZERO_ROWS, LANES), F32)
            return carry
        lax.fori_loop(0, N_ROW_TILES, zero, 0)
        _token_tile(yres, T_ALL)[...] = jnp.zeros((SLABS, LANES), F32)

    @pl.when(jnp.logical_and(i < N_MOE_STEPS, _step_is_used(i, nused_ref)))
    def _():
        def group(g, carry):
            rows = [g * GROUP + j for j in range(GROUP)]
            tiles = [yres.at[pl.ds(pl.multiple_of(off_ref[0, 0, r], SLABS), SLABS), :] for r in rows]
            new = [t[...] + _token_tile(ys_ref, r)[...] for t, r in zip(tiles, rows)]
            for t, v in zip(tiles, new):
                t[...] = v
            return carry

        lax.fori_loop(0, MOE_STEP_ROWS // GROUP, group, 0)

    def finalize(o_ref):
        base = (i - N_MOE_STEPS) * ZERO_ROWS
        for s in range(SLABS):
            cols = slice(s * LANES, (s + 1) * LANES)
            y = yres[pl.ds(base + s, ROW_TILE, stride=SLABS), :]
            o_ref[:, cols] = x1_ref[:, cols] + g2_ref[0][:, cols] * y

    @pl.when(jnp.logical_and(i >= N_MOE_STEPS, i < N_MOE_STEPS + N_CTX_TILES))
    def _():
        finalize(oc_ref)

    @pl.when(i >= N_MOE_STEPS + N_CTX_TILES)
    def _():
        finalize(ol_ref)


def _moe_combine(n_used, off3, ys, x1, mod_l):
    tile = lambda i: jnp.maximum(i - N_MOE_STEPS, 0)
    ctx_tile = lambda i: jnp.minimum(tile(i), N_CTX_TILES - 1)
    lat_tile = lambda i: jnp.maximum(tile(i) - N_CTX_TILES, 0)
    last_used = lambda nused: jnp.maximum(nused[0] - 1, 0) // MOE_STEP_BLOCKS
    return pl.pallas_call(
        _combine_kernel,
        grid_spec=pltpu.PrefetchScalarGridSpec(
            num_scalar_prefetch=1,
            grid=(N_MOE_STEPS + N_ROW_TILES,),
            in_specs=[_smem_rows(MOE_STEP_ROWS),
                      pl.BlockSpec((MOE_STEP_ROWS * SLABS, LANES), lambda i, nused: (jnp.minimum(i, last_used(nused)), 0)),
                      pl.BlockSpec((ROW_TILE, D_MODEL), lambda i, nused: (tile(i), 0)),
                      pl.BlockSpec((1, 1, D_MODEL), lambda i, nused: (_cond_of_tile(tile(i)), 0, 5))],
            out_specs=[pl.BlockSpec((ROW_TILE, D_MODEL), lambda i, nused: (ctx_tile(i), 0)),
                       pl.BlockSpec((ROW_TILE, D_MODEL), lambda i, nused: (lat_tile(i), 0))],
            scratch_shapes=[pltpu.VMEM((N_TILES * SLABS, LANES), F32)],
        ),
        out_shape=[jax.ShapeDtypeStruct((T_CTX, D_MODEL), F32), jax.ShapeDtypeStruct((T_LAT, D_MODEL), F32)],
        compiler_params=_params(("arbitrary",), vmem=BIG_VMEM_LIMIT),
        name="moe_combine",
    )(n_used, off3, ys, x1, mod_l)


def _routing_tables(top_idx, top_w):
    tok_bits = T_ALL.bit_length()
    flat_e = top_idx.reshape(N_ASSIGN)
    experts = jnp.arange(N_EXPERTS, dtype=jnp.int32)
    counts = jnp.sum((flat_e[:, None] == experts[None, :]).astype(jnp.int32), axis=0)
    pad = (-counts) % MOE_ROWS
    spare = jnp.arange(MOE_ROWS, dtype=jnp.int32)
    pad_e = jnp.where(spare[None, :] < pad[:, None], experts[:, None], N_EXPERTS).reshape(-1)
    keys = jnp.concatenate([flat_e * (1 << tok_bits) + jnp.arange(N_ASSIGN, dtype=jnp.int32) // TOP_K,
                            pad_e * (1 << tok_bits) + T_ALL])
    wts = jnp.concatenate([top_w.reshape(N_ASSIGN), jnp.zeros((N_PAD - N_ASSIGN,), F32)])
    keys, w_sorted = lax.sort((keys, wts), num_keys=1)
    src = keys & ((1 << tok_bits) - 1)
    padded = counts + pad
    blk_end = jnp.cumsum(padded) // MOE_ROWS
    nblk = padded // MOE_ROWS
    return (src, (src * SLABS).reshape(N_MOE_STEPS, 1, MOE_STEP_ROWS), w_sorted.reshape(N_MOE_BLOCKS, MOE_ROWS),
            (blk_end - nblk).astype(jnp.int32), nblk.astype(jnp.int32), blk_end[-1:].astype(jnp.int32))


def kernel(x_prompt, x_sample, cache_attn_k, cache_attn_v, state_ret, c, c_ctx, norm_mix, norm_ffn, w_ada, b_ada,
           w_in, q_norm, k_norm, attn_sink, ret_decay, ret_norm, w_out, w_router, b_router, w_gate_up, b_gate_up,
           w_down, b_down):
    x_ctx, x_lat = x_prompt.reshape(T_CTX, D_MODEL), x_sample.reshape(T_LAT, D_MODEL)
    cond = jnp.zeros((COND_PAD, D_MODEL), F32).at[0].set(c_ctx).at[1:N_COND].set(c)
    mod = _modulation(cond, w_ada, b_ada)[:, :N_COND].reshape(DEPTH, N_COND, 1, 6 * D_MODEL)

    cache_k = cache_attn_k.reshape(DEC_BATCH, DEPTH, PAST_LEN, KV_WIDTH)
    cache_v = cache_attn_v.reshape(DEC_BATCH, DEPTH, PAST_LEN, KV_WIDTH)
    cosq, sinq, cosk, sin_k = _rope_tables()
    grp = jnp.arange(LANES) // HEAD_DIM
    head_avg = jnp.where(grp[:, None] == grp[None, :], 1.0 / HEAD_DIM, 0.0).astype(BF16)
    log_gamma = jax.nn.log_sigmoid(ret_decay.astype(F32))
    chunk_decay = jnp.exp(log_gamma * RET_CHUNK)

    new_k, new_v, new_s = [], [], []
    for l in range(DEPTH):
        mod_l = mod[l]
        q, k, v, rq, rk, rv, sg = _input_projection(
            x_ctx, x_lat, norm_mix[l].reshape(1, D_MODEL), mod_l, w_in[l].astype(BF16),
            jnp.tile(q_norm[l], N_Q_HEADS).reshape(1, ATTN_WIDTH), jnp.tile(k_norm[l], N_KV_HEADS).reshape(1, KV_WIDTH),
            head_avg)
        new_k.append(k[:T_CTX].reshape(BATCH, SEQ, N_KV_HEADS, HEAD_DIM))
        new_v.append(v[:T_CTX].reshape(BATCH, SEQ, N_KV_HEADS, HEAD_DIM))
        sink = attn_sink[l].astype(F32)
        att_c = _context_attention(sink, q, k, v)
        att_l = _latent_attention(sink, q, k, v, cache_k, cache_v, l, cosq, sinq, cosk, sin_k)
        lg = log_gamma[l].reshape(2 * N_RET_HEADS)
        cd = chunk_decay[l].reshape(2 * N_RET_HEADS)
        gn = ret_norm[l].reshape(1, RET_WIDTH)
        ret_c, s_fin = _retention(lg, cd, rq, rk, rv, sg, gn, None, l, n_seq=BATCH, seq_len=SEQ, row0=0,
                                  write_state=True)
        (ret_l,) = _retention(lg, cd, rq, rk, rv, sg, gn, state_ret, l, n_seq=DEC_BATCH, seq_len=DEC_SEQ,
                              row0=T_CTX, write_state=False)
        new_s.append(s_fin)
        wr = jnp.zeros((D_MODEL, LANES), F32).at[:, :N_EXPERTS].set(w_router[l])
        wrh = wr.astype(BF16)
        wrl = (wr - wrh.astype(F32)).astype(BF16)
        br = jnp.full((1, LANES), NEG_BIG, F32).at[0, :N_EXPERTS].set(b_router[l])
        x1, h2t, ti, tw = _output_projection(att_c, att_l, ret_c, ret_l, x_ctx, x_lat, w_out[l].astype(BF16), mod_l,
                                             norm_ffn[l].reshape(1, D_MODEL), wrh, wrl, br)
        src, off3, row_w, blk0, nblk, n_used = _routing_tables(ti[:, :TOP_K], tw[:, :TOP_K])
        ys = _moe_experts(blk0, nblk, n_used, src, h2t, row_w, w_gate_up, b_gate_up, w_down, b_down, l)
        x_ctx, x_lat = _moe_combine(n_used, off3, ys, x1, mod_l)

    y_prompt = x_ctx.reshape(BATCH, SEQ, D_MODEL)
    y_sample = x_lat.reshape(DEC_BATCH, DEC_SEQ, D_MODEL)
    return (y_prompt, y_sample, jnp.stack(new_k, axis=1), jnp.stack(new_v, axis=1), jnp.stack(new_s, axis=1))
```

```python
import functools

import jax
import jax.numpy as jnp
from jax import lax
from jax.experimental import pallas as pl
from jax.experimental.pallas import tpu as pltpu

F32 = jnp.float32
BF16 = jnp.bfloat16

D_MODEL = 1024
DEPTH = 4
BATCH, SEQ = 16, 256
DEC_BATCH, DEC_SEQ = 2, 1024
PAST_LEN = 512
GRID_W = 64
HEAD_DIM = 64
N_Q_HEADS = 8
N_KV_HEADS = 2
ATTN_WIDTH = N_Q_HEADS * HEAD_DIM
KV_WIDTH = N_KV_HEADS * HEAD_DIM
WINDOW = 128
BLOCK = 128
ROPE_BASE = 10000.0
N_RET_HEADS = 4
RET_DK = 128
RET_WIDTH = N_RET_HEADS * RET_DK
RET_CHUNK = 256
IN_WIDTH = ATTN_WIDTH + 2 * KV_WIDTH + 4 * RET_WIDTH
N_EXPERTS = 32
TOP_K = 4
D_FF = D_MODEL
SWIGLU_LIMIT = 7.0
SWIGLU_ALPHA = 1.702
EPS = 1e-6

T_CTX = BATCH * SEQ
T_LAT = DEC_BATCH * DEC_SEQ
T_ALL = T_CTX + T_LAT
N_COND = 1 + DEC_BATCH
COND_PAD = 8
LANES = 128
ROW_TILE = 512
N_CTX_TILES = T_CTX // ROW_TILE
MOE_ROWS = 128
N_ASSIGN = T_ALL * TOP_K
N_MOE_BLOCKS = N_ASSIGN // MOE_ROWS + N_EXPERTS
N_PAD = N_MOE_BLOCKS * MOE_ROWS
MOE_STEP_BLOCKS = 8
MOE_STEP_ROWS = MOE_STEP_BLOCKS * MOE_ROWS
N_MOE_STEPS = N_MOE_BLOCKS // MOE_STEP_BLOCKS
SLABS = D_MODEL // LANES
GROUP = 8
N_TILES = T_ALL + 1
NEG_BIG = -1e30
VMEM_LIMIT = 48 * 1024 * 1024
BIG_VMEM_LIMIT = 56 * 1024 * 1024


def _cond_of_tile(i):
    return jnp.where(i < N_CTX_TILES, 0, 1 + (i - N_CTX_TILES) // (DEC_SEQ // ROW_TILE))


def _params(sem, vmem=VMEM_LIMIT):
    return pltpu.CompilerParams(dimension_semantics=sem, vmem_limit_bytes=vmem)


def _mod_kernel(c_ref, w_ref, b_ref, o_ref):
    c = c_ref[...]
    s = (c * jax.nn.sigmoid(c)).astype(BF16)
    o_ref[0] = jnp.dot(s, w_ref[0].astype(BF16), preferred_element_type=F32) + b_ref[0]


def _modulation(cond, w_ada, b_ada):
    n_col = 6 * D_MODEL // D_MODEL
    return pl.pallas_call(
        _mod_kernel,
        grid=(DEPTH, n_col),
        in_specs=[
            pl.BlockSpec((COND_PAD, D_MODEL), lambda l, j: (0, 0)),
            pl.BlockSpec((1, D_MODEL, D_MODEL), lambda l, j: (l, 0, j)),
            pl.BlockSpec((1, 1, D_MODEL), lambda l, j: (l, 0, j)),
        ],
        out_specs=pl.BlockSpec((1, COND_PAD, D_MODEL), lambda l, j: (l, 0, j)),
        out_shape=jax.ShapeDtypeStruct((DEPTH, COND_PAD, 6 * D_MODEL), F32),
        compiler_params=_params(("arbitrary", "arbitrary")),
        name="modulation",
    )(cond, w_ada, b_ada.reshape(DEPTH, 1, 6 * D_MODEL))


def _rms_rows(x, g):
    ms = jnp.mean(x * x, axis=-1, keepdims=True)
    return x * lax.rsqrt(ms + EPS) * g


def _group_rmsnorm(a, avg_ref, g):
    sq = (a * a).astype(BF16)
    avg = avg_ref[...]
    ms = jnp.concatenate([jnp.dot(sq[:, j * LANES:(j + 1) * LANES], avg, preferred_element_type=F32)
                          for j in range(a.shape[1] // LANES)], axis=1)
    return a * lax.rsqrt(ms + EPS) * g


def _group_rows(ctx_ref, lat_ref):
    return jnp.where(pl.program_id(0) < N_CTX_TILES, ctx_ref[...], lat_ref[...])


def _ctx_rows(width):
    return pl.BlockSpec((ROW_TILE, width), lambda i: (jnp.minimum(i, N_CTX_TILES - 1), 0))


def _lat_rows(width):
    return pl.BlockSpec((ROW_TILE, width), lambda i: (jnp.maximum(i - N_CTX_TILES, 0), 0))


def _inproj_kernel(xc_ref, xl_ref, g_ref, sh_ref, sc_ref, w_ref, qn_ref, kn_ref, avg_ref,
                   q_ref, k_ref, v_ref, rq_ref, rk_ref, rv_ref, sg_ref):
    h = _rms_rows(_group_rows(xc_ref, xl_ref), g_ref[...]) * (1.0 + sc_ref[0]) + sh_ref[0]
    hb = h.astype(BF16)

    def proj(lo, width):
        return jnp.dot(hb, w_ref[:, lo:lo + width], preferred_element_type=F32)

    o = 0
    q_ref[...] = _group_rmsnorm(proj(o, ATTN_WIDTH), avg_ref, qn_ref[...])
    o += ATTN_WIDTH
    k_ref[...] = _group_rmsnorm(proj(o, KV_WIDTH), avg_ref, kn_ref[...])
    o += KV_WIDTH
    v_ref[...] = proj(o, KV_WIDTH)
    o += KV_WIDTH
    rq_ref[...] = proj(o, RET_WIDTH) * (RET_DK ** -0.5)
    o += RET_WIDTH
    rk_ref[...] = proj(o, RET_WIDTH)
    o += RET_WIDTH
    rv_ref[...] = proj(o, RET_WIDTH)
    o += RET_WIDTH
    rg = proj(o, RET_WIDTH)
    sg_ref[...] = rg * jax.nn.sigmoid(rg)


def _mod_spec(col):
    return pl.BlockSpec((1, 1, D_MODEL), lambda i, col=col: (_cond_of_tile(i), 0, col))


def _full(shape):
    return pl.BlockSpec(shape, lambda *_: (0,) * len(shape))


def _input_projection(x_ctx, x_lat, norm_g, mod_l, w_in_bf, qn, kn, avg):
    rows = lambda w: pl.BlockSpec((ROW_TILE, w), lambda i: (i, 0))
    widths = (ATTN_WIDTH, KV_WIDTH, KV_WIDTH, RET_WIDTH, RET_WIDTH, RET_WIDTH, RET_WIDTH)
    return pl.pallas_call(
        _inproj_kernel,
        grid=(T_ALL // ROW_TILE,),
        in_specs=[_ctx_rows(D_MODEL), _lat_rows(D_MODEL), _full((1, D_MODEL)), _mod_spec(0), _mod_spec(1),
                  _full((D_MODEL, IN_WIDTH)), _full((1, ATTN_WIDTH)), _full((1, KV_WIDTH)),
                  _full((LANES, LANES))],
        out_specs=[rows(w) for w in widths],
        out_shape=[jax.ShapeDtypeStruct((T_ALL, w), F32) for w in widths],
        compiler_params=_params(("arbitrary",)),
        name="norm_inproj",
    )(x_ctx, x_lat, norm_g, mod_l, mod_l, w_in_bf, qn, kn, avg)


def _attend(q, kall, vall, valid_of, sink_ref, o_ref):
    m_rows, n_keys = q.shape[0], kall.shape[0]
    scale = HEAD_DIM ** -0.5
    lane = lax.broadcasted_iota(jnp.int32, (1, LANES), 1)
    low = lane < HEAD_DIM
    row = lax.broadcasted_iota(jnp.int32, (2 * m_rows, 1), 0)
    second = row >= m_rows
    valid = None
    if valid_of is not None:
        qrow = lax.broadcasted_iota(jnp.int32, (2 * m_rows, n_keys), 0)
        qrow = jnp.where(qrow >= m_rows, qrow - m_rows, qrow)
        valid = valid_of(qrow, lax.broadcasted_iota(jnp.int32, (2 * m_rows, n_keys), 1))
    for g in range(N_KV_HEADS):
        keep = low if g == 0 else jnp.logical_not(low)
        kg = jnp.where(keep, kall, 0.0)
        vg = jnp.where(keep, vall, 0.0)
        kr = pltpu.roll(kg, HEAD_DIM, 1)
        vr = pltpu.roll(vg, HEAD_DIM, 1)
        k_at = (kg, kr) if g == 0 else (kr, kg)
        v_at = (vg, vr) if g == 0 else (vr, vg)
        kcat = jnp.concatenate(k_at, axis=0).astype(BF16)
        qg = jnp.concatenate([q[:, (2 * g + b) * LANES:(2 * g + b + 1) * LANES] for b in range(2)],
                             axis=0).astype(BF16)
        s_all = lax.dot_general(qg, kcat, (((1,), (1,)), ((), ())), preferred_element_type=F32) * scale
        acc = None
        for off in range(2):
            s = s_all[:, off * n_keys:(off + 1) * n_keys]
            if valid is not None:
                s = jnp.where(valid, s, NEG_BIG)
            h0 = 4 * g + off
            sink = jnp.where(second, sink_ref[h0 + 2], sink_ref[h0])
            m = jnp.maximum(jnp.max(s, axis=-1, keepdims=True), sink)
            e = jnp.exp(s - m)
            den = jnp.sum(e, axis=-1, keepdims=True) + jnp.exp(sink - m)
            o = jnp.dot(e.astype(BF16), v_at[off].astype(BF16), preferred_element_type=F32) / den
            acc = o if acc is None else acc + o
        for b in range(2):
            o_ref[:, (2 * g + b) * LANES:(2 * g + b + 1) * LANES] = acc[b * m_rows:(b + 1) * m_rows]


def _ctx_attn_kernel(sink_ref, q_ref, k_ref, v_ref, o_ref):
    _attend(q_ref[...], k_ref[...], v_ref[...], None, sink_ref, o_ref)


def _context_attention(sink, q, k, v):
    return pl.pallas_call(
        _ctx_attn_kernel,
        grid_spec=pltpu.PrefetchScalarGridSpec(
            num_scalar_prefetch=1,
            grid=(BATCH,),
            in_specs=[pl.BlockSpec((SEQ, ATTN_WIDTH), lambda b, s: (b, 0)),
                      pl.BlockSpec((SEQ, KV_WIDTH), lambda b, s: (b, 0)),
                      pl.BlockSpec((SEQ, KV_WIDTH), lambda b, s: (b, 0))],
            out_specs=pl.BlockSpec((SEQ, ATTN_WIDTH), lambda b, s: (b, 0)),
        ),
        out_shape=jax.ShapeDtypeStruct((T_CTX, ATTN_WIDTH), F32),
        compiler_params=_params(("arbitrary",)),
        name="context_attention",
    )(sink, q, k, v)


def _rope_block(x, cos, sin_signed):
    lane = lax.broadcasted_iota(jnp.int32, (1, LANES), 1)
    first = (lane % (HEAD_DIM // 2)) < (HEAD_DIM // 4)
    swapped = jnp.where(first, pltpu.roll(x, LANES - HEAD_DIM // 4, 1), pltpu.roll(x, HEAD_DIM // 4, 1))
    return x * cos + swapped * sin_signed


LOCAL_KEYS = 3 * BLOCK


def _lat_attn_kernel(sink_ref, q_ref, k_ref, v_ref, ck_ref, cv_ref, cosq_ref, sinq_ref, cosk_ref, sin_k_ref,
                     o_ref):
    n = pl.program_id(1)
    start = pl.multiple_of(jnp.clip((n - 1) * BLOCK, 0, DEC_SEQ - LOCAL_KEYS), BLOCK)
    q = q_ref[...]
    q = jnp.concatenate(
        [_rope_block(q[:, j * LANES:(j + 1) * LANES], cosq_ref[:, j * LANES:(j + 1) * LANES],
                     sinq_ref[:, j * LANES:(j + 1) * LANES]) for j in range(ATTN_WIDTH // LANES)], axis=1)
    kw = _rope_block(k_ref[pl.ds(start, LOCAL_KEYS), :], cosk_ref[pl.ds(start, LOCAL_KEYS), :],
                     sin_k_ref[pl.ds(start, LOCAL_KEYS), :])
    vw = v_ref[pl.ds(start, LOCAL_KEYS), :]
    kall = jnp.concatenate([kw, ck_ref[0, 0]], axis=0)
    vall = jnp.concatenate([vw, cv_ref[0, 0]], axis=0)

    def valid_of(qrow, col):
        return jnp.logical_or(col >= LOCAL_KEYS, jnp.abs(n * BLOCK + qrow - (start + col)) <= WINDOW)

    _attend(q, kall, vall, valid_of, sink_ref, o_ref)


def _latent_attention(sink, q, k, v, cache_k, cache_v, layer, cosq, sinq, cosk, sin_k):
    nb = DEC_SEQ // BLOCK
    ctx_block0 = T_CTX // BLOCK
    ctx_seq0 = T_CTX // DEC_SEQ
    cache_spec = pl.BlockSpec((1, 1, PAST_LEN, KV_WIDTH), lambda b, n, s: (b, layer, 0, 0))
    return pl.pallas_call(
        _lat_attn_kernel,
        grid_spec=pltpu.PrefetchScalarGridSpec(
            num_scalar_prefetch=1,
            grid=(DEC_BATCH, nb),
            in_specs=[pl.BlockSpec((BLOCK, ATTN_WIDTH), lambda b, n, s: (ctx_block0 + b * nb + n, 0)),
                      pl.BlockSpec((DEC_SEQ, KV_WIDTH), lambda b, n, s: (ctx_seq0 + b, 0)),
                      pl.BlockSpec((DEC_SEQ, KV_WIDTH), lambda b, n, s: (ctx_seq0 + b, 0)),
                      cache_spec, cache_spec,
                      pl.BlockSpec((BLOCK, ATTN_WIDTH), lambda b, n, s: (n, 0)),
                      pl.BlockSpec((BLOCK, ATTN_WIDTH), lambda b, n, s: (n, 0)),
                      pl.BlockSpec((DEC_SEQ, KV_WIDTH), lambda b, n, s: (0, 0)),
                      pl.BlockSpec((DEC_SEQ, KV_WIDTH), lambda b, n, s: (0, 0))],
            out_specs=pl.BlockSpec((BLOCK, ATTN_WIDTH), lambda b, n, s: (b * nb + n, 0)),
        ),
        out_shape=jax.ShapeDtypeStruct((T_LAT, ATTN_WIDTH), F32),
        compiler_params=_params(("arbitrary", "arbitrary")),
        name="latent_attention",
    )(sink, q, k, v, cache_k, cache_v, cosq, sinq, cosk, sin_k)


def _rope_tables():
    t = jnp.arange(DEC_SEQ)
    nf = HEAD_DIM // 4
    inv = ROPE_BASE ** (-jnp.arange(nf, dtype=F32) / nf)

    def half(coord):
        ang = coord.astype(F32)[:, None] * inv[None, :]
        c, s = jnp.cos(ang), jnp.sin(ang)
        return jnp.concatenate([c, c], axis=1), jnp.concatenate([-s, s], axis=1)

    cr, sr = half(t // GRID_W)
    cc, sc = half(t % GRID_W)
    cos = jnp.concatenate([cr, cc], axis=1)
    sin = jnp.concatenate([sr, sc], axis=1)
    return (jnp.tile(cos, (1, N_Q_HEADS)), jnp.tile(sin, (1, N_Q_HEADS)),
            jnp.tile(cos, (1, N_KV_HEADS)), jnp.tile(sin, (1, N_KV_HEADS)))


def _ret_kernel(lg_ref, cd_ref, q_ref, k_ref, v_ref, sg_ref, gn_ref, *rest, n_chunks, has_s0, write_state):
    rest = list(rest)
    s0_ref = rest.pop(0) if has_s0 else None
    o_ref = rest.pop(0)
    sf_ref = rest.pop(0) if write_state else None
    acc_ref = rest.pop(0)
    row = lax.broadcasted_iota(jnp.int32, (RET_CHUNK, RET_CHUNK), 0).astype(F32)
    col = lax.broadcasted_iota(jnp.int32, (RET_CHUNK, RET_CHUNK), 1).astype(F32)
    rel = row - col
    pos = lax.broadcasted_iota(jnp.int32, (RET_CHUNK, 1), 0).astype(F32)

    def chunk(ref, c, cols):
        return ref[c * RET_CHUNK:(c + 1) * RET_CHUNK, cols]

    def inter(direction, h, c, state):
        cols = slice(h * RET_DK, (h + 1) * RET_DK)
        lg = lg_ref[direction * N_RET_HEADS + h]
        cd = cd_ref[direction * N_RET_HEADS + h]
        if direction == 0:
            q_dec = jnp.exp(lg * (pos + 1.0))
            k_dec = jnp.exp(lg * (RET_CHUNK - 1.0 - pos))
        else:
            q_dec = jnp.exp(lg * (RET_CHUNK - pos))
            k_dec = jnp.exp(lg * pos)
        vb = chunk(v_ref, c, cols).astype(BF16)
        grow = jnp.dot((chunk(k_ref, c, cols) * k_dec).T.astype(BF16), vb, preferred_element_type=F32)
        if state is None:
            return None, grow
        o = jnp.dot(chunk(q_ref, c, cols).astype(BF16), state.astype(BF16), preferred_element_type=F32) * q_dec
        return o, state * cd + grow

    for h in range(N_RET_HEADS):
        cols = slice(h * RET_DK, (h + 1) * RET_DK)
        lgf, lgb = lg_ref[h], lg_ref[N_RET_HEADS + h]
        intra = jnp.where(rel >= 0, jnp.exp(lgf * rel), 0.0) + jnp.where(rel <= 0, jnp.exp(-lgb * rel), 0.0)
        state = s0_ref[0, 0, 0, h] if has_s0 else None
        for c in range(n_chunks):
            qb, kb = chunk(q_ref, c, cols).astype(BF16), chunk(k_ref, c, cols).astype(BF16)
            scores = lax.dot_general(qb, kb, (((1,), (1,)), ((), ())), preferred_element_type=F32) * intra
            o = jnp.dot(scores.astype(BF16), chunk(v_ref, c, cols).astype(BF16), preferred_element_type=F32)
            o_fwd, state = inter(0, h, c, state)
            acc_ref[c * RET_CHUNK:(c + 1) * RET_CHUNK, cols] = o if o_fwd is None else o + o_fwd
        if write_state:
            sf_ref[0, 0, h] = state
        state = s0_ref[0, 0, 1, h] if has_s0 else None
        gn = gn_ref[:, cols]
        for c in range(n_chunks - 1, -1, -1):
            o_bwd, state = inter(1, h, c, state)
            tot = chunk(acc_ref, c, cols)
            if o_bwd is not None:
                tot = tot + o_bwd
            o_ref[c * RET_CHUNK:(c + 1) * RET_CHUNK, cols] = _rms_rows(tot, gn) * chunk(sg_ref, c, cols)
        if write_state:
            sf_ref[0, 1, h] = state


def _retention(lg, cd, rq, rk, rv, sg, gn, s0, layer, *, n_seq, seq_len, row0, write_state):
    blk0 = row0 // seq_len
    rows = pl.BlockSpec((seq_len, RET_WIDTH), lambda b, *_: (blk0 + b, 0))
    in_specs = [rows, rows, rows, rows, pl.BlockSpec((1, RET_WIDTH), lambda b, *_: (0, 0))]
    args = [rq, rk, rv, sg, gn]
    if s0 is not None:
        in_specs.append(pl.BlockSpec((1, 1, 2, N_RET_HEADS, RET_DK, RET_DK), lambda b, *_: (b, layer, 0, 0, 0, 0)))
        args.append(s0)
    out_specs = [pl.BlockSpec((seq_len, RET_WIDTH), lambda b, *_: (b, 0))]
    out_shape = [jax.ShapeDtypeStruct((n_seq * seq_len, RET_WIDTH), F32)]
    if write_state:
        out_specs.append(pl.BlockSpec((1, 2, N_RET_HEADS, RET_DK, RET_DK), lambda b, *_: (b, 0, 0, 0, 0)))
        out_shape.append(jax.ShapeDtypeStruct((n_seq, 2, N_RET_HEADS, RET_DK, RET_DK), F32))
    kern = functools.partial(_ret_kernel, n_chunks=seq_len // RET_CHUNK, has_s0=s0 is not None,
                             write_state=write_state)
    return pl.pallas_call(
        kern,
        grid_spec=pltpu.PrefetchScalarGridSpec(
            num_scalar_prefetch=2,
            grid=(n_seq,),
            in_specs=in_specs,
            out_specs=out_specs,
            scratch_shapes=[pltpu.VMEM((seq_len, RET_WIDTH), F32)],
        ),
        out_shape=out_shape,
        compiler_params=_params(("arbitrary",)),
        name="retention_ctx" if write_state else "retention_lat",
    )(lg, cd, *args)


def _outproj_kernel(attc_ref, attl_ref, retc_ref, retl_ref, xc_ref, xl_ref, wo_ref, g1_ref, sh2_ref, sc2_ref, nf_ref,
                    wrh_ref, wrl_ref, br_ref, x1_ref, h2t_ref, ti_ref, tw_ref):
    att = _group_rows(attc_ref, attl_ref)
    ret = _group_rows(retc_ref, retl_ref)
    y = (jnp.dot(att.astype(BF16), wo_ref[0:ATTN_WIDTH, :], preferred_element_type=F32)
         + jnp.dot(ret.astype(BF16), wo_ref[ATTN_WIDTH:, :], preferred_element_type=F32))
    x1 = _group_rows(xc_ref, xl_ref) + g1_ref[0] * y
    x1_ref[...] = x1
    h2 = _rms_rows(x1, nf_ref[...]) * (1.0 + sc2_ref[0]) + sh2_ref[0]
    for s in range(SLABS):
        h2t_ref[pl.ds(s, ROW_TILE, stride=SLABS), :] = h2[:, s * LANES:(s + 1) * LANES]
    hh = h2.astype(BF16)
    hl = (h2 - hh.astype(F32)).astype(BF16)
    wrh = wrh_ref[...]
    logits = (jnp.dot(hh, wrh, preferred_element_type=F32) + jnp.dot(hl, wrh, preferred_element_type=F32)
              + jnp.dot(hh, wrl_ref[...], preferred_element_type=F32) + br_ref[...])
    lane = lax.broadcasted_iota(jnp.int32, logits.shape, 1).astype(F32)
    vals, idxs = [], []
    cur = logits
    for _ in range(TOP_K):
        m = jnp.max(cur, axis=-1, keepdims=True)
        idx = jnp.min(jnp.where(cur == m, lane, float(LANES)), axis=-1, keepdims=True)
        vals.append(m)
        idxs.append(idx)
        cur = jnp.where(lane == idx, -jnp.inf, cur)
    es = [jnp.exp(v - vals[0]) for v in vals]
    den = es[0] + es[1] + es[2] + es[3]
    ti = jnp.zeros(logits.shape, F32)
    tw = jnp.zeros(logits.shape, F32)
    for k in range(TOP_K):
        ti = jnp.where(lane == float(k), idxs[k], ti)
        tw = jnp.where(lane == float(k), es[k] / den, tw)
    ti_ref[...] = ti.astype(jnp.int32)
    tw_ref[...] = tw


def _output_projection(att_c, att_l, ret_c, ret_l, x_ctx, x_lat, w_out_bf, mod_l, nf, wrh, wrl, br):
    rows = lambda w: pl.BlockSpec((ROW_TILE, w), lambda i: (i, 0))
    return pl.pallas_call(
        _outproj_kernel,
        grid=(T_ALL // ROW_TILE,),
        in_specs=[_ctx_rows(ATTN_WIDTH), _lat_rows(ATTN_WIDTH), _ctx_rows(RET_WIDTH), _lat_rows(RET_WIDTH),
                  _ctx_rows(D_MODEL), _lat_rows(D_MODEL), _full((D_MODEL, D_MODEL)),
                  _mod_spec(2), _mod_spec(3), _mod_spec(4), _full((1, D_MODEL)),
                  _full((D_MODEL, LANES)), _full((D_MODEL, LANES)), _full((1, LANES))],
        out_specs=[rows(D_MODEL), pl.BlockSpec((ROW_TILE * SLABS, LANES), lambda i: (i, 0)), rows(LANES), rows(LANES)],
        out_shape=[jax.ShapeDtypeStruct((T_ALL, D_MODEL), F32), jax.ShapeDtypeStruct((T_ALL * SLABS, LANES), F32),
                   jax.ShapeDtypeStruct((T_ALL, LANES), jnp.int32), jax.ShapeDtypeStruct((T_ALL, LANES), F32)],
        compiler_params=_params(("arbitrary",)),
        name="outproj_router",
    )(att_c, att_l, ret_c, ret_l, x_ctx, x_lat, w_out_bf, mod_l, mod_l, mod_l, nf, wrh, wrl, br)


def _token_tile(ref, t):
    return ref.at[pl.ds(pl.multiple_of(t * SLABS, SLABS), SLABS), :]


def _step_is_used(i, nused_ref):
    return i * MOE_STEP_BLOCKS < nused_ref[0]


def _smem_rows(width):
    return pl.BlockSpec((1, 1, width), lambda i, nused: (jnp.minimum(i, N_MOE_STEPS - 1), 0, 0),
                        memory_space=pltpu.SMEM)


X_SLOTS = 8
X_AHEAD = 6
Y_SLOTS = 4


TILE_ROWS = MOE_ROWS * SLABS


def _tile_block(ref, g):
    return ref.at[pl.ds(pl.multiple_of(g * TILE_ROWS, TILE_ROWS), TILE_ROWS), :]


def _experts_kernel(blk0_ref, nblk_ref, nused_ref, src_ref, h2t_hbm, rw_ref, wgu_ref, bgu_ref, wdn_ref, bdn_ref,
                    yst_hbm, xbuf, ybuf, wgu_bf, wdn_bf, xsem, ysem):
    e = pl.program_id(0)
    b0, nb, nused = blk0_ref[e], nblk_ref[e], nused_ref[0]

    def row_copy(g, slot, r):
        tok = jnp.minimum(src_ref[g * MOE_ROWS + r], T_ALL - 1)
        dst = xbuf.at[pl.ds(pl.multiple_of(slot * TILE_ROWS + r * SLABS, SLABS), SLABS), :]
        return pltpu.make_async_copy(_token_tile(h2t_hbm, tok), dst, xsem.at[slot])

    def gather_wait(slot):
        pltpu.make_async_copy(_tile_block(h2t_hbm, 0), _tile_block(xbuf, slot), xsem.at[slot]).wait()

    def y_copy(g):
        slot = g % Y_SLOTS
        return pltpu.make_async_copy(_tile_block(ybuf, slot), _tile_block(yst_hbm, g), ysem.at[slot])

    @pl.when(e == 0)
    def _():
        for k in range(X_AHEAD):
            @pl.when(k < nused)
            def _():
                def issue(r, carry):
                    row_copy(k, k, r).start()
                    return carry
                lax.fori_loop(0, MOE_ROWS, issue, 0)

    @pl.when(nb > 0)
    def _():
        wgu_bf[...] = wgu_ref[...].astype(BF16)
        wdn_bf[...] = wdn_ref[...].astype(BF16)

        def process(blocks):
            xs, rws = [], []
            for g in blocks:
                slot = g % X_SLOTS
                gather_wait(slot)
                xs.append(jnp.concatenate(
                    [xbuf[pl.ds(slot * TILE_ROWS + s, MOE_ROWS, stride=SLABS), :].astype(BF16)
                     for s in range(SLABS)], axis=1))
                rws.append(jnp.broadcast_to(rw_ref[pl.ds(g, 1), :], (SLABS, MOE_ROWS)).T[:, 0:1])

            for g in blocks:
                ahead = g + X_AHEAD
                ahead_slot = jnp.where(ahead < nused, ahead % X_SLOTS, X_SLOTS + ahead - nused)
                ahead_blk = jnp.minimum(ahead, nused - 1)
                for r in range(MOE_ROWS):
                    row_copy(ahead_blk, ahead_slot, r).start()

            x = xs[0] if len(xs) == 1 else jnp.concatenate(xs, axis=0)
            rw = rws[0] if len(rws) == 1 else jnp.concatenate(rws, axis=0)
            gu = jnp.dot(x, wgu_bf[...], preferred_element_type=F32) + bgu_ref[...]
            x_glu = jnp.minimum(gu[:, :D_FF], SWIGLU_LIMIT)
            x_lin = jnp.clip(gu[:, D_FF:], -SWIGLU_LIMIT, SWIGLU_LIMIT)
            act = x_glu * jax.nn.sigmoid(SWIGLU_ALPHA * x_glu) * (x_lin + 1.0)
            out = (jnp.dot(act.astype(BF16), wdn_bf[...], preferred_element_type=F32) + bdn_ref[...]) * rw

            for g in blocks:
                @pl.when(g >= Y_SLOTS)
                def _():
                    y_copy(g - Y_SLOTS).wait()

            for i, g in enumerate(blocks):
                yslot = g % Y_SLOTS
                for s in range(SLABS):
                    ybuf[pl.ds(yslot * TILE_ROWS + s, MOE_ROWS, stride=SLABS), :] = (
                        out[i * MOE_ROWS:(i + 1) * MOE_ROWS, s * LANES:(s + 1) * LANES])
                y_copy(g).start()

        def one(j, carry):
            process([b0 + j])
            return carry

        lax.fori_loop(0, nb, one, 0)

    @pl.when(e == N_EXPERTS - 1)
    def _():
        for k in range(X_AHEAD):
            @pl.when(k < nused)
            def _():
                gather_wait(X_SLOTS + X_AHEAD - 1 - k)

        for k in range(1, Y_SLOTS + 1):
            @pl.when(nused >= k)
            def _():
                y_copy(nused - k).wait()

        _tile_block(ybuf, 0)[...] = jnp.zeros((TILE_ROWS, LANES), F32)

        def fill(g, carry):
            cp = pltpu.make_async_copy(_tile_block(ybuf, 0), _tile_block(yst_hbm, g), ysem.at[0])
            cp.start()
            cp.wait()
            return carry

        lax.fori_loop(nused, N_MOE_BLOCKS, fill, 0)


def _moe_experts(blk0, nblk, n_used, src, h2t, row_w, w_gu, b_gu, w_dn, b_dn, layer):
    wspec = lambda rows, cols: pl.BlockSpec((None, None, rows, cols), lambda e, *_: (layer, e, 0, 0))
    return pl.pallas_call(
        _experts_kernel,
        grid_spec=pltpu.PrefetchScalarGridSpec(
            num_scalar_prefetch=4,
            grid=(N_EXPERTS,),
            in_specs=[pl.BlockSpec(memory_space=pl.ANY), _full((N_MOE_BLOCKS, MOE_ROWS)),
                      wspec(D_MODEL, 2 * D_FF), wspec(1, 2 * D_FF), wspec(D_FF, D_MODEL), wspec(1, D_MODEL)],
            out_specs=pl.BlockSpec(memory_space=pl.ANY),
            scratch_shapes=[pltpu.VMEM(((X_SLOTS + X_AHEAD) * TILE_ROWS, LANES), F32),
                            pltpu.VMEM((Y_SLOTS * TILE_ROWS, LANES), F32),
                            pltpu.VMEM((D_MODEL, 2 * D_FF), BF16), pltpu.VMEM((D_FF, D_MODEL), BF16),
                            pltpu.SemaphoreType.DMA((X_SLOTS + X_AHEAD,)), pltpu.SemaphoreType.DMA((Y_SLOTS,))],
        ),
        out_shape=jax.ShapeDtypeStruct((N_PAD * SLABS, LANES), F32),
        compiler_params=_params(("arbitrary",), vmem=BIG_VMEM_LIMIT),
        name="moe_experts",
    )(blk0, nblk, n_used, src, h2t, row_w, w_gu, b_gu.reshape(DEPTH, N_EXPERTS, 1, 2 * D_FF),
      w_dn, b_dn.reshape(DEPTH, N_EXPERTS, 1, D_MODEL))


N_ROW_TILES = T_ALL // ROW_TILE
ZERO_ROWS = ROW_TILE * SLABS


def _combine_kernel(nused_ref, off_ref, ys_ref, x1_ref, g2_ref, oc_ref, ol_ref, yres):
    i = pl.program_id(0)

    @pl.when(i == 0)
    def _():
        def zero(j, carry):
            yres[pl.ds(pl.multiple_of(j * ZERO_ROWS, ZERO_ROWS), ZERO_ROWS), :] = jnp.zeros((ZERO_ROWS, LANES), F32)
            return carry
        lax.fori_loop(0, N_ROW_TILES, zero, 0)
        _token_tile(yres, T_ALL)[...] = jnp.zeros((SLABS, LANES), F32)

    @pl.when(jnp.logical_and(i < N_MOE_STEPS, _step_is_used(i, nused_ref)))
    def _():
        def group(g, carry):
            rows = [g * GROUP + j for j in range(GROUP)]
            tiles = [yres.at[pl.ds(pl.multiple_of(off_ref[0, 0, r], SLABS), SLABS), :] for r in rows]
            new = [t[...] + _token_tile(ys_ref, r)[...] for t, r in zip(tiles, rows)]
            for t, v in zip(tiles, new):
                t[...] = v
            return carry

        lax.fori_loop(0, MOE_STEP_ROWS // GROUP, group, 0)

    def finalize(o_ref):
        base = (i - N_MOE_STEPS) * ZERO_ROWS
        for s in range(SLABS):
            cols = slice(s * LANES, (s + 1) * LANES)
            y = yres[pl.ds(base + s, ROW_TILE, stride=SLABS), :]
            o_ref[:, cols] = x1_ref[:, cols] + g2_ref[0][:, cols] * y

    @pl.when(jnp.logical_and(i >= N_MOE_STEPS, i < N_MOE_STEPS + N_CTX_TILES))
    def _():
        finalize(oc_ref)

    @pl.when(i >= N_MOE_STEPS + N_CTX_TILES)
    def _():
        finalize(ol_ref)


def _moe_combine(n_used, off3, ys, x1, mod_l):
    tile = lambda i: jnp.maximum(i - N_MOE_STEPS, 0)
    ctx_tile = lambda i: jnp.minimum(tile(i), N_CTX_TILES - 1)
    lat_tile = lambda i: jnp.maximum(tile(i) - N_CTX_TILES, 0)
    last_used = lambda nused: jnp.maximum(nused[0] - 1, 0) // MOE_STEP_BLOCKS
    return pl.pallas_call(
        _combine_kernel,
        grid_spec=pltpu.PrefetchScalarGridSpec(
            num_scalar_prefetch=1,
            grid=(N_MOE_STEPS + N_ROW_TILES,),
            in_specs=[_smem_rows(MOE_STEP_ROWS),
                      pl.BlockSpec((MOE_STEP_ROWS * SLABS, LANES), lambda i, nused: (jnp.minimum(i, last_used(nused)), 0)),
                      pl.BlockSpec((ROW_TILE, D_MODEL), lambda i, nused: (tile(i), 0)),
                      pl.BlockSpec((1, 1, D_MODEL), lambda i, nused: (_cond_of_tile(tile(i)), 0, 5))],
            out_specs=[pl.BlockSpec((ROW_TILE, D_MODEL), lambda i, nused: (ctx_tile(i), 0)),
                       pl.BlockSpec((ROW_TILE, D_MODEL), lambda i, nused: (lat_tile(i), 0))],
            scratch_shapes=[pltpu.VMEM((N_TILES * SLABS, LANES), F32)],
        ),
        out_shape=[jax.ShapeDtypeStruct((T_CTX, D_MODEL), F32), jax.ShapeDtypeStruct((T_LAT, D_MODEL), F32)],
        compiler_params=_params(("arbitrary",), vmem=BIG_VMEM_LIMIT),
        name="moe_combine",
    )(n_used, off3, ys, x1, mod_l)


def _routing_tables(top_idx, top_w):
    tok_bits = T_ALL.bit_length()
    flat_e = top_idx.reshape(N_ASSIGN)
    experts = jnp.arange(N_EXPERTS, dtype=jnp.int32)
    counts = jnp.sum((flat_e[:, None] == experts[None, :]).astype(jnp.int32), axis=0)
    pad = (-counts) % MOE_ROWS
    spare = jnp.arange(MOE_ROWS, dtype=jnp.int32)
    pad_e = jnp.where(spare[None, :] < pad[:, None], experts[:, None], N_EXPERTS).reshape(-1)
    keys = jnp.concatenate([flat_e * (1 << tok_bits) + jnp.arange(N_ASSIGN, dtype=jnp.int32) // TOP_K,
                            pad_e * (1 << tok_bits) + T_ALL])
    wts = jnp.concatenate([top_w.reshape(N_ASSIGN), jnp.zeros((N_PAD - N_ASSIGN,), F32)])
    keys, w_sorted = lax.sort((keys, wts), num_keys=1)
    src = keys & ((1 << tok_bits) - 1)
    padded = counts + pad
    blk_end = jnp.cumsum(padded) // MOE_ROWS
    nblk = padded // MOE_ROWS
    return (src, (src * SLABS).reshape(N_MOE_STEPS, 1, MOE_STEP_ROWS), w_sorted.reshape(N_MOE_BLOCKS, MOE_ROWS),
            (blk_end - nblk).astype(jnp.int32), nblk.astype(jnp.int32), blk_end[-1:].astype(jnp.int32))


def kernel(x_prompt, x_sample, cache_attn_k, cache_attn_v, state_ret, c, c_ctx, norm_mix, norm_ffn, w_ada, b_ada,
           w_in, q_norm, k_norm, attn_sink, ret_decay, ret_norm, w_out, w_router, b_router, w_gate_up, b_gate_up,
           w_down, b_down):
    x_ctx, x_lat = x_prompt.reshape(T_CTX, D_MODEL), x_sample.reshape(T_LAT, D_MODEL)
    cond = jnp.zeros((COND_PAD, D_MODEL), F32).at[0].set(c_ctx).at[1:N_COND].set(c)
    mod = _modulation(cond, w_ada, b_ada)[:, :N_COND].reshape(DEPTH, N_COND, 1, 6 * D_MODEL)

    cache_k = cache_attn_k.reshape(DEC_BATCH, DEPTH, PAST_LEN, KV_WIDTH)
    cache_v = cache_attn_v.reshape(DEC_BATCH, DEPTH, PAST_LEN, KV_WIDTH)
    cosq, sinq, cosk, sin_k = _rope_tables()
    grp = jnp.arange(LANES) // HEAD_DIM
    head_avg = jnp.where(grp[:, None] == grp[None, :], 1.0 / HEAD_DIM, 0.0).astype(BF16)
    log_gamma = jax.nn.log_sigmoid(ret_decay.astype(F32))
    chunk_decay = jnp.exp(log_gamma * RET_CHUNK)

    new_k, new_v, new_s = [], [], []
    for l in range(DEPTH):
        mod_l = mod[l]
        q, k, v, rq, rk, rv, sg = _input_projection(
            x_ctx, x_lat, norm_mix[l].reshape(1, D_MODEL), mod_l, w_in[l].astype(BF16),
            jnp.tile(q_norm[l], N_Q_HEADS).reshape(1, ATTN_WIDTH), jnp.tile(k_norm[l], N_KV_HEADS).reshape(1, KV_WIDTH),
            head_avg)
        new_k.append(k[:T_CTX].reshape(BATCH, SEQ, N_KV_HEADS, HEAD_DIM))
        new_v.append(v[:T_CTX].reshape(BATCH, SEQ, N_KV_HEADS, HEAD_DIM))
        sink = attn_sink[l].astype(F32)
        att_c = _context_attention(sink, q, k, v)
        att_l = _latent_attention(sink, q, k, v, cache_k, cache_v, l, cosq, sinq, cosk, sin_k)
        lg = log_gamma[l].reshape(2 * N_RET_HEADS)
        cd = chunk_decay[l].reshape(2 * N_RET_HEADS)
        gn = ret_norm[l].reshape(1, RET_WIDTH)
        ret_c, s_fin = _retention(lg, cd, rq, rk, rv, sg, gn, None, l, n_seq=BATCH, seq_len=SEQ, row0=0,
                                  write_state=True)
        (ret_l,) = _retention(lg, cd, rq, rk, rv, sg, gn, state_ret, l, n_seq=DEC_BATCH, seq_len=DEC_SEQ,
                              row0=T_CTX, write_state=False)
        new_s.append(s_fin)
        wr = jnp.zeros((D_MODEL, LANES), F32).at[:, :N_EXPERTS].set(w_router[l])
        wrh = wr.astype(BF16)
        wrl = (wr - wrh.astype(F32)).astype(BF16)
        br = jnp.full((1, LANES), NEG_BIG, F32).at[0, :N_EXPERTS].set(b_router[l])
        x1, h2t, ti, tw = _output_projection(att_c, att_l, ret_c, ret_l, x_ctx, x_lat, w_out[l].astype(BF16), mod_l,
                                             norm_ffn[l].reshape(1, D_MODEL), wrh, wrl, br)
        src, off3, row_w, blk0, nblk, n_used = _routing_tables(ti[:, :TOP_K], tw[:, :TOP_K])
        ys = _moe_experts(blk0, nblk, n_used, src, h2t, row_w, w_gate_up, b_gate_up, w_down, b_down, l)
        x_ctx, x_lat = _moe_combine(n_used, off3, ys, x1, mod_l)

    y_prompt = x_ctx.reshape(BATCH, SEQ, D_MODEL)
    y_sample = x_lat.reshape(DEC_BATCH, DEC_SEQ, D_MODEL)
    return (y_prompt, y_sample, jnp.stack(new_k, axis=1), jnp.stack(new_v, axis=1), jnp.stack(new_s, axis=1))
```
